```python
import jax, jax.numpy as jnp
from jax import lax
import numpy as np

D_MODEL = 2048
BATCH = 4
SEQ = 4096
DEPTH = 2

GRID_W = 64
CTX_LEN = 256
EPS = 1e-6

MLA_HEADS = 8
MLA_NOPE = 128
MLA_ROPE = 64
MLA_V = 128
MLA_Q_RANK = 512
MLA_KV_RANK = 512
MLA_SCALE = (MLA_NOPE + MLA_ROPE) ** -0.5
ROPE_BASE = 10000.0

NA_HEADS = 4
NA_HEAD_DIM = 128
NA_KH = 8
NA_KW = 16
NA_SCALE = NA_HEAD_DIM ** -0.5

FN_GROUPS = 4
FN_CH = 128

MLA_WIDTH = MLA_HEADS * MLA_V
NA_WIDTH = NA_HEADS * NA_HEAD_DIM
FN_WIDTH = FN_GROUPS * FN_CH
MIX_WIDTH = MLA_WIDTH + NA_WIDTH + FN_WIDTH
IN_SPLITS = (MLA_Q_RANK, MLA_KV_RANK, MLA_ROPE, NA_WIDTH, NA_WIDTH, NA_WIDTH, FN_WIDTH)
IN_COLS = sum(IN_SPLITS)

D_FF = 5632
CONV_W = 3
Q_BLOCK = 128

kernel_name = 'hybrid_mla_natten_fnet_convffn_prefix'


def rmsnorm(x, g):
    xf = x.astype(jnp.float32)
    y = xf * lax.rsqrt(jnp.mean(xf * xf, axis=-1, keepdims=True) + EPS)
    return (y * g.astype(jnp.float32)).astype(x.dtype)


def split_columns(u):
    offs = np.cumsum(IN_SPLITS)[:-1].tolist()
    return jnp.split(u, offs, axis=-1)


def axial_angles(rows, cols):
    n = MLA_ROPE // 4
    freqs = ROPE_BASE ** (-jnp.arange(n, dtype=jnp.float32) / n)
    ang_r = rows.astype(jnp.float32)[:, None, None] * freqs
    ang_c = cols.astype(jnp.float32)[:, None, None] * freqs
    return ang_r, ang_c


def rotate(x, ang):
    n = x.shape[-1] // 2
    x1, x2 = x[..., :n], x[..., n:]
    cos, sin = jnp.cos(ang).astype(x.dtype), jnp.sin(ang).astype(x.dtype)
    return jnp.concatenate([x1 * cos - x2 * sin, x2 * cos + x1 * sin], axis=-1)


def axial_rope(x, ang_r, ang_c):
    h = x.shape[-1] // 2
    return jnp.concatenate([rotate(x[..., :h], ang_r), rotate(x[..., h:], ang_c)], axis=-1)


def mla_qkv(c_q, c_kv, k_r, g_q, w_uq, g_kv, w_ukv, angles):
    B, L, _ = c_q.shape
    q = (rmsnorm(c_q, g_q) @ w_uq).reshape(B, L, MLA_HEADS, MLA_NOPE + MLA_ROPE)
    kv = (rmsnorm(c_kv, g_kv) @ w_ukv).reshape(B, L, MLA_HEADS, MLA_NOPE + MLA_V)
    q_nope, q_rope = q[..., :MLA_NOPE], q[..., MLA_NOPE:]
    k_nope, v = kv[..., :MLA_NOPE], kv[..., MLA_NOPE:]
    k_rope = k_r[:, :, None, :]
    if angles is not None:
        q_rope = axial_rope(q_rope, *angles)
        k_rope = axial_rope(k_rope, *angles)
    q = jnp.concatenate([q_nope, q_rope], axis=-1)
    k = jnp.concatenate([k_nope, jnp.broadcast_to(k_rope, (B, L, MLA_HEADS, MLA_ROPE))], axis=-1)
    return q, k, v


def blocked_attention(q, k, v, scale):
    B, L, H, dq = q.shape
    dv = v.shape[-1]
    nb = L // Q_BLOCK
    qb = q.reshape(B, nb, Q_BLOCK, H, dq).transpose(1, 0, 2, 3, 4)

    def one_block(qi):
        s = jnp.einsum('bqhd,bkhd->bhqk', qi, k, preferred_element_type=jnp.float32) * scale
        p = jax.nn.softmax(s, axis=-1).astype(v.dtype)
        return jnp.einsum('bhqk,bkhd->bqhd', p, v)

    o = lax.map(one_block, qb)
    return o.transpose(1, 0, 2, 3, 4).reshape(B, L, H * dv)


def neighborhood_attention(q, k, v, k_ctx, v_ctx, rpb):
    B, S, H, d = q.shape
    rows = S // GRID_W
    kh = min(NA_KH, rows)
    qg = q.reshape(B, rows, GRID_W, H, d)
    kg = k.reshape(B, rows, GRID_W, H, d)
    vg = v.reshape(B, rows, GRID_W, H, d)
    w_idx = jnp.arange(GRID_W)
    col_start = jnp.clip(w_idx - NA_KW // 2, 0, GRID_W - NA_KW)
    col_idx = col_start[:, None] + jnp.arange(NA_KW)
    rel_col = col_idx - w_idx[:, None] + (NA_KW - 1)
    nwin = kh * NA_KW

    def one_row(r):
        rs = jnp.clip(r - kh // 2, 0, rows - kh)
        q_r = lax.dynamic_index_in_dim(qg, r, axis=1, keepdims=False)
        k_rows = lax.dynamic_slice_in_dim(kg, rs, kh, axis=1)
        v_rows = lax.dynamic_slice_in_dim(vg, rs, kh, axis=1)
        k_win = k_rows[:, :, col_idx]
        v_win = v_rows[:, :, col_idx]
        rel_row = rs + jnp.arange(kh) - r + (NA_KH - 1)
        bias = rpb[:, rel_row[None, :, None], rel_col[:, None, :]]
        s_win = jnp.einsum('bwhd,bawjhd->bhwaj', q_r, k_win, preferred_element_type=jnp.float32) * NA_SCALE
        s_win = s_win + bias.astype(jnp.float32)
        s_ctx = jnp.einsum('bwhd,bchd->bhwc', q_r, k_ctx, preferred_element_type=jnp.float32) * NA_SCALE
        s = jnp.concatenate([s_win.reshape(B, H, GRID_W, nwin), s_ctx], axis=-1)
        p = jax.nn.softmax(s, axis=-1).astype(v.dtype)
        p_win = p[..., :nwin].reshape(B, H, GRID_W, kh, NA_KW)
        p_ctx = p[..., nwin:]
        return (jnp.einsum('bhwaj,bawjhd->bwhd', p_win, v_win)
                + jnp.einsum('bhwc,bchd->bwhd', p_ctx, v_ctx))

    o = lax.map(one_row, jnp.arange(rows))
    return o.transpose(1, 0, 2, 3, 4).reshape(B, S, H * d)


def fourier_mix(f, w_fnet):
    B, L, _ = f.shape
    fg = f.reshape(B, L, FN_GROUPS, FN_CH).astype(jnp.float32)
    spec = jnp.fft.fft2(fg, axes=(1, 3), norm='ortho').real.astype(f.dtype)
    return jnp.einsum('blgc,gcd->blgd', spec, w_fnet).reshape(B, L, FN_WIDTH)


def conv_ffn(h, w_up, conv_w, conv_b, w_down):
    L = h.shape[1]
    u = h @ w_up
    up = jnp.pad(u, ((0, 0), (CONV_W // 2, CONV_W // 2), (0, 0)))
    u = sum(up[:, j:j + L] * conv_w[j] for j in range(CONV_W)) + conv_b
    gate, val = jnp.split(u, 2, axis=-1)
    return (jax.nn.silu(gate) * val) @ w_down


def heads(t, n_heads):
    B, L, _ = t.shape
    return t.reshape(B, L, n_heads, -1)


def setup_inputs(seed: int = 0) -> dict:
    key = jax.random.key(seed)
    ks = jax.random.split(key, 24)
    D = D_MODEL

    def nrm(k, shape, scale):
        return jax.random.normal(k, shape, jnp.float32) * scale

    return {
        'x': nrm(ks[0], (BATCH, SEQ, D), 1.0),
        'c': nrm(ks[1], (BATCH, D), 1.0),
        'ctx': nrm(ks[2], (BATCH, CTX_LEN, D), 1.0),
        'c_ctx': nrm(ks[3], (D,), 1.0),
        'w_mod': nrm(ks[4], (DEPTH, D, 6 * D), D ** -0.5),
        'b_mod': nrm(ks[5], (DEPTH, 6 * D), 0.02),
        'g_attn': 1.0 + nrm(ks[6], (DEPTH, D), 0.02),
        'g_ffn': 1.0 + nrm(ks[7], (DEPTH, D), 0.02),
        'w_in': nrm(ks[8], (DEPTH, D, IN_COLS), D ** -0.5),
        'g_q': 1.0 + nrm(ks[9], (DEPTH, MLA_Q_RANK), 0.02),
        'w_uq': nrm(ks[10], (DEPTH, MLA_Q_RANK, MLA_HEADS * (MLA_NOPE + MLA_ROPE)), MLA_Q_RANK ** -0.5),
        'g_kv': 1.0 + nrm(ks[11], (DEPTH, MLA_KV_RANK), 0.02),
        'w_ukv': nrm(ks[12], (DEPTH, MLA_KV_RANK, MLA_HEADS * (MLA_NOPE + MLA_V)), MLA_KV_RANK ** -0.5),
        'na_rpb': nrm(ks[13], (DEPTH, NA_HEADS, 2 * NA_KH - 1, 2 * NA_KW - 1), 0.1),
        'w_fnet': nrm(ks[14], (DEPTH, FN_GROUPS, FN_CH, FN_CH), FN_CH ** -0.5),
        'w_out': nrm(ks[15], (DEPTH, MIX_WIDTH, D), MIX_WIDTH ** -0.5),
        'w_up': nrm(ks[16], (DEPTH, D, 2 * D_FF), D ** -0.5),
        'conv_w': nrm(ks[17], (DEPTH, CONV_W, 2 * D_FF), CONV_W ** -0.5),
        'conv_b': nrm(ks[18], (DEPTH, 2 * D_FF), 0.02),
        'w_down': nrm(ks[19], (DEPTH, D_FF, D), D_FF ** -0.5),
        'g_final': 1.0 + nrm(ks[20], (D,), 0.02),
    }


def reference(x, c, ctx, c_ctx, w_mod, b_mod, g_attn, g_ffn, w_in, g_q, w_uq, g_kv, w_ukv,
              na_rpb, w_fnet, w_out, w_up, conv_w, conv_b, w_down, g_final):
    S = x.shape[1]
    pos = jnp.arange(S)
    angles = axial_angles(pos // GRID_W, pos % GRID_W)
    xc = ctx
    mod_in = jax.nn.silu(c)
    modc_in = jax.nn.silu(c_ctx)

    for l in range(DEPTH):
        ctx_out = l < DEPTH - 1
        sh1, sc1, gt1, sh2, sc2, gt2 = jnp.split((mod_in @ w_mod[l] + b_mod[l])[:, None, :], 6, axis=-1)
        csh1, csc1, cgt1, csh2, csc2, cgt2 = jnp.split(modc_in @ w_mod[l] + b_mod[l], 6, axis=-1)

        h = rmsnorm(x, g_attn[l]) * (1 + sc1) + sh1
        hc = rmsnorm(xc, g_attn[l]) * (1 + csc1) + csh1
        cq, ckv, kr, qn, kn, vn, f = split_columns(h @ w_in[l])
        cq_c, ckv_c, kr_c, qn_c, kn_c, vn_c, f_c = split_columns(hc @ w_in[l])

        q_m, k_m, v_m = mla_qkv(cq, ckv, kr, g_q[l], w_uq[l], g_kv[l], w_ukv[l], angles)
        q_mc, k_mc, v_mc = mla_qkv(cq_c, ckv_c, kr_c, g_q[l], w_uq[l], g_kv[l], w_ukv[l], None)
        o_mla = blocked_attention(q_m, jnp.concatenate([k_m, k_mc], axis=1),
                                  jnp.concatenate([v_m, v_mc], axis=1), MLA_SCALE)
        kn_c, vn_c = heads(kn_c, NA_HEADS), heads(vn_c, NA_HEADS)
        o_na = neighborhood_attention(heads(qn, NA_HEADS), heads(kn, NA_HEADS), heads(vn, NA_HEADS),
                                      kn_c, vn_c, na_rpb[l])
        o_fn = fourier_mix(f, w_fnet[l])
        x = x + gt1 * (jnp.concatenate([o_mla, o_na, o_fn], axis=-1) @ w_out[l])

        if ctx_out:
            o_mla_c = blocked_attention(q_mc, k_mc, v_mc, MLA_SCALE)
            o_na_c = blocked_attention(heads(qn_c, NA_HEADS), kn_c, vn_c, NA_SCALE)
            o_fn_c = fourier_mix(f_c, w_fnet[l])
            xc = xc + cgt1 * (jnp.concatenate([o_mla_c, o_na_c, o_fn_c], axis=-1) @ w_out[l])

        h2 = rmsnorm(x, g_ffn[l]) * (1 + sc2) + sh2
        x = x + gt2 * conv_ffn(h2, w_up[l], conv_w[l], conv_b[l], w_down[l])
        if ctx_out:
            h2c = rmsnorm(xc, g_ffn[l]) * (1 + csc2) + csh2
            xc = xc + cgt2 * conv_ffn(h2c, w_up[l], conv_w[l], conv_b[l], w_down[l])

    return rmsnorm(x, g_final)
```

```python
import functools

import numpy as np
import jax
import jax.numpy as jnp
from jax import lax
from jax.experimental import pallas as pl
from jax.experimental.pallas import tpu as pltpu

F32 = jnp.float32
BF16 = jnp.bfloat16

D_MODEL = 2048
GRID_W = 64
EPS = 1e-6

MLA_HEADS = 8
MLA_NOPE = 128
MLA_ROPE = 64
MLA_V = 128
MLA_Q_RANK = 512
MLA_KV_RANK = 512
MLA_QK_PAD = 256
MLA_SCALE = (MLA_NOPE + MLA_ROPE) ** -0.5
ROPE_BASE = 10000.0

NA_HEADS = 4
NA_HEAD_DIM = 128
NA_KH = 8
NA_KW = 16
NA_SCALE = NA_HEAD_DIM ** -0.5
NA_QROWS = 8
NA_KROWS = NA_QROWS + NA_KH - 1

FN_GROUPS = 4
FN_CH = 128
FN_WIDTH = FN_GROUPS * FN_CH
FN_K2 = 16

MLA_WIDTH = MLA_HEADS * MLA_V
NA_WIDTH = NA_HEADS * NA_HEAD_DIM
D_FF = 5632
CONV_W = 3

ROW_BLOCK = 512
HALO = 16
MASK_VALUE = -1e30
VMEM_LIMIT = 56 * 1024 * 1024


def _params(*sem):
    return pltpu.CompilerParams(dimension_semantics=sem, vmem_limit_bytes=VMEM_LIMIT)


def _dot(a, b):
    return jnp.dot(a, b, preferred_element_type=F32)


def _dot_nt(a, b):
    return lax.dot_general(a, b, (((1,), (1,)), ((), ())), preferred_element_type=F32)


def _split(x):
    hi = x.astype(BF16)
    lo = (x - hi.astype(F32)).astype(BF16)
    return hi, lo


def _dot3_left(m_hi, m_lo, x):
    x_hi, x_lo = _split(x)
    return _dot(m_hi, x_hi) + (_dot(m_hi, x_lo) + _dot(m_lo, x_hi))


def _dot3_right(x, m_hi, m_lo):
    x_hi, x_lo = _split(x)
    return _dot(x_hi, m_hi) + (_dot(x_lo, m_hi) + _dot(x_hi, m_lo))


def _sigmoid(x):
    return 1.0 / (1.0 + jnp.exp(-x))


def _rms(x, g):
    y = x * lax.rsqrt(jnp.mean(x * x, axis=-1, keepdims=True) + EPS)
    return y * g


def _normmod(x, g, sc, sh):
    return _rms(x, g) * (1.0 + sc) + sh


def _mod_kernel(c_ref, w_ref, b_ref, o_ref):
    c = c_ref[...]
    s = (c * _sigmoid(c)).astype(BF16)
    o_ref[...] = _dot(s, w_ref[...].astype(BF16)) + b_ref[...]


def _modulation(cin, w_mod, b_mod):
    depth, d, n = w_mod.shape
    bn = 1024
    return pl.pallas_call(
        _mod_kernel,
        grid=(depth, n // bn),
        in_specs=[
            pl.BlockSpec((8, d), lambda l, j: (0, 0)),
            pl.BlockSpec((None, d, bn), lambda l, j: (l, 0, j)),
            pl.BlockSpec((None, 1, bn), lambda l, j: (l, 0, j)),
        ],
        out_specs=pl.BlockSpec((None, 8, bn), lambda l, j: (l, 0, j)),
        out_shape=jax.ShapeDtypeStruct((depth, 8, n), F32),
        compiler_params=_params("arbitrary", "arbitrary"),
        name="modulation",
    )(cin, w_mod, b_mod.reshape(depth, 1, n))


def _inproj_kernel(x_ref, sh_ref, sc_ref, g_ref, w_ref, oc_ref, on_ref, of_ref, okr_ref):
    h = _normmod(x_ref[...], g_ref[...], sc_ref[...], sh_ref[...]).astype(BF16)
    oc_ref[...] = _dot(h, w_ref[:, 0:1024])
    on_ref[...] = _dot(h, w_ref[:, 1024:2560]).astype(BF16)
    of_ref[...] = _dot(h, w_ref[:, 2560:3072])
    okr_ref[...] = _dot(h, w_ref[:, 3072:3200])


def _mod_spec(k, blocks_per_batch):
    return pl.BlockSpec((None, 1, D_MODEL), lambda i, *_: (i // blocks_per_batch, 0, k))


def _inproj(xa, mod_l, g, w_p, blocks_per_batch):
    rows = xa.shape[0]
    bm = ROW_BLOCK
    row = lambda w: pl.BlockSpec((bm, w), lambda i: (i, 0))
    return pl.pallas_call(
        _inproj_kernel,
        grid=(rows // bm,),
        in_specs=[
            row(D_MODEL),
            _mod_spec(0, blocks_per_batch),
            _mod_spec(1, blocks_per_batch),
            pl.BlockSpec((1, D_MODEL), lambda i: (0, 0)),
            pl.BlockSpec(w_p.shape, lambda i: (0, 0), pipeline_mode=pl.Buffered(1)),
        ],
        out_specs=[row(1024), row(1536), row(512), row(128)],
        out_shape=[
            jax.ShapeDtypeStruct((rows, 1024), F32),
            jax.ShapeDtypeStruct((rows, 1536), BF16),
            jax.ShapeDtypeStruct((rows, 512), F32),
            jax.ShapeDtypeStruct((rows, 128), F32),
        ],
        compiler_params=_params("arbitrary"),
        name="in_projection",
    )(xa, mod_l, mod_l, g, w_p)


def _rope(t, cs):
    t = t * cs
    return t + pltpu.roll(t, 64, axis=1)


def _mla_up_kernel(c_ref, kr_ref, cs_ref, gq_ref, gkv_ref, wq_ref, wkv_ref, q_ref, k_ref, v_ref):
    cs = cs_ref[...]
    cq = _rms(c_ref[:, 0:MLA_Q_RANK], gq_ref[...]).astype(BF16)
    ckv = _rms(c_ref[:, MLA_Q_RANK:MLA_Q_RANK + MLA_KV_RANK], gkv_ref[...]).astype(BF16)
    q = _dot(cq, wq_ref[...])
    kv = _dot(ckv, wkv_ref[...])
    lane = lax.broadcasted_iota(jnp.int32, cs.shape, 1)
    k_rope = jnp.where(lane < MLA_ROPE, _rope(kr_ref[...], cs), 0.0).astype(BF16)
    for h in range(MLA_HEADS):
        o = h * MLA_QK_PAD
        q_ref[:, o:o + MLA_NOPE] = q[:, o:o + MLA_NOPE].astype(BF16)
        q_ref[:, o + MLA_NOPE:o + MLA_QK_PAD] = _rope(q[:, o + MLA_NOPE:o + MLA_QK_PAD], cs).astype(BF16)
        k_ref[:, o:o + MLA_NOPE] = kv[:, h * MLA_NOPE:(h + 1) * MLA_NOPE].astype(BF16)
        k_ref[:, o + MLA_NOPE:o + MLA_QK_PAD] = k_rope
    v_ref[...] = kv[:, MLA_HEADS * MLA_NOPE:].astype(BF16)


def _mla_up(oc, okr, cs_tab, g_q, g_kv, wq_p, wkv_p, lat_blocks, pos_blocks):
    rows = oc.shape[0]
    bm = ROW_BLOCK
    row = lambda w: pl.BlockSpec((bm, w), lambda i: (i, 0))
    const = lambda a: pl.BlockSpec(a.shape, lambda i: (0, 0))
    cs_spec = pl.BlockSpec((bm, 128), lambda i: (jnp.where(i < lat_blocks, i % pos_blocks, pos_blocks), 0))
    qk_w = MLA_HEADS * MLA_QK_PAD
    return pl.pallas_call(
        _mla_up_kernel,
        grid=(rows // bm,),
        in_specs=[row(1024), row(128), cs_spec, const(g_q), const(g_kv), const(wq_p), const(wkv_p)],
        out_specs=[row(qk_w), row(qk_w), row(MLA_WIDTH)],
        out_shape=[
            jax.ShapeDtypeStruct((rows, qk_w), BF16),
            jax.ShapeDtypeStruct((rows, qk_w), BF16),
            jax.ShapeDtypeStruct((rows, MLA_WIDTH), BF16),
        ],
        compiler_params=_params("arbitrary"),
        name="mla_up_projection",
    )(oc, okr, cs_tab, g_q, g_kv, wq_p, wkv_p)


def _attention_kernel(*refs, scale, chunks):
    n_src = (len(refs) - 3) // 2
    q_ref, o_ref, s_ref = refs[0], refs[-2], refs[-1]
    k_refs = refs[1:1 + n_src]
    v_refs = refs[1 + n_src:1 + 2 * n_src]
    q = q_ref[...]
    m = None
    off = 0
    for src, start, size in chunks:
        s = _dot_nt(q, k_refs[src][start:start + size, :]) * scale
        s_ref[:, off:off + size] = s
        cm = jnp.max(s, axis=-1, keepdims=True)
        m = cm if m is None else jnp.maximum(m, cm)
        off += size
    l = None
    acc = None
    off = 0
    for src, start, size in chunks:
        p = jnp.exp(s_ref[:, off:off + size] - m)
        ps = jnp.sum(p, axis=-1, keepdims=True)
        pv = _dot(p.astype(BF16), v_refs[src][start:start + size, :])
        l = ps if l is None else l + ps
        acc = pv if acc is None else acc + pv
        off += size
    o_ref[...] = (acc / l).astype(o_ref.dtype)


def _attention(q_arr, k_arr, v_arr, *, batch, heads, dq, dv, bq, nq, q_block0, sources, k_col0, v_col0,
               scale, out_rows, name, chunk=512):
    chunks = []
    for s_idx, (_, size) in enumerate(sources):
        for start in range(0, size, chunk):
            chunks.append((s_idx, start, min(chunk, size - start)))
    total = sum(size for _, size in sources)
    in_specs = [pl.BlockSpec((bq, dq), lambda b, h, i: (q_block0(b) + i, h))]
    for fn, size in sources:
        in_specs.append(pl.BlockSpec((size, dq), lambda b, h, i, fn=fn: (fn(b), k_col0 + h)))
    for fn, size in sources:
        in_specs.append(pl.BlockSpec((size, dv), lambda b, h, i, fn=fn: (fn(b), v_col0 + h)))
    return pl.pallas_call(
        functools.partial(_attention_kernel, scale=scale, chunks=tuple(chunks)),
        grid=(batch, heads, nq),
        in_specs=in_specs,
        out_specs=pl.BlockSpec((bq, dv), lambda b, h, i: (b * nq + i, h)),
        out_shape=jax.ShapeDtypeStruct((out_rows, heads * dv), BF16),
        scratch_shapes=[pltpu.VMEM((bq, total), F32)],
        compiler_params=_params("arbitrary", "arbitrary", "arbitrary"),
        name=name,
    )(q_arr, *([k_arr] * len(sources)), *([v_arr] * len(sources)))


def _na_kernel(q_ref, k_ref, v_ref, kc_ref, vc_ref, t_ref, o_ref, *, grid_rows):
    r = pl.program_id(2)
    ks = jnp.clip(r * NA_QROWS - NA_KH // 2, 0, grid_rows - NA_KROWS)
    start = pl.multiple_of(ks * GRID_W, GRID_W)
    nwin = NA_KROWS * GRID_W
    q = q_ref[...]
    s_win = _dot_nt(q, k_ref[pl.ds(start, nwin), :]) * NA_SCALE + t_ref[...]
    s_ctx = _dot_nt(q, kc_ref[...]) * NA_SCALE
    m = jnp.maximum(jnp.max(s_win, axis=-1, keepdims=True), jnp.max(s_ctx, axis=-1, keepdims=True))
    p_win = jnp.exp(s_win - m)
    p_ctx = jnp.exp(s_ctx - m)
    l = jnp.sum(p_win, axis=-1, keepdims=True) + jnp.sum(p_ctx, axis=-1, keepdims=True)
    acc = _dot(p_win.astype(BF16), v_ref[pl.ds(start, nwin), :]) + _dot(p_ctx.astype(BF16), vc_ref[...])
    o_ref[...] = (acc / l).astype(o_ref.dtype)


def _na_table(rpb, grid_rows):
    tabs = []
    for r0 in (0, NA_QROWS, grid_rows - NA_QROWS):
        ks = min(max(r0 - NA_KH // 2, 0), grid_rows - NA_KROWS)
        r = r0 + np.arange(NA_QROWS)[:, None, None, None]
        w = np.arange(GRID_W)[None, :, None, None]
        kr = ks + np.arange(NA_KROWS)[None, None, :, None]
        j = np.arange(GRID_W)[None, None, None, :]
        rs = np.clip(r - NA_KH // 2, 0, grid_rows - NA_KH)
        cs = np.clip(w - NA_KW // 2, 0, GRID_W - NA_KW)
        valid = (kr >= rs) & (kr < rs + NA_KH) & (j >= cs) & (j < cs + NA_KW)
        rel_r = np.clip(kr - r + (NA_KH - 1), 0, 2 * NA_KH - 2)
        rel_c = np.clip(j - w + (NA_KW - 1), 0, 2 * NA_KW - 2)
        rel_r, rel_c, valid = np.broadcast_arrays(rel_r, rel_c, valid)
        bias = rpb[:, rel_r, rel_c].astype(F32)
        tab = jnp.where(valid[None], bias, MASK_VALUE)
        tabs.append(tab.reshape(rpb.shape[0], NA_QROWS * GRID_W, NA_KROWS * GRID_W))
    return jnp.stack(tabs)


def _neighborhood_attention(on, table, batch, seq, ctx_len):
    grid_rows = seq // GRID_W
    nblk = grid_rows // NA_QROWS
    bq = NA_QROWS * GRID_W
    d = NA_HEAD_DIM
    ctx0 = batch * seq // ctx_len
    return pl.pallas_call(
        functools.partial(_na_kernel, grid_rows=grid_rows),
        grid=(batch, NA_HEADS, nblk),
        in_specs=[
            pl.BlockSpec((bq, d), lambda b, h, r: (b * nblk + r, h)),
            pl.BlockSpec((seq, d), lambda b, h, r: (b, NA_HEADS + h)),
            pl.BlockSpec((seq, d), lambda b, h, r: (b, 2 * NA_HEADS + h)),
            pl.BlockSpec((ctx_len, d), lambda b, h, r: (ctx0 + b, NA_HEADS + h)),
            pl.BlockSpec((ctx_len, d), lambda b, h, r: (ctx0 + b, 2 * NA_HEADS + h)),
            pl.BlockSpec((None, None, bq, NA_KROWS * GRID_W),
                         lambda b, h, r: (jnp.where(r == 0, 0, jnp.where(r == nblk - 1, 2, 1)), h, 0, 0)),
        ],
        out_specs=pl.BlockSpec((bq, d), lambda b, h, r: (b * nblk + r, h)),
        out_shape=jax.ShapeDtypeStruct((batch * seq, NA_WIDTH), BF16),
        compiler_params=_params("arbitrary", "arbitrary", "arbitrary"),
        name="neighborhood_attention",
    )(on, on, on, on, on, table)


def _dft_consts(n):
    jk = (np.arange(n)[:, None] * np.arange(n)[None, :]) % n
    ang = 2.0 * np.pi * jk / n
    return np.cos(ang), np.sin(ang)


def _hi_lo(m):
    m = jnp.asarray(m, F32)
    hi = m.astype(BF16)
    return hi, (m - hi.astype(F32)).astype(BF16)


def _fn_stage1_kernel(x_ref, fh_ref, fl_ref, tc_ref, ts_ref, o_ref):
    n = GRID_W
    b = _dot3_left(fh_ref[...], fl_ref[...], x_ref[...])
    br, bi = b[:n], b[n:]
    tc, ts = tc_ref[...], ts_ref[...]
    o_ref[0] = br * tc + bi * ts
    o_ref[1] = bi * tc - br * ts


def _fn_stage2_kernel(t_ref, gh_ref, gl_ref, ch_ref, cl_ref, w_ref, p_ref, o_ref):
    n = GRID_W
    outs = []
    for j in range(FN_K2):
        t = jnp.concatenate([t_ref[0, j], t_ref[1, j]], axis=0)
        a = _dot3_left(gh_ref[...], gl_ref[...], t)
        outs.append(a)
    ar = jnp.concatenate([a[:n] for a in outs], axis=0)
    ai = jnp.concatenate([a[n:] for a in outs], axis=0)
    ys = []
    for g in range(FN_GROUPS):
        sl = slice(g * FN_CH, (g + 1) * FN_CH)
        z = jnp.concatenate([ar[:, sl], ai[:, sl]], axis=1)
        spec = _dot3_right(z, ch_ref[...], cl_ref[...])
        ys.append(_dot(spec.astype(BF16), w_ref[g]).astype(BF16))
    y = jnp.concatenate(ys, axis=1)
    y = _dot(p_ref[...], y).astype(BF16)
    o_ref[...] = y.reshape(n, FN_K2, FN_WIDTH)


def _fourier_latent(of, w_fnet, batch, seq):
    n = GRID_W
    assert seq == n * n
    cols = n * FN_WIDTH
    c64, s64 = _dft_consts(n)
    f1h, f1l = _hi_lo(np.concatenate([c64, -s64], axis=0))
    tw = 2.0 * np.pi * (np.arange(n)[:, None] * np.arange(n)[None, :]) / (n * n)
    tc = jnp.asarray(np.repeat(np.cos(tw).reshape(n, n, 1), FN_WIDTH, axis=2).reshape(n, cols), F32)
    ts = jnp.asarray(np.repeat(np.sin(tw).reshape(n, n, 1), FN_WIDTH, axis=2).reshape(n, cols), F32)
    x2 = of[:batch * seq].reshape(batch, n, cols)
    cb = 4096
    t = pl.pallas_call(
        _fn_stage1_kernel,
        grid=(batch, cols // cb),
        in_specs=[
            pl.BlockSpec((None, n, cb), lambda b, j: (b, 0, j)),
            pl.BlockSpec(f1h.shape, lambda b, j: (0, 0)),
            pl.BlockSpec(f1l.shape, lambda b, j: (0, 0)),
            pl.BlockSpec((n, cb), lambda b, j: (0, j)),
            pl.BlockSpec((n, cb), lambda b, j: (0, j)),
        ],
        out_specs=pl.BlockSpec((None, 2, n, cb), lambda b, j: (b, 0, 0, j)),
        out_shape=jax.ShapeDtypeStruct((batch, 2, n, cols), F32),
        compiler_params=_params("arbitrary", "arbitrary"),
        name="fourier_rows",
    )(x2, f1h, f1l, tc, ts)
    t = t.reshape(batch, 2, n, n, FN_WIDTH)

    g2h, g2l = _hi_lo(np.block([[c64, s64], [-s64, c64]]))
    cc, sc = _dft_consts(FN_CH)
    norm = 1.0 / np.sqrt(seq * FN_CH)
    c4h, c4l = _hi_lo(np.concatenate([cc, sc], axis=0) * norm)
    rows = FN_K2 * n
    perm = np.zeros((rows, rows), np.float32)
    k1 = np.arange(n)[:, None]
    j = np.arange(FN_K2)[None, :]
    perm[(k1 * FN_K2 + j).ravel(), (j * n + k1).ravel()] = 1.0
    perm = jnp.asarray(perm, BF16)
    const = lambda a: pl.BlockSpec(a.shape, lambda b, i: (0,) * a.ndim)
    y = pl.pallas_call(
        _fn_stage2_kernel,
        grid=(batch, n // FN_K2),
        in_specs=[
            pl.BlockSpec((None, 2, FN_K2, n, FN_WIDTH), lambda b, i: (b, 0, i, 0, 0)),
            const(g2h), const(g2l), const(c4h), const(c4l), const(w_fnet), const(perm),
        ],
        out_specs=pl.BlockSpec((None, n, FN_K2, FN_WIDTH), lambda b, i: (b, 0, i, 0)),
        out_shape=jax.ShapeDtypeStruct((batch, n, n, FN_WIDTH), BF16),
        compiler_params=_params("arbitrary", "arbitrary"),
        name="fourier_cols_channels",
    )(t, g2h, g2l, c4h, c4l, w_fnet, perm)
    return y.reshape(batch * seq, FN_WIDTH)


def _fn_ctx_kernel(x_ref, fh_ref, fl_ref, ch_ref, cl_ref, w_ref, o_ref, *, n):
    a = _dot3_left(fh_ref[...], fl_ref[...], x_ref[...])
    ar, ai = a[:n], a[n:]
    ys = []
    for g in range(FN_GROUPS):
        sl = slice(g * FN_CH, (g + 1) * FN_CH)
        z = jnp.concatenate([ar[:, sl], ai[:, sl]], axis=1)
        spec = _dot3_right(z, ch_ref[...], cl_ref[...])
        ys.append(_dot(spec.astype(BF16), w_ref[g]).astype(BF16))
    o_ref[...] = jnp.concatenate(ys, axis=1)


def _fourier_context(of, w_fnet, batch, row0, n):
    c, s = _dft_consts(n)
    fh, fl = _hi_lo(np.concatenate([c, -s], axis=0))
    cc, sc = _dft_consts(FN_CH)
    c4h, c4l = _hi_lo(np.concatenate([cc, sc], axis=0) / np.sqrt(n * FN_CH))
    const = lambda a: pl.BlockSpec(a.shape, lambda b: (0,) * a.ndim)
    blk0 = row0 // n
    return pl.pallas_call(
        functools.partial(_fn_ctx_kernel, n=n),
        grid=(batch,),
        in_specs=[pl.BlockSpec((n, FN_WIDTH), lambda b: (blk0 + b, 0)),
                  const(fh), const(fl), const(c4h), const(c4l), const(w_fnet)],
        out_specs=pl.BlockSpec((n, FN_WIDTH), lambda b: (b, 0)),
        out_shape=jax.ShapeDtypeStruct((batch * n, FN_WIDTH), BF16),
        compiler_params=_params("arbitrary"),
        name="fourier_context",
    )(of, fh, fl, c4h, c4l, w_fnet)


def _proj_residual_kernel(*refs, n_in, lat_blocks, has_ctx):
    lat = refs[:n_in]
    ctx = refs[n_in:2 * n_in] if has_ctx else ()
    rest = refs[(2 if has_ctx else 1) * n_in:]
    w_refs, (x_ref, gt_ref, o_ref) = rest[:n_in], rest[n_in:]

    def run(a_refs):
        acc = None
        for a_ref, w_ref in zip(a_refs, w_refs):
            t = _dot(a_ref[...], w_ref[...])
            acc = t if acc is None else acc + t
        o_ref[...] = x_ref[...] + gt_ref[...] * acc

    if has_ctx:
        i = pl.program_id(0)
        pl.when(i < lat_blocks)(lambda: run(lat))
        pl.when(i >= lat_blocks)(lambda: run(ctx))
    else:
        run(lat)


def _proj_residual(xa, lat_ins, ctx_ins, ws, mod_l, gate_k, blocks_per_batch, n_blocks):
    bm, bn = ROW_BLOCK, 512
    n_in = len(lat_ins)
    has_ctx = ctx_ins is not None
    lat_blocks = lat_ins[0].shape[0] // bm
    in_specs = [pl.BlockSpec((bm, a.shape[1]), lambda i, j: (jnp.minimum(i, lat_blocks - 1), 0)) for a in lat_ins]
    if has_ctx:
        in_specs += [pl.BlockSpec((bm, a.shape[1]), lambda i, j: (jnp.maximum(i - lat_blocks, 0), 0))
                     for a in ctx_ins]
    in_specs += [pl.BlockSpec((w.shape[0], bn), lambda i, j: (0, j)) for w in ws]
    in_specs += [
        pl.BlockSpec((bm, bn), lambda i, j: (i, j)),
        pl.BlockSpec((None, 1, bn), lambda i, j: (i // blocks_per_batch, 0, gate_k * (D_MODEL // bn) + j)),
    ]
    args = list(lat_ins) + (list(ctx_ins) if has_ctx else []) + list(ws) + [xa, mod_l]
    return pl.pallas_call(
        functools.partial(_proj_residual_kernel, n_in=n_in, lat_blocks=lat_blocks, has_ctx=has_ctx),
        grid=(n_blocks, D_MODEL // bn),
        in_specs=in_specs,
        out_specs=pl.BlockSpec((bm, bn), lambda i, j: (i, j)),
        out_shape=jax.ShapeDtypeStruct(xa.shape, F32),
        input_output_aliases={len(args) - 2: 0},
        compiler_params=_params("arbitrary", "arbitrary"),
        name="projection_residual",
    )(*args)


def _ffn_up_kernel(xm_ref, xp_ref, xn_ref, sh_ref, sc_ref, g_ref, wg_ref, wv_ref, cwg_ref, cwv_ref,
                   cbg_ref, cbv_ref, o_ref, h_ref, *, lat_blocks, seq, ctx_len):
    bm = ROW_BLOCK
    i = pl.program_id(0)

    @pl.when(pl.program_id(1) == 0)
    def _():
        g, sc, sh = g_ref[...], sc_ref[...], sh_ref[...]
        h_ref[0:HALO, :] = _normmod(xp_ref[...], g, sc, sh).astype(BF16)
        h_ref[HALO:HALO + bm, :] = _normmod(xm_ref[...], g, sc, sh).astype(BF16)
        h_ref[HALO + bm:, :] = _normmod(xn_ref[...], g, sc, sh).astype(BF16)

    h = h_ref[...]
    rows = lax.broadcasted_iota(jnp.int32, (bm + 2 * HALO, 1), 0) + (i * bm - HALO)
    period = jnp.where(i < lat_blocks, seq, ctx_len)
    first = (rows & (period - 1)) == 0
    last = ((rows + 1) & (period - 1)) == 0

    def conv(w_ref, cw_ref, cb_ref):
        u = _dot(h, w_ref[...])
        prev = jnp.where(first, 0.0, pltpu.roll(u, 1, axis=0))
        nxt = jnp.where(last, 0.0, pltpu.roll(u, bm + 2 * HALO - 1, axis=0))
        return prev * cw_ref[0:1, :] + u * cw_ref[1:2, :] + nxt * cw_ref[2:3, :] + cb_ref[...]

    gate = conv(wg_ref, cwg_ref, cbg_ref)
    val = conv(wv_ref, cwv_ref, cbv_ref)
    a = gate * _sigmoid(gate) * val
    o_ref[...] = a[HALO:HALO + bm].astype(BF16)


def _ffn_up(xa, mod_l, g, w_up, conv_w, conv_b, blocks_per_batch, n_blocks, lat_blocks, seq, ctx_len):
    rows = xa.shape[0]
    bm, bn = ROW_BLOCK, 512
    nj = D_FF // bn
    hb = bm // HALO
    last_halo = rows // HALO - 1
    assert seq & (seq - 1) == 0 and ctx_len & (ctx_len - 1) == 0
    return pl.pallas_call(
        functools.partial(_ffn_up_kernel, lat_blocks=lat_blocks, seq=seq, ctx_len=ctx_len),
        grid=(n_blocks, nj),
        in_specs=[
            pl.BlockSpec((bm, D_MODEL), lambda i, j: (i, 0)),
            pl.BlockSpec((HALO, D_MODEL), lambda i, j: (jnp.maximum(i * hb - 1, 0), 0)),
            pl.BlockSpec((HALO, D_MODEL), lambda i, j: (jnp.minimum((i + 1) * hb, last_halo), 0)),
            _mod_spec(3, blocks_per_batch),
            _mod_spec(4, blocks_per_batch),
            pl.BlockSpec((1, D_MODEL), lambda i, j: (0, 0)),
            pl.BlockSpec((D_MODEL, bn), lambda i, j: (0, j)),
            pl.BlockSpec((D_MODEL, bn), lambda i, j: (0, nj + j)),
            pl.BlockSpec((CONV_W, bn), lambda i, j: (0, j)),
            pl.BlockSpec((CONV_W, bn), lambda i, j: (0, nj + j)),
            pl.BlockSpec((1, bn), lambda i, j: (0, j)),
            pl.BlockSpec((1, bn), lambda i, j: (0, nj + j)),
        ],
        out_specs=pl.BlockSpec((bm, bn), lambda i, j: (i, j)),
        out_shape=jax.ShapeDtypeStruct((n_blocks * bm, D_FF), BF16),
        scratch_shapes=[pltpu.VMEM((bm + 2 * HALO, D_MODEL), BF16)],
        compiler_params=_params("arbitrary", "arbitrary"),
        name="ffn_up_conv_gate",
    )(xa, xa, xa, mod_l, mod_l, g, w_up, w_up, conv_w, conv_w, conv_b, conv_b)


def _final_norm_kernel(x_ref, g_ref, o_ref):
    o_ref[...] = _rms(x_ref[...], g_ref[...])


def _final_norm(xa, g, rows):
    bm = ROW_BLOCK
    return pl.pallas_call(
        _final_norm_kernel,
        grid=(rows // bm,),
        in_specs=[pl.BlockSpec((bm, D_MODEL), lambda i: (i, 0)), pl.BlockSpec((1, D_MODEL), lambda i: (0, 0))],
        out_specs=pl.BlockSpec((bm, D_MODEL), lambda i: (i, 0)),
        out_shape=jax.ShapeDtypeStruct((rows, D_MODEL), F32),
        compiler_params=_params("arbitrary"),
        name="final_norm",
    )(xa, g)


def _rope_table(seq, pad_rows):
    n = MLA_ROPE // 4
    freqs = ROPE_BASE ** (-jnp.arange(n, dtype=F32) / n)
    pos = jnp.arange(seq)
    ang_r = (pos // GRID_W).astype(F32)[:, None] * freqs
    ang_c = (pos % GRID_W).astype(F32)[:, None] * freqs
    cos = jnp.concatenate([jnp.cos(ang_r)] * 2 + [jnp.cos(ang_c)] * 2, axis=1)
    sin = jnp.concatenate([-jnp.sin(ang_r), jnp.sin(ang_r), -jnp.sin(ang_c), jnp.sin(ang_c)], axis=1)
    lat = jnp.concatenate([cos, sin], axis=1)
    ident = jnp.concatenate([jnp.ones((pad_rows, MLA_ROPE), F32), jnp.zeros((pad_rows, MLA_ROPE), F32)], axis=1)
    return jnp.concatenate([lat, ident], axis=0)


def _partner_perm():
    q = MLA_ROPE // 4
    return np.concatenate([np.arange(q, 2 * q), np.arange(0, q), np.arange(3 * q, 4 * q), np.arange(2 * q, 3 * q)])


def _layout_w_in(w):
    cq, ckv, kr, qn, kn, vn, f = jnp.split(w, np.cumsum(
        (MLA_Q_RANK, MLA_KV_RANK, MLA_ROPE, NA_WIDTH, NA_WIDTH, NA_WIDTH))[:].tolist(), axis=1)
    return jnp.concatenate([cq, ckv, qn, kn, vn, f, kr, kr[:, _partner_perm()]], axis=1).astype(BF16)


def _layout_w_uq(w):
    w = w.reshape(MLA_Q_RANK, MLA_HEADS, MLA_NOPE + MLA_ROPE)
    rope = w[:, :, MLA_NOPE:]
    return jnp.concatenate([w, rope[:, :, _partner_perm()]], axis=2).reshape(MLA_Q_RANK, -1).astype(BF16)


def _layout_w_ukv(w):
    w = w.reshape(MLA_KV_RANK, MLA_HEADS, MLA_NOPE + MLA_V)
    return jnp.concatenate([w[:, :, :MLA_NOPE].reshape(MLA_KV_RANK, -1),
                            w[:, :, MLA_NOPE:].reshape(MLA_KV_RANK, -1)], axis=1).astype(BF16)


def kernel(x, c, ctx, c_ctx, w_mod, b_mod, g_attn, g_ffn, w_in, g_q, w_uq, g_kv, w_ukv, na_rpb, w_fnet, w_out,
           w_up, conv_w, conv_b, w_down, g_final):
    batch, seq, d = x.shape
    ctx_len = ctx.shape[1]
    depth = w_mod.shape[0]
    n_lat, n_ctx = batch * seq, batch * ctx_len
    bm = ROW_BLOCK
    lat_blocks = n_lat // bm
    all_blocks = (n_lat + n_ctx) // bm
    blocks_per_batch = seq // bm
    assert n_ctx % bm == 0 and n_lat // bm // blocks_per_batch == batch and batch < 8

    xa = jnp.concatenate([x.reshape(n_lat, d), ctx.reshape(n_ctx, d)], axis=0)
    cin = jnp.zeros((8, d), F32).at[:batch].set(c).at[batch].set(c_ctx)
    mod = _modulation(cin, w_mod, b_mod)
    cs_tab = _rope_table(seq, bm)
    row = lambda v: v.reshape(1, -1)

    for l in range(depth):
        ctx_out = l < depth - 1
        mod_l = mod[l].reshape(8, 1, 6 * d)
        oc, on, of, okr = _inproj(xa, mod_l, row(g_attn[l]), _layout_w_in(w_in[l]), blocks_per_batch)
        qm, km, vm = _mla_up(oc, okr, cs_tab, row(g_q[l]), row(g_kv[l]), _layout_w_uq(w_uq[l]),
                             _layout_w_ukv(w_ukv[l]), lat_blocks, blocks_per_batch)

        bq = 512
        o_mla = _attention(
            qm, km, vm, batch=batch, heads=MLA_HEADS, dq=MLA_QK_PAD, dv=MLA_V, bq=bq, nq=seq // bq,
            q_block0=lambda b: b * (seq // bq),
            sources=((lambda b: b, seq), (lambda b: n_lat // ctx_len + b, ctx_len)),
            k_col0=0, v_col0=0, scale=MLA_SCALE, out_rows=n_lat, name="mla_attention")
        o_na = _neighborhood_attention(on, _na_table(na_rpb[l], seq // GRID_W), batch, seq, ctx_len)
        o_fn = _fourier_latent(of, w_fnet[l].astype(BF16), batch, seq)
        w_o = w_out[l].astype(BF16)
        w_os = (w_o[:MLA_WIDTH], w_o[MLA_WIDTH:MLA_WIDTH + NA_WIDTH], w_o[MLA_WIDTH + NA_WIDTH:])

        ctx_ins = None
        if ctx_out:
            ctx_src = ((lambda b: n_lat // ctx_len + b, ctx_len),)
            o_mla_c = _attention(
                qm, km, vm, batch=batch, heads=MLA_HEADS, dq=MLA_QK_PAD, dv=MLA_V, bq=ctx_len, nq=1,
                q_block0=lambda b: n_lat // ctx_len + b, sources=ctx_src, k_col0=0, v_col0=0,
                scale=MLA_SCALE, out_rows=n_ctx, name="mla_attention_context")
            o_na_c = _attention(
                on, on, on, batch=batch, heads=NA_HEADS, dq=NA_HEAD_DIM, dv=NA_HEAD_DIM, bq=ctx_len, nq=1,
                q_block0=lambda b: n_lat // ctx_len + b, sources=ctx_src, k_col0=NA_HEADS, v_col0=2 * NA_HEADS,
                scale=NA_SCALE, out_rows=n_ctx, name="na_attention_context")
            o_fn_c = _fourier_context(of, w_fnet[l].astype(BF16), batch, n_lat, ctx_len)
            ctx_ins = (o_mla_c, o_na_c, o_fn_c)

        n_blocks = all_blocks if ctx_out else lat_blocks
        xa = _proj_residual(xa, (o_mla, o_na, o_fn), ctx_ins, w_os, mod_l, 2, blocks_per_batch, n_blocks)
        a = _ffn_up(xa, mod_l, row(g_ffn[l]), w_up[l].astype(BF16), conv_w[l], row(conv_b[l]),
                    blocks_per_batch, n_blocks, lat_blocks, seq, ctx_len)
        xa = _proj_residual(xa, (a,), None, (w_down[l].astype(BF16),), mod_l, 5, blocks_per_batch, n_blocks)

    out = _final_norm(xa, row(g_final), n_lat)
    return out.reshape(batch, seq, d)
```

```python
import functools

import numpy as np
import jax
import jax.numpy as jnp
from jax import lax
from jax.experimental import pallas as pl
from jax.experimental.pallas import tpu as pltpu

F32 = jnp.float32
BF16 = jnp.bfloat16

D_MODEL = 2048
GRID_W = 64
EPS = 1e-6

MLA_HEADS = 8
MLA_NOPE = 128
MLA_ROPE = 64
MLA_V = 128
MLA_Q_RANK = 512
MLA_KV_RANK = 512
MLA_QK_PAD = 256
MLA_SCALE = (MLA_NOPE + MLA_ROPE) ** -0.5
ROPE_BASE = 10000.0

NA_HEADS = 4
NA_HEAD_DIM = 128
NA_KH = 8
NA_KW = 16
NA_SCALE = NA_HEAD_DIM ** -0.5
NA_QROWS = 8
NA_KROWS = NA_QROWS + NA_KH - 1

FN_GROUPS = 4
FN_CH = 128
FN_WIDTH = FN_GROUPS * FN_CH
FN_K2 = 16

MLA_WIDTH = MLA_HEADS * MLA_V
NA_WIDTH = NA_HEADS * NA_HEAD_DIM
D_FF = 5632
CONV_W = 3

ROW_BLOCK = 512
HALO = 16
MASK_VALUE = -1e30
LOG2E = 1.4426950408889634
VMEM_LIMIT = 56 * 1024 * 1024


def _params(*sem):
    return pltpu.CompilerParams(dimension_semantics=sem, vmem_limit_bytes=VMEM_LIMIT)


def _dot(a, b):
    return jnp.dot(a, b, preferred_element_type=F32)


def _dot_nt(a, b):
    return lax.dot_general(a, b, (((1,), (1,)), ((), ())), preferred_element_type=F32)


def _split(x):
    hi = x.astype(BF16)
    lo = (x - hi.astype(F32)).astype(BF16)
    return hi, lo


def _dot3_left(m_hi, m_lo, x):
    x_hi, x_lo = _split(x)
    return _dot(m_hi, x_hi) + (_dot(m_hi, x_lo) + _dot(m_lo, x_hi))


def _dot3_right(x, m_hi, m_lo):
    x_hi, x_lo = _split(x)
    return _dot(x_hi, m_hi) + (_dot(x_lo, m_hi) + _dot(x_hi, m_lo))


def _sigmoid(x):
    return 1.0 / (1.0 + jnp.exp(-x))


def _rms(x, g):
    y = x * lax.rsqrt(jnp.mean(x * x, axis=-1, keepdims=True) + EPS)
    return y * g


def _normmod(x, g, sc, sh):
    return _rms(x, g) * (1.0 + sc) + sh


def _mod_kernel(c_ref, w_ref, b_ref, o_ref):
    c = c_ref[...]
    s = (c * _sigmoid(c)).astype(BF16)
    o_ref[...] = _dot(s, w_ref[...].astype(BF16)) + b_ref[...]


def _modulation(cin, w_mod, b_mod):
    depth, d, n = w_mod.shape
    bn = 1024
    return pl.pallas_call(
        _mod_kernel,
        grid=(depth, n // bn),
        in_specs=[
            pl.BlockSpec((8, d), lambda l, j: (0, 0)),
            pl.BlockSpec((None, d, bn), lambda l, j: (l, 0, j)),
            pl.BlockSpec((None, 1, bn), lambda l, j: (l, 0, j)),
        ],
        out_specs=pl.BlockSpec((None, 8, bn), lambda l, j: (l, 0, j)),
        out_shape=jax.ShapeDtypeStruct((depth, 8, n), F32),
        compiler_params=_params("arbitrary", "arbitrary"),
        name="modulation",
    )(cin, w_mod, b_mod.reshape(depth, 1, n))


def _inproj_kernel(x_ref, sh_ref, sc_ref, g_ref, w_ref, oc_ref, on_ref, of_ref, okr_ref):
    h = _normmod(x_ref[...], g_ref[...], sc_ref[...], sh_ref[...]).astype(BF16)
    oc_ref[...] = _dot(h, w_ref[:, 0:1024])
    on_ref[:, 0:NA_WIDTH] = (_dot(h, w_ref[:, 1024:1024 + NA_WIDTH]) * (NA_SCALE * LOG2E)).astype(BF16)
    on_ref[:, NA_WIDTH:] = _dot(h, w_ref[:, 1024 + NA_WIDTH:2560]).astype(BF16)
    of_ref[...] = _dot(h, w_ref[:, 2560:3072])
    okr_ref[...] = _dot(h, w_ref[:, 3072:3200])


def _mod_spec(k, blocks_per_batch):
    return pl.BlockSpec((None, 1, D_MODEL), lambda i, *_: (i // blocks_per_batch, 0, k))


def _inproj(xa, mod_l, g, w_p, blocks_per_batch):
    rows = xa.shape[0]
    bm = ROW_BLOCK
    row = lambda w: pl.BlockSpec((bm, w), lambda i: (i, 0))
    return pl.pallas_call(
        _inproj_kernel,
        grid=(rows // bm,),
        in_specs=[
            row(D_MODEL),
            _mod_spec(0, blocks_per_batch),
            _mod_spec(1, blocks_per_batch),
            pl.BlockSpec((1, D_MODEL), lambda i: (0, 0)),
            pl.BlockSpec(w_p.shape, lambda i: (0, 0), pipeline_mode=pl.Buffered(1)),
        ],
        out_specs=[row(1024), row(1536), row(512), row(128)],
        out_shape=[
            jax.ShapeDtypeStruct((rows, 1024), F32),
            jax.ShapeDtypeStruct((rows, 1536), BF16),
            jax.ShapeDtypeStruct((rows, 512), F32),
            jax.ShapeDtypeStruct((rows, 128), F32),
        ],
        compiler_params=_params("arbitrary"),
        name="in_projection",
    )(xa, mod_l, mod_l, g, w_p)


def _rope(t, cs):
    t = t * cs
    return t + pltpu.roll(t, 64, axis=1)


def _mla_up_kernel(c_ref, kr_ref, cs_ref, gq_ref, gkv_ref, wq_ref, wkv_ref, q_ref, k_ref, v_ref):
    cs = cs_ref[...]
    cq = _rms(c_ref[:, 0:MLA_Q_RANK], gq_ref[...]).astype(BF16)
    ckv = _rms(c_ref[:, MLA_Q_RANK:MLA_Q_RANK + MLA_KV_RANK], gkv_ref[...]).astype(BF16)
    q = _dot(cq, wq_ref[...]) * (MLA_SCALE * LOG2E)
    kv = _dot(ckv, wkv_ref[...])
    lane = lax.broadcasted_iota(jnp.int32, cs.shape, 1)
    k_rope = jnp.where(lane < MLA_ROPE, _rope(kr_ref[...], cs), 0.0).astype(BF16)
    for h in range(MLA_HEADS):
        o = h * MLA_QK_PAD
        q_ref[:, o:o + MLA_NOPE] = q[:, o:o + MLA_NOPE].astype(BF16)
        q_ref[:, o + MLA_NOPE:o + MLA_QK_PAD] = _rope(q[:, o + MLA_NOPE:o + MLA_QK_PAD], cs).astype(BF16)
        k_ref[:, o:o + MLA_NOPE] = kv[:, h * MLA_NOPE:(h + 1) * MLA_NOPE].astype(BF16)
        k_ref[:, o + MLA_NOPE:o + MLA_QK_PAD] = k_rope
    v_ref[...] = kv[:, MLA_HEADS * MLA_NOPE:].astype(BF16)


def _mla_up(oc, okr, cs_tab, g_q, g_kv, wq_p, wkv_p, lat_blocks, pos_blocks):
    rows = oc.shape[0]
    bm = ROW_BLOCK
    row = lambda w: pl.BlockSpec((bm, w), lambda i: (i, 0))
    const = lambda a: pl.BlockSpec(a.shape, lambda i: (0, 0))
    cs_spec = pl.BlockSpec((bm, 128), lambda i: (jnp.where(i < lat_blocks, i % pos_blocks, pos_blocks), 0))
    qk_w = MLA_HEADS * MLA_QK_PAD
    return pl.pallas_call(
        _mla_up_kernel,
        grid=(rows // bm,),
        in_specs=[row(1024), row(128), cs_spec, const(g_q), const(g_kv), const(wq_p), const(wkv_p)],
        out_specs=[row(qk_w), row(qk_w), row(MLA_WIDTH)],
        out_shape=[
            jax.ShapeDtypeStruct((rows, qk_w), BF16),
            jax.ShapeDtypeStruct((rows, qk_w), BF16),
            jax.ShapeDtypeStruct((rows, MLA_WIDTH), BF16),
        ],
        compiler_params=_params("arbitrary"),
        name="mla_up_projection",
    )(oc, okr, cs_tab, g_q, g_kv, wq_p, wkv_p)


def _softmax_first(s, v):
    m = jnp.max(s, axis=-1, keepdims=True)
    p = jnp.exp2(s - m)
    return m, jnp.sum(p, axis=-1, keepdims=True), _dot(p.astype(BF16), v)


def _softmax_next(s, s_max, v, m, l, acc):
    m_new = jnp.maximum(m, s_max)
    p = jnp.exp2(s - m_new)
    alpha = jnp.exp2(m - m_new)
    l = alpha * l + jnp.sum(p, axis=-1, keepdims=True)
    acc = alpha * acc + _dot(p.astype(BF16), v)
    return m_new, l, acc


def _attention_kernel(q_ref, k_ref, v_ref, kc_ref, vc_ref, o_ref, sa_ref, sb_ref, *, ck, n_main):
    q = q_ref[...]
    carry = _softmax_first(_dot_nt(q, kc_ref[...]), vc_ref[...])
    bufs = (sa_ref, sb_ref)

    def scores(c):
        s = _dot_nt(q, k_ref[c * ck:(c + 1) * ck, :])
        bufs[c % 2][...] = s
        return jnp.max(s, axis=-1, keepdims=True)

    s_max = scores(0)
    for c in range(n_main):
        nxt_max = scores(c + 1) if c + 1 < n_main else None
        carry = _softmax_next(bufs[c % 2][...], s_max, v_ref[c * ck:(c + 1) * ck, :], *carry)
        s_max = nxt_max
    _, l, acc = carry
    o_ref[...] = (acc / l).astype(o_ref.dtype)


def _attention(q_arr, k_arr, v_arr, *, batch, heads, dq, dv, bq, seq, ctx_len, k_col0, v_col0, name, ck=512):
    nq = seq // bq
    ctx0 = batch * seq // ctx_len
    return pl.pallas_call(
        functools.partial(_attention_kernel, ck=ck, n_main=seq // ck),
        grid=(batch, heads, nq),
        in_specs=[
            pl.BlockSpec((bq, dq), lambda b, h, i: (b * nq + i, h)),
            pl.BlockSpec((seq, dq), lambda b, h, i: (b, k_col0 + h)),
            pl.BlockSpec((seq, dv), lambda b, h, i: (b, v_col0 + h)),
            pl.BlockSpec((ctx_len, dq), lambda b, h, i: (ctx0 + b, k_col0 + h)),
            pl.BlockSpec((ctx_len, dv), lambda b, h, i: (ctx0 + b, v_col0 + h)),
        ],
        out_specs=pl.BlockSpec((bq, dv), lambda b, h, i: (b * nq + i, h)),
        out_shape=jax.ShapeDtypeStruct((batch * seq, heads * dv), BF16),
        scratch_shapes=[pltpu.VMEM((bq, ck), F32), pltpu.VMEM((bq, ck), F32)],
        compiler_params=_params("arbitrary", "arbitrary", "arbitrary"),
        name=name,
    )(q_arr, k_arr, v_arr, k_arr, v_arr)


def _context_attention_kernel(q_ref, k_ref, v_ref, o_ref):
    _, l, acc = _softmax_first(_dot_nt(q_ref[...], k_ref[...]), v_ref[...])
    o_ref[...] = (acc / l).astype(o_ref.dtype)


def _context_attention(q_arr, k_arr, v_arr, *, batch, heads, dq, dv, row0, ctx_len, k_col0, v_col0, name):
    blk0 = row0 // ctx_len
    return pl.pallas_call(
        _context_attention_kernel,
        grid=(batch, heads),
        in_specs=[
            pl.BlockSpec((ctx_len, dq), lambda b, h: (blk0 + b, h)),
            pl.BlockSpec((ctx_len, dq), lambda b, h: (blk0 + b, k_col0 + h)),
            pl.BlockSpec((ctx_len, dv), lambda b, h: (blk0 + b, v_col0 + h)),
        ],
        out_specs=pl.BlockSpec((ctx_len, dv), lambda b, h: (b, h)),
        out_shape=jax.ShapeDtypeStruct((batch * ctx_len, heads * dv), BF16),
        compiler_params=_params("arbitrary", "arbitrary"),
        name=name,
    )(q_arr, k_arr, v_arr)


def _na_kernel(q_ref, k_ref, v_ref, kc_ref, vc_ref, t_ref, o_ref, *, grid_rows):
    r = pl.program_id(2)
    ks = jnp.clip(r * NA_QROWS - NA_KH // 2, 0, grid_rows - NA_KROWS)
    start = pl.multiple_of(ks * GRID_W, GRID_W)
    nwin = NA_KROWS * GRID_W
    q = q_ref[...]
    s_win = _dot_nt(q, k_ref[pl.ds(start, nwin), :]) + t_ref[...]
    s_ctx = _dot_nt(q, kc_ref[...])
    m = jnp.maximum(jnp.max(s_win, axis=-1, keepdims=True), jnp.max(s_ctx, axis=-1, keepdims=True))
    p_win = jnp.exp2(s_win - m)
    p_ctx = jnp.exp2(s_ctx - m)
    l = jnp.sum(p_win, axis=-1, keepdims=True) + jnp.sum(p_ctx, axis=-1, keepdims=True)
    acc = _dot(p_win.astype(BF16), v_ref[pl.ds(start, nwin), :]) + _dot(p_ctx.astype(BF16), vc_ref[...])
    o_ref[...] = (acc / l).astype(o_ref.dtype)


def _na_table(rpb, grid_rows):
    tabs = []
    for r0 in (0, NA_QROWS, grid_rows - NA_QROWS):
        ks = min(max(r0 - NA_KH // 2, 0), grid_rows - NA_KROWS)
        r = r0 + np.arange(NA_QROWS)[:, None, None, None]
        w = np.arange(GRID_W)[None, :, None, None]
        kr = ks + np.arange(NA_KROWS)[None, None, :, None]
        j = np.arange(GRID_W)[None, None, None, :]
        rs = np.clip(r - NA_KH // 2, 0, grid_rows - NA_KH)
        cs = np.clip(w - NA_KW // 2, 0, GRID_W - NA_KW)
        valid = (kr >= rs) & (kr < rs + NA_KH) & (j >= cs) & (j < cs + NA_KW)
        sel_r = ((kr - r + (NA_KH - 1))[:, 0, :, 0, None] == np.arange(2 * NA_KH - 1)).astype(np.float32)
        sel_c = ((j - w + (NA_KW - 1))[0, :, 0, :, None] == np.arange(2 * NA_KW - 1)).astype(np.float32)
        bias = jnp.einsum("rap,hpq,wjq->hrwaj", sel_r, rpb.astype(F32), sel_c,
                          precision=lax.Precision.HIGHEST)
        tab = jnp.where(np.broadcast_to(valid, bias.shape[1:])[None], bias * LOG2E, MASK_VALUE)
        tabs.append(tab.reshape(rpb.shape[0], NA_QROWS * GRID_W, NA_KROWS * GRID_W))
    return jnp.stack(tabs)


def _neighborhood_attention(on, table, batch, seq, ctx_len):
    grid_rows = seq // GRID_W
    nblk = grid_rows // NA_QROWS
    bq = NA_QROWS * GRID_W
    d = NA_HEAD_DIM
    ctx0 = batch * seq // ctx_len
    return pl.pallas_call(
        functools.partial(_na_kernel, grid_rows=grid_rows),
        grid=(batch, NA_HEADS, nblk),
        in_specs=[
            pl.BlockSpec((bq, d), lambda b, h, r: (b * nblk + r, h)),
            pl.BlockSpec((seq, d), lambda b, h, r: (b, NA_HEADS + h)),
            pl.BlockSpec((seq, d), lambda b, h, r: (b, 2 * NA_HEADS + h)),
            pl.BlockSpec((ctx_len, d), lambda b, h, r: (ctx0 + b, NA_HEADS + h)),
            pl.BlockSpec((ctx_len, d), lambda b, h, r: (ctx0 + b, 2 * NA_HEADS + h)),
            pl.BlockSpec((None, None, bq, NA_KROWS * GRID_W),
                         lambda b, h, r: (jnp.where(r == 0, 0, jnp.where(r == nblk - 1, 2, 1)), h, 0, 0)),
        ],
        out_specs=pl.BlockSpec((bq, d), lambda b, h, r: (b * nblk + r, h)),
        out_shape=jax.ShapeDtypeStruct((batch * seq, NA_WIDTH), BF16),
        compiler_params=_params("arbitrary", "arbitrary", "arbitrary"),
        name="neighborhood_attention",
    )(on, on, on, on, on, table)


def _dft_consts(n):
    jk = (np.arange(n)[:, None] * np.arange(n)[None, :]) % n
    ang = 2.0 * np.pi * jk / n
    return np.cos(ang), np.sin(ang)


def _hi_lo(m):
    m = jnp.asarray(m, F32)
    hi = m.astype(BF16)
    return hi, (m - hi.astype(F32)).astype(BF16)


def _fn_stage1_kernel(x_ref, fh_ref, fl_ref, tc_ref, ts_ref, o_ref):
    n = GRID_W
    b = _dot3_left(fh_ref[...], fl_ref[...], x_ref[...])
    br, bi = b[:n], b[n:]
    tc, ts = tc_ref[...], ts_ref[...]
    o_ref[0] = br * tc + bi * ts
    o_ref[1] = bi * tc - br * ts


def _fn_stage2_kernel(t_ref, gh_ref, gl_ref, ch_ref, cl_ref, w_ref, p_ref, o_ref):
    n = GRID_W
    outs = []
    for j in range(FN_K2):
        t = jnp.concatenate([t_ref[0, j], t_ref[1, j]], axis=0)
        a = _dot3_left(gh_ref[...], gl_ref[...], t)
        outs.append(a)
    ar = jnp.concatenate([a[:n] for a in outs], axis=0)
    ai = jnp.concatenate([a[n:] for a in outs], axis=0)
    ys = []
    for g in range(FN_GROUPS):
        sl = slice(g * FN_CH, (g + 1) * FN_CH)
        z = jnp.concatenate([ar[:, sl], ai[:, sl]], axis=1)
        spec = _dot3_right(z, ch_ref[...], cl_ref[...])
        ys.append(_dot(spec.astype(BF16), w_ref[g]).astype(BF16))
    y = jnp.concatenate(ys, axis=1)
    y = _dot(p_ref[...], y).astype(BF16)
    o_ref[...] = y.reshape(n, FN_K2, FN_WIDTH)


def _fourier_latent(of, w_fnet, batch, seq):
    n = GRID_W
    assert seq == n * n
    cols = n * FN_WIDTH
    c64, s64 = _dft_consts(n)
    f1h, f1l = _hi_lo(np.concatenate([c64, -s64], axis=0))
    tw = 2.0 * np.pi * (np.arange(n)[:, None] * np.arange(n)[None, :]) / (n * n)
    tc = jnp.asarray(np.repeat(np.cos(tw).reshape(n, n, 1), FN_WIDTH, axis=2).reshape(n, cols), F32)
    ts = jnp.asarray(np.repeat(np.sin(tw).reshape(n, n, 1), FN_WIDTH, axis=2).reshape(n, cols), F32)
    x2 = of[:batch * seq].reshape(batch, n, cols)
    cb = 4096
    t = pl.pallas_call(
        _fn_stage1_kernel,
        grid=(batch, cols // cb),
        in_specs=[
            pl.BlockSpec((None, n, cb), lambda b, j: (b, 0, j)),
            pl.BlockSpec(f1h.shape, lambda b, j: (0, 0)),
            pl.BlockSpec(f1l.shape, lambda b, j: (0, 0)),
            pl.BlockSpec((n, cb), lambda b, j: (0, j)),
            pl.BlockSpec((n, cb), lambda b, j: (0, j)),
        ],
        out_specs=pl.BlockSpec((None, 2, n, cb), lambda b, j: (b, 0, 0, j)),
        out_shape=jax.ShapeDtypeStruct((batch, 2, n, cols), F32),
        compiler_params=_params("arbitrary", "arbitrary"),
        name="fourier_rows",
    )(x2, f1h, f1l, tc, ts)
    t = t.reshape(batch, 2, n, n, FN_WIDTH)

    g2h, g2l = _hi_lo(np.block([[c64, s64], [-s64, c64]]))
    cc, sc = _dft_consts(FN_CH)
    norm = 1.0 / np.sqrt(seq * FN_CH)
    c4h, c4l = _hi_lo(np.concatenate([cc, sc], axis=0) * norm)
    rows = FN_K2 * n
    perm = np.zeros((rows, rows), np.float32)
    k1 = np.arange(n)[:, None]
    j = np.arange(FN_K2)[None, :]
    perm[(k1 * FN_K2 + j).ravel(), (j * n + k1).ravel()] = 1.0
    perm = jnp.asarray(perm, BF16)
    const = lambda a: pl.BlockSpec(a.shape, lambda b, i: (0,) * a.ndim)
    y = pl.pallas_call(
        _fn_stage2_kernel,
        grid=(batch, n // FN_K2),
        in_specs=[
            pl.BlockSpec((None, 2, FN_K2, n, FN_WIDTH), lambda b, i: (b, 0, i, 0, 0)),
            const(g2h), const(g2l), const(c4h), const(c4l), const(w_fnet), const(perm),
        ],
        out_specs=pl.BlockSpec((None, n, FN_K2, FN_WIDTH), lambda b, i: (b, 0, i, 0)),
        out_shape=jax.ShapeDtypeStruct((batch, n, n, FN_WIDTH), BF16),
        compiler_params=_params("arbitrary", "arbitrary"),
        name="fourier_cols_channels",
    )(t, g2h, g2l, c4h, c4l, w_fnet, perm)
    return y.reshape(batch * seq, FN_WIDTH)


def _fn_ctx_kernel(x_ref, fh_ref, fl_ref, ch_ref, cl_ref, w_ref, o_ref, *, n):
    a = _dot3_left(fh_ref[...], fl_ref[...], x_ref[...])
    ar, ai = a[:n], a[n:]
    ys = []
    for g in range(FN_GROUPS):
        sl = slice(g * FN_CH, (g + 1) * FN_CH)
        z = jnp.concatenate([ar[:, sl], ai[:, sl]], axis=1)
        spec = _dot3_right(z, ch_ref[...], cl_ref[...])
        ys.append(_dot(spec.astype(BF16), w_ref[g]).astype(BF16))
    o_ref[...] = jnp.concatenate(ys, axis=1)


def _fourier_context(of, w_fnet, batch, row0, n):
    c, s = _dft_consts(n)
    fh, fl = _hi_lo(np.concatenate([c, -s], axis=0))
    cc, sc = _dft_consts(FN_CH)
    c4h, c4l = _hi_lo(np.concatenate([cc, sc], axis=0) / np.sqrt(n * FN_CH))
    const = lambda a: pl.BlockSpec(a.shape, lambda b: (0,) * a.ndim)
    blk0 = row0 // n
    return pl.pallas_call(
        functools.partial(_fn_ctx_kernel, n=n),
        grid=(batch,),
        in_specs=[pl.BlockSpec((n, FN_WIDTH), lambda b: (blk0 + b, 0)),
                  const(fh), const(fl), const(c4h), const(c4l), const(w_fnet)],
        out_specs=pl.BlockSpec((n, FN_WIDTH), lambda b: (b, 0)),
        out_shape=jax.ShapeDtypeStruct((batch * n, FN_WIDTH), BF16),
        compiler_params=_params("arbitrary"),
        name="fourier_context",
    )(of, fh, fl, c4h, c4l, w_fnet)


def _proj_residual_kernel(*refs, n_in, lat_blocks, has_ctx):
    lat = refs[:n_in]
    ctx = refs[n_in:2 * n_in] if has_ctx else ()
    rest = refs[(2 if has_ctx else 1) * n_in:]
    w_refs, (x_ref, gt_ref, o_ref) = rest[:n_in], rest[n_in:]

    def run(a_refs):
        acc = None
        for a_ref, w_ref in zip(a_refs, w_refs):
            t = _dot(a_ref[...], w_ref[...])
            acc = t if acc is None else acc + t
        o_ref[...] = x_ref[...] + gt_ref[...] * acc

    if has_ctx:
        i = pl.program_id(0)
        pl.when(i < lat_blocks)(lambda: run(lat))
        pl.when(i >= lat_blocks)(lambda: run(ctx))
    else:
        run(lat)


def _proj_residual(xa, lat_ins, ctx_ins, ws, mod_l, gate_k, blocks_per_batch, n_blocks):
    bm, bn = ROW_BLOCK, 512
    n_in = len(lat_ins)
    has_ctx = ctx_ins is not None
    lat_blocks = lat_ins[0].shape[0] // bm
    in_specs = [pl.BlockSpec((bm, a.shape[1]), lambda i, j: (jnp.minimum(i, lat_blocks - 1), 0)) for a in lat_ins]
    if has_ctx:
        in_specs += [pl.BlockSpec((bm, a.shape[1]), lambda i, j: (jnp.maximum(i - lat_blocks, 0), 0))
                     for a in ctx_ins]
    in_specs += [pl.BlockSpec((w.shape[0], bn), lambda i, j: (0, j)) for w in ws]
    in_specs += [
        pl.BlockSpec((bm, bn), lambda i, j: (i, j)),
        pl.BlockSpec((None, 1, bn), lambda i, j: (i // blocks_per_batch, 0, gate_k * (D_MODEL // bn) + j)),
    ]
    args = list(lat_ins) + (list(ctx_ins) if has_ctx else []) + list(ws) + [xa, mod_l]
    return pl.pallas_call(
        functools.partial(_proj_residual_kernel, n_in=n_in, lat_blocks=lat_blocks, has_ctx=has_ctx),
        grid=(n_blocks, D_MODEL // bn),
        in_specs=in_specs,
        out_specs=pl.BlockSpec((bm, bn), lambda i, j: (i, j)),
        out_shape=jax.ShapeDtypeStruct(xa.shape, F32),
        input_output_aliases={len(args) - 2: 0},
        compiler_params=_params("arbitrary", "arbitrary"),
        name="projection_residual",
    )(*args)


def _ffn_up_kernel(xm_ref, xp_ref, xn_ref, sh_ref, sc_ref, g_ref, wg_ref, wv_ref, cwg_ref, cwv_ref,
                   cbg_ref, cbv_ref, o_ref, h_ref, *, lat_blocks, seq, ctx_len):
    bm = ROW_BLOCK
    i = pl.program_id(0)

    @pl.when(pl.program_id(1) == 0)
    def _():
        g, sc, sh = g_ref[...], sc_ref[...], sh_ref[...]
        h_ref[0:HALO, :] = _normmod(xp_ref[...], g, sc, sh).astype(BF16)
        h_ref[HALO:HALO + bm, :] = _normmod(xm_ref[...], g, sc, sh).astype(BF16)
        h_ref[HALO + bm:, :] = _normmod(xn_ref[...], g, sc, sh).astype(BF16)

    h = h_ref[...]
    rows = lax.broadcasted_iota(jnp.int32, (bm + 2 * HALO, 1), 0) + (i * bm - HALO)
    period = jnp.where(i < lat_blocks, seq, ctx_len)
    first = (rows & (period - 1)) == 0
    last = ((rows + 1) & (period - 1)) == 0

    def conv(w_ref, cw_ref, cb_ref):
        u = _dot(h, w_ref[...])
        prev = jnp.where(first, 0.0, pltpu.roll(u, 1, axis=0))
        nxt = jnp.where(last, 0.0, pltpu.roll(u, bm + 2 * HALO - 1, axis=0))
        return prev * cw_ref[0:1, :] + u * cw_ref[1:2, :] + nxt * cw_ref[2:3, :] + cb_ref[...]

    gate = conv(wg_ref, cwg_ref, cbg_ref)
    val = conv(wv_ref, cwv_ref, cbv_ref)
    a = gate * _sigmoid(gate) * val
    o_ref[...] = a[HALO:HALO + bm].astype(BF16)


def _ffn_up(xa, mod_l, g, w_up, conv_w, conv_b, blocks_per_batch, n_blocks, lat_blocks, seq, ctx_len):
    rows = xa.shape[0]
    bm, bn = ROW_BLOCK, 512
    nj = D_FF // bn
    hb = bm // HALO
    last_halo = rows // HALO - 1
    assert seq & (seq - 1) == 0 and ctx_len & (ctx_len - 1) == 0
    return pl.pallas_call(
        functools.partial(_ffn_up_kernel, lat_blocks=lat_blocks, seq=seq, ctx_len=ctx_len),
        grid=(n_blocks, nj),
        in_specs=[
            pl.BlockSpec((bm, D_MODEL), lambda i, j: (i, 0)),
            pl.BlockSpec((HALO, D_MODEL), lambda i, j: (jnp.maximum(i * hb - 1, 0), 0)),
            pl.BlockSpec((HALO, D_MODEL), lambda i, j: (jnp.minimum((i + 1) * hb, last_halo), 0)),
            _mod_spec(3, blocks_per_batch),
            _mod_spec(4, blocks_per_batch),
            pl.BlockSpec((1, D_MODEL), lambda i, j: (0, 0)),
            pl.BlockSpec((D_MODEL, bn), lambda i, j: (0, j)),
            pl.BlockSpec((D_MODEL, bn), lambda i, j: (0, nj + j)),
            pl.BlockSpec((CONV_W, bn), lambda i, j: (0, j)),
            pl.BlockSpec((CONV_W, bn), lambda i, j: (0, nj + j)),
            pl.BlockSpec((1, bn), lambda i, j: (0, j)),
            pl.BlockSpec((1, bn), lambda i, j: (0, nj + j)),
        ],
        out_specs=pl.BlockSpec((bm, bn), lambda i, j: (i, j)),
        out_shape=jax.ShapeDtypeStruct((n_blocks * bm, D_FF), BF16),
        scratch_shapes=[pltpu.VMEM((bm + 2 * HALO, D_MODEL), BF16)],
        compiler_params=_params("arbitrary", "arbitrary"),
        name="ffn_up_conv_gate",
    )(xa, xa, xa, mod_l, mod_l, g, w_up, w_up, conv_w, conv_w, conv_b, conv_b)


def _final_norm_kernel(x_ref, g_ref, o_ref):
    o_ref[...] = _rms(x_ref[...], g_ref[...])


def _final_norm(xa, g, rows):
    bm = ROW_BLOCK
    return pl.pallas_call(
        _final_norm_kernel,
        grid=(rows // bm,),
        in_specs=[pl.BlockSpec((bm, D_MODEL), lambda i: (i, 0)), pl.BlockSpec((1, D_MODEL), lambda i: (0, 0))],
        out_specs=pl.BlockSpec((bm, D_MODEL), lambda i: (i, 0)),
        out_shape=jax.ShapeDtypeStruct((rows, D_MODEL), F32),
        compiler_params=_params("arbitrary"),
        name="final_norm",
    )(xa, g)


def _rope_table(seq, pad_rows):
    n = MLA_ROPE // 4
    freqs = ROPE_BASE ** (-jnp.arange(n, dtype=F32) / n)
    pos = jnp.arange(seq)
    ang_r = (pos // GRID_W).astype(F32)[:, None] * freqs
    ang_c = (pos % GRID_W).astype(F32)[:, None] * freqs
    cos = jnp.concatenate([jnp.cos(ang_r)] * 2 + [jnp.cos(ang_c)] * 2, axis=1)
    sin = jnp.concatenate([-jnp.sin(ang_r), jnp.sin(ang_r), -jnp.sin(ang_c), jnp.sin(ang_c)], axis=1)
    lat = jnp.concatenate([cos, sin], axis=1)
    ident = jnp.concatenate([jnp.ones((pad_rows, MLA_ROPE), F32), jnp.zeros((pad_rows, MLA_ROPE), F32)], axis=1)
    return jnp.concatenate([lat, ident], axis=0)


def _partner_perm():
    q = MLA_ROPE // 4
    return np.concatenate([np.arange(q, 2 * q), np.arange(0, q), np.arange(3 * q, 4 * q), np.arange(2 * q, 3 * q)])


def _layout_w_in(w):
    cq, ckv, kr, qn, kn, vn, f = jnp.split(w, np.cumsum(
        (MLA_Q_RANK, MLA_KV_RANK, MLA_ROPE, NA_WIDTH, NA_WIDTH, NA_WIDTH))[:].tolist(), axis=1)
    return jnp.concatenate([cq, ckv, qn, kn, vn, f, kr, kr[:, _partner_perm()]], axis=1).astype(BF16)


def _layout_w_uq(w):
    w = w.reshape(MLA_Q_RANK, MLA_HEADS, MLA_NOPE + MLA_ROPE)
    rope = w[:, :, MLA_NOPE:]
    return jnp.concatenate([w, rope[:, :, _partner_perm()]], axis=2).reshape(MLA_Q_RANK, -1).astype(BF16)


def _layout_w_ukv(w):
    w = w.reshape(MLA_KV_RANK, MLA_HEADS, MLA_NOPE + MLA_V)
    return jnp.concatenate([w[:, :, :MLA_NOPE].reshape(MLA_KV_RANK, -1),
                            w[:, :, MLA_NOPE:].reshape(MLA_KV_RANK, -1)], axis=1).astype(BF16)


def kernel(x, c, ctx, c_ctx, w_mod, b_mod, g_attn, g_ffn, w_in, g_q, w_uq, g_kv, w_ukv, na_rpb, w_fnet, w_out,
           w_up, conv_w, conv_b, w_down, g_final):
    batch, seq, d = x.shape
    ctx_len = ctx.shape[1]
    depth = w_mod.shape[0]
    n_lat, n_ctx = batch * seq, batch * ctx_len
    bm = ROW_BLOCK
    lat_blocks = n_lat // bm
    all_blocks = (n_lat + n_ctx) // bm
    blocks_per_batch = seq // bm
    assert n_ctx % bm == 0 and n_lat // bm // blocks_per_batch == batch and batch < 8

    xa = jnp.concatenate([x.reshape(n_lat, d), ctx.reshape(n_ctx, d)], axis=0)
    cin = jnp.zeros((8, d), F32).at[:batch].set(c).at[batch].set(c_ctx)
    mod = _modulation(cin, w_mod, b_mod)
    cs_tab = _rope_table(seq, bm)
    row = lambda v: v.reshape(1, -1)

    for l in range(depth):
        ctx_out = l < depth - 1
        mod_l = mod[l].reshape(8, 1, 6 * d)
        oc, on, of, okr = _inproj(xa, mod_l, row(g_attn[l]), _layout_w_in(w_in[l]), blocks_per_batch)
        qm, km, vm = _mla_up(oc, okr, cs_tab, row(g_q[l]), row(g_kv[l]), _layout_w_uq(w_uq[l]),
                             _layout_w_ukv(w_ukv[l]), lat_blocks, blocks_per_batch)

        o_mla = _attention(qm, km, vm, batch=batch, heads=MLA_HEADS, dq=MLA_QK_PAD, dv=MLA_V, bq=1024, ck=1024,
                           seq=seq, ctx_len=ctx_len, k_col0=0, v_col0=0, name="mla_attention")
        o_na = _neighborhood_attention(on, _na_table(na_rpb[l], seq // GRID_W), batch, seq, ctx_len)
        o_fn = _fourier_latent(of, w_fnet[l].astype(BF16), batch, seq)
        w_o = w_out[l].astype(BF16)
        w_os = (w_o[:MLA_WIDTH], w_o[MLA_WIDTH:MLA_WIDTH + NA_WIDTH], w_o[MLA_WIDTH + NA_WIDTH:])

        ctx_ins = None
        if ctx_out:
            o_mla_c = _context_attention(
                qm, km, vm, batch=batch, heads=MLA_HEADS, dq=MLA_QK_PAD, dv=MLA_V, row0=n_lat, ctx_len=ctx_len,
                k_col0=0, v_col0=0, name="mla_attention_context")
            o_na_c = _context_attention(
                on, on, on, batch=batch, heads=NA_HEADS, dq=NA_HEAD_DIM, dv=NA_HEAD_DIM, row0=n_lat,
                ctx_len=ctx_len, k_col0=NA_HEADS, v_col0=2 * NA_HEADS, name="na_attention_context")
            o_fn_c = _fourier_context(of, w_fnet[l].astype(BF16), batch, n_lat, ctx_len)
            ctx_ins = (o_mla_c, o_na_c, o_fn_c)

        n_blocks = all_blocks if ctx_out else lat_blocks
        xa = _proj_residual(xa, (o_mla, o_na, o_fn), ctx_ins, w_os, mod_l, 2, blocks_per_batch, n_blocks)
        a = _ffn_up(xa, mod_l, row(g_ffn[l]), w_up[l].astype(BF16), conv_w[l], row(conv_b[l]),
                    blocks_per_batch, n_blocks, lat_blocks, seq, ctx_len)
        xa = _proj_residual(xa, (a,), None, (w_down[l].astype(BF16),), mod_l, 5, blocks_per_batch, n_blocks)

    out = _final_norm(xa, row(g_final), n_lat)
    return out.reshape(batch, seq, d)
```

```python
import functools

import numpy as np
import jax
import jax.numpy as jnp
from jax import lax
from jax.experimental import pallas as pl
from jax.experimental.pallas import tpu as pltpu

F32 = jnp.float32
BF16 = jnp.bfloat16

D_MODEL = 2048
GRID_W = 64
EPS = 1e-6

MLA_HEADS = 8
MLA_NOPE = 128
MLA_ROPE = 64
MLA_V = 128
MLA_Q_RANK = 512
MLA_KV_RANK = 512
MLA_QK_PAD = 256
MLA_SCALE = (MLA_NOPE + MLA_ROPE) ** -0.5
ROPE_BASE = 10000.0

NA_HEADS = 4
NA_HEAD_DIM = 128
NA_KH = 8
NA_KW = 16
NA_SCALE = NA_HEAD_DIM ** -0.5
NA_QROWS = 8
NA_KROWS = NA_QROWS + NA_KH - 1

FN_GROUPS = 4
FN_CH = 128
FN_WIDTH = FN_GROUPS * FN_CH
FN_K2 = 16

MLA_WIDTH = MLA_HEADS * MLA_V
NA_WIDTH = NA_HEADS * NA_HEAD_DIM
D_FF = 5632
FFN_TILE = 128
CONV_W = 3

ROW_BLOCK = 512
HALO = 16
PROJ_COLS = 512
MASK_VALUE = -1e30
LOG2E = 1.4426950408889634
VMEM_LIMIT = 56 * 1024 * 1024


def _params(*sem, flags=None):
    return pltpu.CompilerParams(dimension_semantics=sem, vmem_limit_bytes=VMEM_LIMIT, flags=flags)


def _dot(a, b):
    return jnp.dot(a, b, preferred_element_type=F32)


def _dot_nt(a, b):
    return lax.dot_general(a, b, (((1,), (1,)), ((), ())), preferred_element_type=F32)


def _split(x):
    hi = x.astype(BF16)
    lo = (x - hi.astype(F32)).astype(BF16)
    return hi, lo


def _dot3_left(m_hi, m_lo, x):
    x_hi, x_lo = _split(x)
    return _dot(m_hi, x_hi) + (_dot(m_hi, x_lo) + _dot(m_lo, x_hi))


def _dot3_right(x, m_hi, m_lo):
    x_hi, x_lo = _split(x)
    return _dot(x_hi, m_hi) + (_dot(x_lo, m_hi) + _dot(x_hi, m_lo))


def _sigmoid(x):
    return 1.0 / (1.0 + jnp.exp(-x))


def _rms(x, g):
    y = x * lax.rsqrt(jnp.mean(x * x, axis=-1, keepdims=True) + EPS)
    return y * g


def _normmod(x, g, sc, sh):
    return _rms(x, g) * (1.0 + sc) + sh


def _mod_kernel(c_ref, w_ref, b_ref, o_ref):
    c = c_ref[...]
    s = (c * _sigmoid(c)).astype(BF16)
    o_ref[...] = _dot(s, w_ref[...].astype(BF16)) + b_ref[...]


def _modulation(cin, w_mod, b_mod):
    depth, d, n = w_mod.shape
    bn = 1024
    return pl.pallas_call(
        _mod_kernel,
        grid=(depth, n // bn),
        in_specs=[
            pl.BlockSpec((8, d), lambda l, j: (0, 0)),
            pl.BlockSpec((None, d, bn), lambda l, j: (l, 0, j)),
            pl.BlockSpec((None, 1, bn), lambda l, j: (l, 0, j)),
        ],
        out_specs=pl.BlockSpec((None, 8, bn), lambda l, j: (l, 0, j)),
        out_shape=jax.ShapeDtypeStruct((depth, 8, n), F32),
        compiler_params=_params("arbitrary", "arbitrary"),
        name="modulation",
    )(cin, w_mod, b_mod.reshape(depth, 1, n))


def _inproj_kernel(x_ref, sh_ref, sc_ref, g_ref, w_ref, oc_ref, on_ref, of_ref, okr_ref):
    h = _normmod(x_ref[...], g_ref[...], sc_ref[...], sh_ref[...]).astype(BF16)
    oc_ref[...] = _dot(h, w_ref[:, 0:1024])
    on_ref[:, 0:NA_WIDTH] = (_dot(h, w_ref[:, 1024:1024 + NA_WIDTH]) * (NA_SCALE * LOG2E)).astype(BF16)
    on_ref[:, NA_WIDTH:] = _dot(h, w_ref[:, 1024 + NA_WIDTH:2560]).astype(BF16)
    of_ref[...] = _dot(h, w_ref[:, 2560:3072])
    okr_ref[...] = _dot(h, w_ref[:, 3072:3200])


def _mod_spec(k, blocks_per_batch):
    return pl.BlockSpec((None, 1, D_MODEL), lambda i, *_: (i // blocks_per_batch, 0, k))


def _inproj(xa, mod_l, g, w_p, blocks_per_batch):
    rows = xa.shape[0]
    bm = ROW_BLOCK
    row = lambda w: pl.BlockSpec((bm, w), lambda i: (i, 0))
    return pl.pallas_call(
        _inproj_kernel,
        grid=(rows // bm,),
        in_specs=[
            row(D_MODEL),
            _mod_spec(0, blocks_per_batch),
            _mod_spec(1, blocks_per_batch),
            pl.BlockSpec((1, D_MODEL), lambda i: (0, 0)),
            pl.BlockSpec(w_p.shape, lambda i: (0, 0), pipeline_mode=pl.Buffered(1)),
        ],
        out_specs=[row(1024), row(1536), row(512), row(128)],
        out_shape=[
            jax.ShapeDtypeStruct((rows, 1024), F32),
            jax.ShapeDtypeStruct((rows, 1536), BF16),
            jax.ShapeDtypeStruct((rows, 512), F32),
            jax.ShapeDtypeStruct((rows, 128), F32),
        ],
        compiler_params=_params("arbitrary"),
        name="in_projection",
    )(xa, mod_l, mod_l, g, w_p)


def _rope(t, cs):
    t = t * cs
    return t + pltpu.roll(t, 64, axis=1)


def _mla_up_kernel(c_ref, kr_ref, cs_ref, gq_ref, gkv_ref, wq_ref, wkv_ref, q_ref, k_ref, v_ref):
    cs = cs_ref[...]
    cq = _rms(c_ref[:, 0:MLA_Q_RANK], gq_ref[...]).astype(BF16)
    ckv = _rms(c_ref[:, MLA_Q_RANK:MLA_Q_RANK + MLA_KV_RANK], gkv_ref[...]).astype(BF16)
    q = _dot(cq, wq_ref[...]) * (MLA_SCALE * LOG2E)
    kv = _dot(ckv, wkv_ref[...])
    lane = lax.broadcasted_iota(jnp.int32, cs.shape, 1)
    k_rope = jnp.where(lane < MLA_ROPE, _rope(kr_ref[...], cs), 0.0).astype(BF16)
    for h in range(MLA_HEADS):
        o = h * MLA_QK_PAD
        q_ref[:, o:o + MLA_NOPE] = q[:, o:o + MLA_NOPE].astype(BF16)
        q_ref[:, o + MLA_NOPE:o + MLA_QK_PAD] = _rope(q[:, o + MLA_NOPE:o + MLA_QK_PAD], cs).astype(BF16)
        k_ref[:, o:o + MLA_NOPE] = kv[:, h * MLA_NOPE:(h + 1) * MLA_NOPE].astype(BF16)
        k_ref[:, o + MLA_NOPE:o + MLA_QK_PAD] = k_rope
    v_ref[...] = kv[:, MLA_HEADS * MLA_NOPE:].astype(BF16)


def _mla_up(oc, okr, cs_tab, g_q, g_kv, wq_p, wkv_p, lat_blocks, pos_blocks):
    rows = oc.shape[0]
    bm = ROW_BLOCK
    row = lambda w: pl.BlockSpec((bm, w), lambda i: (i, 0))
    const = lambda a: pl.BlockSpec(a.shape, lambda i: (0, 0))
    cs_spec = pl.BlockSpec((bm, 128), lambda i: (jnp.where(i < lat_blocks, i % pos_blocks, pos_blocks), 0))
    qk_w = MLA_HEADS * MLA_QK_PAD
    return pl.pallas_call(
        _mla_up_kernel,
        grid=(rows // bm,),
        in_specs=[row(1024), row(128), cs_spec, const(g_q), const(g_kv), const(wq_p), const(wkv_p)],
        out_specs=[row(qk_w), row(qk_w), row(MLA_WIDTH)],
        out_shape=[
            jax.ShapeDtypeStruct((rows, qk_w), BF16),
            jax.ShapeDtypeStruct((rows, qk_w), BF16),
            jax.ShapeDtypeStruct((rows, MLA_WIDTH), BF16),
        ],
        compiler_params=_params("arbitrary"),
        name="mla_up_projection",
    )(oc, okr, cs_tab, g_q, g_kv, wq_p, wkv_p)


def _softmax_first(s, v):
    m = jnp.max(s, axis=-1, keepdims=True)
    p = jnp.exp2(s - m)
    return m, jnp.sum(p, axis=-1, keepdims=True), _dot(p.astype(BF16), v)


def _softmax_next(s, s_max, v, m, l, acc):
    m_new = jnp.maximum(m, s_max)
    p = jnp.exp2(s - m_new)
    alpha = jnp.exp2(m - m_new)
    l = alpha * l + jnp.sum(p, axis=-1, keepdims=True)
    acc = alpha * acc + _dot(p.astype(BF16), v)
    return m_new, l, acc


def _attention_kernel(q_ref, k_ref, v_ref, kc_ref, vc_ref, o_ref, sa_ref, sb_ref, *, ck, n_main):
    q = q_ref[...]
    carry = _softmax_first(_dot_nt(q, kc_ref[...]), vc_ref[...])
    bufs = (sa_ref, sb_ref)

    def scores(c):
        s = _dot_nt(q, k_ref[c * ck:(c + 1) * ck, :])
        bufs[c % 2][...] = s
        return jnp.max(s, axis=-1, keepdims=True)

    s_max = scores(0)
    for c in range(n_main):
        nxt_max = scores(c + 1) if c + 1 < n_main else None
        carry = _softmax_next(bufs[c % 2][...], s_max, v_ref[c * ck:(c + 1) * ck, :], *carry)
        s_max = nxt_max
    _, l, acc = carry
    o_ref[...] = (acc / l).astype(o_ref.dtype)


def _attention(q_arr, k_arr, v_arr, *, batch, heads, dq, dv, bq, seq, ctx_len, k_col0, v_col0, name, ck=512):
    nq = seq // bq
    ctx0 = batch * seq // ctx_len
    return pl.pallas_call(
        functools.partial(_attention_kernel, ck=ck, n_main=seq // ck),
        grid=(batch, heads, nq),
        in_specs=[
            pl.BlockSpec((bq, dq), lambda b, h, i: (b * nq + i, h)),
            pl.BlockSpec((seq, dq), lambda b, h, i: (b, k_col0 + h)),
            pl.BlockSpec((seq, dv), lambda b, h, i: (b, v_col0 + h)),
            pl.BlockSpec((ctx_len, dq), lambda b, h, i: (ctx0 + b, k_col0 + h)),
            pl.BlockSpec((ctx_len, dv), lambda b, h, i: (ctx0 + b, v_col0 + h)),
        ],
        out_specs=pl.BlockSpec((bq, dv), lambda b, h, i: (b * nq + i, h)),
        out_shape=jax.ShapeDtypeStruct((batch * seq, heads * dv), BF16),
        scratch_shapes=[pltpu.VMEM((bq, ck), F32), pltpu.VMEM((bq, ck), F32)],
        compiler_params=_params("arbitrary", "arbitrary", "arbitrary"),
        name=name,
    )(q_arr, k_arr, v_arr, k_arr, v_arr)


def _context_attention_kernel(q_ref, k_ref, v_ref, o_ref):
    _, l, acc = _softmax_first(_dot_nt(q_ref[...], k_ref[...]), v_ref[...])
    o_ref[...] = (acc / l).astype(o_ref.dtype)


def _context_attention(q_arr, k_arr, v_arr, *, batch, heads, dq, dv, row0, ctx_len, k_col0, v_col0, name):
    blk0 = row0 // ctx_len
    return pl.pallas_call(
        _context_attention_kernel,
        grid=(batch, heads),
        in_specs=[
            pl.BlockSpec((ctx_len, dq), lambda b, h: (blk0 + b, h)),
            pl.BlockSpec((ctx_len, dq), lambda b, h: (blk0 + b, k_col0 + h)),
            pl.BlockSpec((ctx_len, dv), lambda b, h: (blk0 + b, v_col0 + h)),
        ],
        out_specs=pl.BlockSpec((ctx_len, dv), lambda b, h: (b, h)),
        out_shape=jax.ShapeDtypeStruct((batch * ctx_len, heads * dv), BF16),
        compiler_params=_params("arbitrary", "arbitrary"),
        name=name,
    )(q_arr, k_arr, v_arr)


def _na_kernel(q_ref, k_ref, v_ref, kc_ref, vc_ref, t_ref, o_ref, *, grid_rows):
    r = pl.program_id(2)
    ks = jnp.clip(r * NA_QROWS - NA_KH // 2, 0, grid_rows - NA_KROWS)
    start = pl.multiple_of(ks * GRID_W, GRID_W)
    nwin = NA_KROWS * GRID_W
    q = q_ref[...]
    s_win = _dot_nt(q, k_ref[pl.ds(start, nwin), :]) + t_ref[...]
    s_ctx = _dot_nt(q, kc_ref[...])
    m = jnp.maximum(jnp.max(s_win, axis=-1, keepdims=True), jnp.max(s_ctx, axis=-1, keepdims=True))
    p_win = jnp.exp2(s_win - m)
    p_ctx = jnp.exp2(s_ctx - m)
    l = jnp.sum(p_win, axis=-1, keepdims=True) + jnp.sum(p_ctx, axis=-1, keepdims=True)
    acc = _dot(p_win.astype(BF16), v_ref[pl.ds(start, nwin), :]) + _dot(p_ctx.astype(BF16), vc_ref[...])
    o_ref[...] = (acc / l).astype(o_ref.dtype)


def _na_table(rpb, grid_rows):
    tabs = []
    for r0 in (0, NA_QROWS, grid_rows - NA_QROWS):
        ks = min(max(r0 - NA_KH // 2, 0), grid_rows - NA_KROWS)
        r = r0 + np.arange(NA_QROWS)[:, None, None, None]
        w = np.arange(GRID_W)[None, :, None, None]
        kr = ks + np.arange(NA_KROWS)[None, None, :, None]
        j = np.arange(GRID_W)[None, None, None, :]
        rs = np.clip(r - NA_KH // 2, 0, grid_rows - NA_KH)
        cs = np.clip(w - NA_KW // 2, 0, GRID_W - NA_KW)
        valid = (kr >= rs) & (kr < rs + NA_KH) & (j >= cs) & (j < cs + NA_KW)
        sel_r = ((kr - r + (NA_KH - 1))[:, 0, :, 0, None] == np.arange(2 * NA_KH - 1)).astype(np.float32)
        sel_c = ((j - w + (NA_KW - 1))[0, :, 0, :, None] == np.arange(2 * NA_KW - 1)).astype(np.float32)
        bias = jnp.einsum("rap,hpq,wjq->hrwaj", sel_r, rpb.astype(F32), sel_c,
                          precision=lax.Precision.HIGHEST)
        tab = jnp.where(np.broadcast_to(valid, bias.shape[1:])[None], bias * LOG2E, MASK_VALUE)
        tabs.append(tab.reshape(rpb.shape[0], NA_QROWS * GRID_W, NA_KROWS * GRID_W))
    return jnp.stack(tabs)


def _neighborhood_attention(on, table, batch, seq, ctx_len):
    grid_rows = seq // GRID_W
    nblk = grid_rows // NA_QROWS
    bq = NA_QROWS * GRID_W
    d = NA_HEAD_DIM
    ctx0 = batch * seq // ctx_len
    return pl.pallas_call(
        functools.partial(_na_kernel, grid_rows=grid_rows),
        grid=(batch, NA_HEADS, nblk),
        in_specs=[
            pl.BlockSpec((bq, d), lambda b, h, r: (b * nblk + r, h)),
            pl.BlockSpec((seq, d), lambda b, h, r: (b, NA_HEADS + h)),
            pl.BlockSpec((seq, d), lambda b, h, r: (b, 2 * NA_HEADS + h)),
            pl.BlockSpec((ctx_len, d), lambda b, h, r: (ctx0 + b, NA_HEADS + h)),
            pl.BlockSpec((ctx_len, d), lambda b, h, r: (ctx0 + b, 2 * NA_HEADS + h)),
            pl.BlockSpec((None, None, bq, NA_KROWS * GRID_W),
                         lambda b, h, r: (jnp.where(r == 0, 0, jnp.where(r == nblk - 1, 2, 1)), h, 0, 0)),
        ],
        out_specs=pl.BlockSpec((bq, d), lambda b, h, r: (b * nblk + r, h)),
        out_shape=jax.ShapeDtypeStruct((batch * seq, NA_WIDTH), BF16),
        compiler_params=_params("arbitrary", "arbitrary", "arbitrary"),
        name="neighborhood_attention",
    )(on, on, on, on, on, table)


def _dft_consts(n):
    jk = (np.arange(n)[:, None] * np.arange(n)[None, :]) % n
    ang = 2.0 * np.pi * jk / n
    return np.cos(ang), np.sin(ang)


def _hi_lo(m):
    m = jnp.asarray(m, F32)
    hi = m.astype(BF16)
    return hi, (m - hi.astype(F32)).astype(BF16)


def _fn_stage1_kernel(x_ref, fh_ref, fl_ref, tc_ref, ts_ref, o_ref):
    n = GRID_W
    b = _dot3_left(fh_ref[...], fl_ref[...], x_ref[...])
    br, bi = b[:n], b[n:]
    tc, ts = tc_ref[...], ts_ref[...]
    o_ref[0] = br * tc + bi * ts
    o_ref[1] = bi * tc - br * ts


def _fn_stage2_kernel(t_ref, gh_ref, gl_ref, ch_ref, cl_ref, w_ref, p_ref, o_ref):
    n = GRID_W
    outs = []
    for j in range(FN_K2):
        t = jnp.concatenate([t_ref[0, j], t_ref[1, j]], axis=0)
        a = _dot3_left(gh_ref[...], gl_ref[...], t)
        outs.append(a)
    ar = jnp.concatenate([a[:n] for a in outs], axis=0)
    ai = jnp.concatenate([a[n:] for a in outs], axis=0)
    ys = []
    for g in range(FN_GROUPS):
        sl = slice(g * FN_CH, (g + 1) * FN_CH)
        z = jnp.concatenate([ar[:, sl], ai[:, sl]], axis=1)
        spec = _dot3_right(z, ch_ref[...], cl_ref[...])
        ys.append(_dot(spec.astype(BF16), w_ref[g]).astype(BF16))
    y = jnp.concatenate(ys, axis=1)
    y = _dot(p_ref[...], y).astype(BF16)
    o_ref[...] = y.reshape(n, FN_K2, FN_WIDTH)


def _fourier_latent(of, w_fnet, batch, seq):
    n = GRID_W
    assert seq == n * n
    cols = n * FN_WIDTH
    c64, s64 = _dft_consts(n)
    f1h, f1l = _hi_lo(np.concatenate([c64, -s64], axis=0))
    tw = 2.0 * np.pi * (np.arange(n)[:, None] * np.arange(n)[None, :]) / (n * n)
    tc = jnp.asarray(np.repeat(np.cos(tw).reshape(n, n, 1), FN_WIDTH, axis=2).reshape(n, cols), F32)
    ts = jnp.asarray(np.repeat(np.sin(tw).reshape(n, n, 1), FN_WIDTH, axis=2).reshape(n, cols), F32)
    x2 = of[:batch * seq].reshape(batch, n, cols)
    cb = 4096
    t = pl.pallas_call(
        _fn_stage1_kernel,
        grid=(batch, cols // cb),
        in_specs=[
            pl.BlockSpec((None, n, cb), lambda b, j: (b, 0, j)),
            pl.BlockSpec(f1h.shape, lambda b, j: (0, 0)),
            pl.BlockSpec(f1l.shape, lambda b, j: (0, 0)),
            pl.BlockSpec((n, cb), lambda b, j: (0, j)),
            pl.BlockSpec((n, cb), lambda b, j: (0, j)),
        ],
        out_specs=pl.BlockSpec((None, 2, n, cb), lambda b, j: (b, 0, 0, j)),
        out_shape=jax.ShapeDtypeStruct((batch, 2, n, cols), F32),
        compiler_params=_params("arbitrary", "arbitrary"),
        name="fourier_rows",
    )(x2, f1h, f1l, tc, ts)
    t = t.reshape(batch, 2, n, n, FN_WIDTH)

    g2h, g2l = _hi_lo(np.block([[c64, s64], [-s64, c64]]))
    cc, sc = _dft_consts(FN_CH)
    norm = 1.0 / np.sqrt(seq * FN_CH)
    c4h, c4l = _hi_lo(np.concatenate([cc, sc], axis=0) * norm)
    rows = FN_K2 * n
    perm = np.zeros((rows, rows), np.float32)
    k1 = np.arange(n)[:, None]
    j = np.arange(FN_K2)[None, :]
    perm[(k1 * FN_K2 + j).ravel(), (j * n + k1).ravel()] = 1.0
    perm = jnp.asarray(perm, BF16)
    const = lambda a: pl.BlockSpec(a.shape, lambda b, i: (0,) * a.ndim)
    y = pl.pallas_call(
        _fn_stage2_kernel,
        grid=(batch, n // FN_K2),
        in_specs=[
            pl.BlockSpec((None, 2, FN_K2, n, FN_WIDTH), lambda b, i: (b, 0, i, 0, 0)),
            const(g2h), const(g2l), const(c4h), const(c4l), const(w_fnet), const(perm),
        ],
        out_specs=pl.BlockSpec((None, n, FN_K2, FN_WIDTH), lambda b, i: (b, 0, i, 0)),
        out_shape=jax.ShapeDtypeStruct((batch, n, n, FN_WIDTH), BF16),
        compiler_params=_params("arbitrary", "arbitrary"),
        name="fourier_cols_channels",
    )(t, g2h, g2l, c4h, c4l, w_fnet, perm)
    return y.reshape(batch * seq, FN_WIDTH)


def _fn_ctx_kernel(x_ref, fh_ref, fl_ref, ch_ref, cl_ref, w_ref, o_ref, *, n):
    a = _dot3_left(fh_ref[...], fl_ref[...], x_ref[...])
    ar, ai = a[:n], a[n:]
    ys = []
    for g in range(FN_GROUPS):
        sl = slice(g * FN_CH, (g + 1) * FN_CH)
        z = jnp.concatenate([ar[:, sl], ai[:, sl]], axis=1)
        spec = _dot3_right(z, ch_ref[...], cl_ref[...])
        ys.append(_dot(spec.astype(BF16), w_ref[g]).astype(BF16))
    o_ref[...] = jnp.concatenate(ys, axis=1)


def _fourier_context(of, w_fnet, batch, row0, n):
    c, s = _dft_consts(n)
    fh, fl = _hi_lo(np.concatenate([c, -s], axis=0))
    cc, sc = _dft_consts(FN_CH)
    c4h, c4l = _hi_lo(np.concatenate([cc, sc], axis=0) / np.sqrt(n * FN_CH))
    const = lambda a: pl.BlockSpec(a.shape, lambda b: (0,) * a.ndim)
    blk0 = row0 // n
    return pl.pallas_call(
        functools.partial(_fn_ctx_kernel, n=n),
        grid=(batch,),
        in_specs=[pl.BlockSpec((n, FN_WIDTH), lambda b: (blk0 + b, 0)),
                  const(fh), const(fl), const(c4h), const(c4l), const(w_fnet)],
        out_specs=pl.BlockSpec((n, FN_WIDTH), lambda b: (b, 0)),
        out_shape=jax.ShapeDtypeStruct((batch * n, FN_WIDTH), BF16),
        compiler_params=_params("arbitrary"),
        name="fourier_context",
    )(of, fh, fl, c4h, c4l, w_fnet)


def _proj_residual_kernel(*refs, n_in, lat_blocks, has_ctx):
    lat = refs[:n_in]
    ctx = refs[n_in:2 * n_in] if has_ctx else ()
    rest = refs[(2 if has_ctx else 1) * n_in:]
    w_refs, (x_ref, gt_ref, o_ref) = rest[:n_in], rest[n_in:]

    def run(a_refs):
        for n in range(0, D_MODEL, PROJ_COLS):
            cols = slice(n, n + PROJ_COLS)
            acc = None
            for a_ref, w_ref in zip(a_refs, w_refs):
                t = _dot(a_ref[...], w_ref[:, cols])
                acc = t if acc is None else acc + t
            o_ref[:, cols] = x_ref[:, cols] + gt_ref[:, cols] * acc

    if has_ctx:
        i = pl.program_id(0)
        pl.when(i < lat_blocks)(lambda: run(lat))
        pl.when(i >= lat_blocks)(lambda: run(ctx))
    else:
        run(lat)


def _proj_residual(xa, lat_ins, ctx_ins, ws, mod_l, gate_k, rows_per_batch, n_rows, bm):
    n_in = len(lat_ins)
    has_ctx = ctx_ins is not None
    lat_blocks = lat_ins[0].shape[0] // bm
    bpb = rows_per_batch // bm
    in_specs = [pl.BlockSpec((bm, a.shape[1]), lambda i: (jnp.minimum(i, lat_blocks - 1), 0)) for a in lat_ins]
    if has_ctx:
        in_specs += [pl.BlockSpec((bm, a.shape[1]), lambda i: (jnp.maximum(i - lat_blocks, 0), 0)) for a in ctx_ins]
    in_specs += [pl.BlockSpec(w.shape, lambda i: (0, 0), pipeline_mode=pl.Buffered(1)) for w in ws]
    in_specs += [
        pl.BlockSpec((bm, D_MODEL), lambda i: (i, 0)),
        pl.BlockSpec((None, 1, D_MODEL), lambda i: (i // bpb, 0, gate_k)),
    ]
    args = list(lat_ins) + (list(ctx_ins) if has_ctx else []) + list(ws) + [xa, mod_l]
    return pl.pallas_call(
        functools.partial(_proj_residual_kernel, n_in=n_in, lat_blocks=lat_blocks, has_ctx=has_ctx),
        grid=(n_rows // bm,),
        in_specs=in_specs,
        out_specs=pl.BlockSpec((bm, D_MODEL), lambda i: (i, 0)),
        out_shape=jax.ShapeDtypeStruct(xa.shape, F32),
        input_output_aliases={len(args) - 2: 0},
        compiler_params=_params("arbitrary"),
        name="projection_residual",
    )(*args)


def _ffn_up_kernel(xm_ref, xp_ref, xn_ref, sh_ref, sc_ref, g_ref, w_ref, cw_ref, cb_ref, o_ref, h_ref,
                   ua_ref, ub_ref, *, lat_blocks, seq, ctx_len):
    bm = ROW_BLOCK
    i = pl.program_id(0)

    @pl.when(pl.program_id(1) == 0)
    def _():
        g, sc, sh = g_ref[...], sc_ref[...], sh_ref[...]
        h_ref[0:HALO, :] = _normmod(xp_ref[...], g, sc, sh).astype(BF16)
        h_ref[HALO:HALO + bm, :] = _normmod(xm_ref[...], g, sc, sh).astype(BF16)
        h_ref[HALO + bm:, :] = _normmod(xn_ref[...], g, sc, sh).astype(BF16)

    rows = lax.broadcasted_iota(jnp.int32, (bm, 1), 0) + i * bm
    period = jnp.where(i < lat_blocks, seq, ctx_len)
    first = (rows & (period - 1)) == 0
    last = ((rows + 1) & (period - 1)) == 0

    t = FFN_TILE
    n_pairs = o_ref.shape[1] // t
    bufs = (ua_ref, ub_ref)

    def matmul(k):
        bufs[k % 2][...] = _dot(h_ref[...], w_ref[:, 2 * t * k:2 * t * (k + 1)])

    matmul(0)
    for k in range(n_pairs):
        if k + 1 < n_pairs:
            matmul(k + 1)
        src = bufs[k % 2]
        cols = slice(2 * t * k, 2 * t * (k + 1))
        prev = jnp.where(first, 0.0, src[HALO - 1:HALO - 1 + bm, :])
        nxt = jnp.where(last, 0.0, src[HALO + 1:HALO + 1 + bm, :])
        c = (prev * cw_ref[0:1, cols] + src[HALO:HALO + bm, :] * cw_ref[1:2, cols]
             + nxt * cw_ref[2:3, cols] + cb_ref[:, cols])
        gate, val = c[:, :t], c[:, t:]
        o_ref[:, t * k:t * (k + 1)] = (gate * _sigmoid(gate) * val).astype(BF16)


def _pair_gate_val(w):
    lead = w.shape[:-1]
    w = w.reshape(*lead, 2, D_FF // FFN_TILE, FFN_TILE)
    return jnp.swapaxes(w, -3, -2).reshape(*lead, 2 * D_FF)


def _ffn_up(xa, mod_l, g, w_up, conv_w, conv_b, blocks_per_batch, n_blocks, lat_blocks, seq, ctx_len):
    rows = xa.shape[0]
    bm, bn = ROW_BLOCK, 512
    hb = bm // HALO
    last_halo = rows // HALO - 1
    assert seq & (seq - 1) == 0 and ctx_len & (ctx_len - 1) == 0
    return pl.pallas_call(
        functools.partial(_ffn_up_kernel, lat_blocks=lat_blocks, seq=seq, ctx_len=ctx_len),
        grid=(n_blocks, D_FF // bn),
        in_specs=[
            pl.BlockSpec((bm, D_MODEL), lambda i, j: (i, 0)),
            pl.BlockSpec((HALO, D_MODEL), lambda i, j: (jnp.maximum(i * hb - 1, 0), 0)),
            pl.BlockSpec((HALO, D_MODEL), lambda i, j: (jnp.minimum((i + 1) * hb, last_halo), 0)),
            _mod_spec(3, blocks_per_batch),
            _mod_spec(4, blocks_per_batch),
            pl.BlockSpec((1, D_MODEL), lambda i, j: (0, 0)),
            pl.BlockSpec((D_MODEL, 2 * bn), lambda i, j: (0, j)),
            pl.BlockSpec((CONV_W, 2 * bn), lambda i, j: (0, j)),
            pl.BlockSpec((1, 2 * bn), lambda i, j: (0, j)),
        ],
        out_specs=pl.BlockSpec((bm, bn), lambda i, j: (i, j)),
        out_shape=jax.ShapeDtypeStruct((n_blocks * bm, D_FF), BF16),
        scratch_shapes=[pltpu.VMEM((bm + 2 * HALO, D_MODEL), BF16),
                        pltpu.VMEM((bm + 2 * HALO, 2 * FFN_TILE), F32),
                        pltpu.VMEM((bm + 2 * HALO, 2 * FFN_TILE), F32)],
        compiler_params=_params("arbitrary", "arbitrary"),
        name="ffn_up_conv_gate",
    )(xa, xa, xa, mod_l, mod_l, g, w_up, conv_w, conv_b)


def _final_norm_kernel(x_ref, g_ref, o_ref):
    o_ref[...] = _rms(x_ref[...], g_ref[...])


def _final_norm(xa, g, rows):
    bm = ROW_BLOCK
    return pl.pallas_call(
        _final_norm_kernel,
        grid=(rows // bm,),
        in_specs=[pl.BlockSpec((bm, D_MODEL), lambda i: (i, 0)), pl.BlockSpec((1, D_MODEL), lambda i: (0, 0))],
        out_specs=pl.BlockSpec((bm, D_MODEL), lambda i: (i, 0)),
        out_shape=jax.ShapeDtypeStruct((rows, D_MODEL), F32),
        compiler_params=_params("arbitrary"),
        name="final_norm",
    )(xa, g)


def _rope_table(seq, pad_rows):
    n = MLA_ROPE // 4
    freqs = ROPE_BASE ** (-jnp.arange(n, dtype=F32) / n)
    pos = jnp.arange(seq)
    ang_r = (pos // GRID_W).astype(F32)[:, None] * freqs
    ang_c = (pos % GRID_W).astype(F32)[:, None] * freqs
    cos = jnp.concatenate([jnp.cos(ang_r)] * 2 + [jnp.cos(ang_c)] * 2, axis=1)
    sin = jnp.concatenate([-jnp.sin(ang_r), jnp.sin(ang_r), -jnp.sin(ang_c), jnp.sin(ang_c)], axis=1)
    lat = jnp.concatenate([cos, sin], axis=1)
    ident = jnp.concatenate([jnp.ones((pad_rows, MLA_ROPE), F32), jnp.zeros((pad_rows, MLA_ROPE), F32)], axis=1)
    return jnp.concatenate([lat, ident], axis=0)


def _partner_perm():
    q = MLA_ROPE // 4
    return np.concatenate([np.arange(q, 2 * q), np.arange(0, q), np.arange(3 * q, 4 * q), np.arange(2 * q, 3 * q)])


def _layout_w_in(w):
    cq, ckv, kr, qn, kn, vn, f = jnp.split(w, np.cumsum(
        (MLA_Q_RANK, MLA_KV_RANK, MLA_ROPE, NA_WIDTH, NA_WIDTH, NA_WIDTH))[:].tolist(), axis=1)
    return jnp.concatenate([cq, ckv, qn, kn, vn, f, kr, kr[:, _partner_perm()]], axis=1).astype(BF16)


def _layout_w_uq(w):
    w = w.reshape(MLA_Q_RANK, MLA_HEADS, MLA_NOPE + MLA_ROPE)
    rope = w[:, :, MLA_NOPE:]
    return jnp.concatenate([w, rope[:, :, _partner_perm()]], axis=2).reshape(MLA_Q_RANK, -1).astype(BF16)


def _layout_w_ukv(w):
    w = w.reshape(MLA_KV_RANK, MLA_HEADS, MLA_NOPE + MLA_V)
    return jnp.concatenate([w[:, :, :MLA_NOPE].reshape(MLA_KV_RANK, -1),
                            w[:, :, MLA_NOPE:].reshape(MLA_KV_RANK, -1)], axis=1).astype(BF16)


def kernel(x, c, ctx, c_ctx, w_mod, b_mod, g_attn, g_ffn, w_in, g_q, w_uq, g_kv, w_ukv, na_rpb, w_fnet, w_out,
           w_up, conv_w, conv_b, w_down, g_final):
    batch, seq, d = x.shape
    ctx_len = ctx.shape[1]
    depth = w_mod.shape[0]
    n_lat, n_ctx = batch * seq, batch * ctx_len
    bm = ROW_BLOCK
    lat_blocks = n_lat // bm
    all_blocks = (n_lat + n_ctx) // bm
    blocks_per_batch = seq // bm
    assert n_ctx % bm == 0 and n_lat // bm // blocks_per_batch == batch and batch < 8

    xa = jnp.concatenate([x.reshape(n_lat, d), ctx.reshape(n_ctx, d)], axis=0)
    cin = jnp.zeros((8, d), F32).at[:batch].set(c).at[batch].set(c_ctx)
    mod = _modulation(cin, w_mod, b_mod)
    cs_tab = _rope_table(seq, bm)
    row = lambda v: v.reshape(1, -1)

    for l in range(depth):
        ctx_out = l < depth - 1
        mod_l = mod[l].reshape(8, 1, 6 * d)
        oc, on, of, okr = _inproj(xa, mod_l, row(g_attn[l]), _layout_w_in(w_in[l]), blocks_per_batch)
        qm, km, vm = _mla_up(oc, okr, cs_tab, row(g_q[l]), row(g_kv[l]), _layout_w_uq(w_uq[l]),
                             _layout_w_ukv(w_ukv[l]), lat_blocks, blocks_per_batch)

        o_mla = _attention(qm, km, vm, batch=batch, heads=MLA_HEADS, dq=MLA_QK_PAD, dv=MLA_V, bq=1024, ck=1024,
                           seq=seq, ctx_len=ctx_len, k_col0=0, v_col0=0, name="mla_attention")
        o_na = _neighborhood_attention(on, _na_table(na_rpb[l], seq // GRID_W), batch, seq, ctx_len)
        o_fn = _fourier_latent(of, w_fnet[l].astype(BF16), batch, seq)
        w_o = w_out[l].astype(BF16)
        w_os = (w_o[:MLA_WIDTH], w_o[MLA_WIDTH:MLA_WIDTH + NA_WIDTH], w_o[MLA_WIDTH + NA_WIDTH:])

        ctx_ins = None
        if ctx_out:
            o_mla_c = _context_attention(
                qm, km, vm, batch=batch, heads=MLA_HEADS, dq=MLA_QK_PAD, dv=MLA_V, row0=n_lat, ctx_len=ctx_len,
                k_col0=0, v_col0=0, name="mla_attention_context")
            o_na_c = _context_attention(
                on, on, on, batch=batch, heads=NA_HEADS, dq=NA_HEAD_DIM, dv=NA_HEAD_DIM, row0=n_lat,
                ctx_len=ctx_len, k_col0=NA_HEADS, v_col0=2 * NA_HEADS, name="na_attention_context")
            o_fn_c = _fourier_context(of, w_fnet[l].astype(BF16), batch, n_lat, ctx_len)
            ctx_ins = (o_mla_c, o_na_c, o_fn_c)

        n_blocks = all_blocks if ctx_out else lat_blocks
        n_rows = n_blocks * bm
        xa = _proj_residual(xa, (o_mla, o_na, o_fn), ctx_ins, w_os, mod_l, 2, seq, n_rows, bm)
        a = _ffn_up(xa, mod_l, row(g_ffn[l]), _pair_gate_val(w_up[l].astype(BF16)), _pair_gate_val(conv_w[l]),
                    _pair_gate_val(row(conv_b[l])), blocks_per_batch, n_blocks, lat_blocks, seq, ctx_len)
        xa = _proj_residual(xa, (a,), None, (w_down[l].astype(BF16),), mod_l, 5, seq, n_rows, bm)

    out = _final_norm(xa, row(g_final), n_lat)
    return out.reshape(batch, seq, d)
```

```python
import functools

import numpy as np
import jax
import jax.numpy as jnp
from jax import lax
from jax.experimental import pallas as pl
from jax.experimental.pallas import tpu as pltpu

F32 = jnp.float32
BF16 = jnp.bfloat16

D_MODEL = 2048
GRID_W = 64
EPS = 1e-6

MLA_HEADS = 8
MLA_NOPE = 128
MLA_ROPE = 64
MLA_V = 128
MLA_Q_RANK = 512
MLA_KV_RANK = 512
MLA_QK_PAD = 256
MLA_SCALE = (MLA_NOPE + MLA_ROPE) ** -0.5
ROPE_BASE = 10000.0

NA_HEADS = 4
NA_HEAD_DIM = 128
NA_KH = 8
NA_KW = 16
NA_SCALE = NA_HEAD_DIM ** -0.5
NA_QROWS = 8
NA_KROWS = NA_QROWS + NA_KH - 1

FN_GROUPS = 4
FN_CH = 128
FN_WIDTH = FN_GROUPS * FN_CH
FN_K2 = 16

MLA_WIDTH = MLA_HEADS * MLA_V
NA_WIDTH = NA_HEADS * NA_HEAD_DIM
D_FF = 5632
FFN_TILE = 128
CONV_W = 3

ROW_BLOCK = 512
HALO = 16
PROJ_COLS = 512
MASK_VALUE = -1e30
LOG2E = 1.4426950408889634
VMEM_LIMIT = 56 * 1024 * 1024


def _params(*sem, flags=None):
    return pltpu.CompilerParams(dimension_semantics=sem, vmem_limit_bytes=VMEM_LIMIT, flags=flags)


def _dot(a, b):
    return jnp.dot(a, b, preferred_element_type=F32)


def _dot_nt(a, b):
    return lax.dot_general(a, b, (((1,), (1,)), ((), ())), preferred_element_type=F32)


def _split(x):
    hi = x.astype(BF16)
    lo = (x - hi.astype(F32)).astype(BF16)
    return hi, lo


def _dot3_left(m_hi, m_lo, x):
    x_hi, x_lo = _split(x)
    return _dot(m_hi, x_hi) + (_dot(m_hi, x_lo) + _dot(m_lo, x_hi))


def _dot3_right(x, m_hi, m_lo):
    x_hi, x_lo = _split(x)
    return _dot(x_hi, m_hi) + (_dot(x_lo, m_hi) + _dot(x_hi, m_lo))


def _sigmoid(x):
    return 1.0 / (1.0 + jnp.exp(-x))


def _rms(x, g):
    y = x * lax.rsqrt(jnp.mean(x * x, axis=-1, keepdims=True) + EPS)
    return y * g


def _normmod(x, g, sc, sh):
    return _rms(x, g) * (1.0 + sc) + sh


def _mod_kernel(c_ref, w_ref, b_ref, o_ref):
    c = c_ref[...]
    s = (c * _sigmoid(c)).astype(BF16)
    o_ref[...] = _dot(s, w_ref[...].astype(BF16)) + b_ref[...]


def _modulation(cin, w_mod, b_mod):
    depth, d, n = w_mod.shape
    bn = 1024
    return pl.pallas_call(
        _mod_kernel,
        grid=(depth, n // bn),
        in_specs=[
            pl.BlockSpec((8, d), lambda l, j: (0, 0)),
            pl.BlockSpec((None, d, bn), lambda l, j: (l, 0, j)),
            pl.BlockSpec((None, 1, bn), lambda l, j: (l, 0, j)),
        ],
        out_specs=pl.BlockSpec((None, 8, bn), lambda l, j: (l, 0, j)),
        out_shape=jax.ShapeDtypeStruct((depth, 8, n), F32),
        compiler_params=_params("arbitrary", "arbitrary"),
        name="modulation",
    )(cin, w_mod, b_mod.reshape(depth, 1, n))


def _inproj_kernel(x_ref, sh_ref, sc_ref, g_ref, w_ref, oc_ref, on_ref, of_ref, okr_ref):
    h = _normmod(x_ref[...], g_ref[...], sc_ref[...], sh_ref[...]).astype(BF16)
    oc_ref[...] = _dot(h, w_ref[:, 0:1024])
    on_ref[:, 0:NA_WIDTH] = (_dot(h, w_ref[:, 1024:1024 + NA_WIDTH]) * (NA_SCALE * LOG2E)).astype(BF16)
    on_ref[:, NA_WIDTH:] = _dot(h, w_ref[:, 1024 + NA_WIDTH:2560]).astype(BF16)
    of_ref[...] = _dot(h, w_ref[:, 2560:3072])
    okr_ref[...] = _dot(h, w_ref[:, 3072:3200])


def _mod_spec(k, blocks_per_batch):
    return pl.BlockSpec((None, 1, D_MODEL), lambda i, *_: (i // blocks_per_batch, 0, k))


def _inproj(xa, mod_l, g, w_p, blocks_per_batch):
    rows = xa.shape[0]
    bm = ROW_BLOCK
    row = lambda w: pl.BlockSpec((bm, w), lambda i: (i, 0))
    return pl.pallas_call(
        _inproj_kernel,
        grid=(rows // bm,),
        in_specs=[
            row(D_MODEL),
            _mod_spec(0, blocks_per_batch),
            _mod_spec(1, blocks_per_batch),
            pl.BlockSpec((1, D_MODEL), lambda i: (0, 0)),
            pl.BlockSpec(w_p.shape, lambda i: (0, 0), pipeline_mode=pl.Buffered(1)),
        ],
        out_specs=[row(1024), row(1536), row(512), row(128)],
        out_shape=[
            jax.ShapeDtypeStruct((rows, 1024), F32),
            jax.ShapeDtypeStruct((rows, 1536), BF16),
            jax.ShapeDtypeStruct((rows, 512), F32),
            jax.ShapeDtypeStruct((rows, 128), F32),
        ],
        compiler_params=_params("arbitrary"),
        name="in_projection",
    )(xa, mod_l, mod_l, g, w_p)


def _rope(t, cs):
    t = t * cs
    return t + pltpu.roll(t, 64, axis=1)


def _mla_up_kernel(c_ref, kr_ref, cs_ref, gq_ref, gkv_ref, wq_ref, wkv_ref, q_ref, k_ref, v_ref):
    cs = cs_ref[...]
    cq = _rms(c_ref[:, 0:MLA_Q_RANK], gq_ref[...]).astype(BF16)
    ckv = _rms(c_ref[:, MLA_Q_RANK:MLA_Q_RANK + MLA_KV_RANK], gkv_ref[...]).astype(BF16)
    q = _dot(cq, wq_ref[...]) * (MLA_SCALE * LOG2E)
    kv = _dot(ckv, wkv_ref[...])
    lane = lax.broadcasted_iota(jnp.int32, cs.shape, 1)
    k_rope = jnp.where(lane < MLA_ROPE, _rope(kr_ref[...], cs), 0.0).astype(BF16)
    for h in range(MLA_HEADS):
        o = h * MLA_QK_PAD
        q_ref[:, o:o + MLA_NOPE] = q[:, o:o + MLA_NOPE].astype(BF16)
        q_ref[:, o + MLA_NOPE:o + MLA_QK_PAD] = _rope(q[:, o + MLA_NOPE:o + MLA_QK_PAD], cs).astype(BF16)
        k_ref[:, o:o + MLA_NOPE] = kv[:, h * MLA_NOPE:(h + 1) * MLA_NOPE].astype(BF16)
        k_ref[:, o + MLA_NOPE:o + MLA_QK_PAD] = k_rope
    v_ref[...] = kv[:, MLA_HEADS * MLA_NOPE:].astype(BF16)


def _mla_up(oc, okr, cs_tab, g_q, g_kv, wq_p, wkv_p, lat_blocks, pos_blocks):
    rows = oc.shape[0]
    bm = ROW_BLOCK
    row = lambda w: pl.BlockSpec((bm, w), lambda i: (i, 0))
    const = lambda a: pl.BlockSpec(a.shape, lambda i: (0, 0))
    cs_spec = pl.BlockSpec((bm, 128), lambda i: (jnp.where(i < lat_blocks, i % pos_blocks, pos_blocks), 0))
    qk_w = MLA_HEADS * MLA_QK_PAD
    return pl.pallas_call(
        _mla_up_kernel,
        grid=(rows // bm,),
        in_specs=[row(1024), row(128), cs_spec, const(g_q), const(g_kv), const(wq_p), const(wkv_p)],
        out_specs=[row(qk_w), row(qk_w), row(MLA_WIDTH)],
        out_shape=[
            jax.ShapeDtypeStruct((rows, qk_w), BF16),
            jax.ShapeDtypeStruct((rows, qk_w), BF16),
            jax.ShapeDtypeStruct((rows, MLA_WIDTH), BF16),
        ],
        compiler_params=_params("arbitrary"),
        name="mla_up_projection",
    )(oc, okr, cs_tab, g_q, g_kv, wq_p, wkv_p)


def _softmax_first(s, v):
    m = jnp.max(s, axis=-1, keepdims=True)
    p = jnp.exp2(s - m)
    return m, jnp.sum(p, axis=-1, keepdims=True), _dot(p.astype(BF16), v)


def _softmax_next(s, s_max, v, m, l, acc):
    m_new = jnp.maximum(m, s_max)
    p = jnp.exp2(s - m_new)
    alpha = jnp.exp2(m - m_new)
    l = alpha * l + jnp.sum(p, axis=-1, keepdims=True)
    acc = alpha * acc + _dot(p.astype(BF16), v)
    return m_new, l, acc


def _attention_kernel(q_ref, k_ref, v_ref, kc_ref, vc_ref, o_ref, sa_ref, sb_ref, *, ck, n_main):
    q = q_ref[...]
    carry = _softmax_first(_dot_nt(q, kc_ref[...]), vc_ref[...])
    bufs = (sa_ref, sb_ref)

    def scores(c):
        s = _dot_nt(q, k_ref[c * ck:(c + 1) * ck, :])
        bufs[c % 2][...] = s
        return jnp.max(s, axis=-1, keepdims=True)

    s_max = scores(0)
    for c in range(n_main):
        nxt_max = scores(c + 1) if c + 1 < n_main else None
        carry = _softmax_next(bufs[c % 2][...], s_max, v_ref[c * ck:(c + 1) * ck, :], *carry)
        s_max = nxt_max
    _, l, acc = carry
    o_ref[...] = (acc / l).astype(o_ref.dtype)


def _attention(q_arr, k_arr, v_arr, *, batch, heads, dq, dv, bq, seq, ctx_len, k_col0, v_col0, name, ck=512):
    nq = seq // bq
    ctx0 = batch * seq // ctx_len
    return pl.pallas_call(
        functools.partial(_attention_kernel, ck=ck, n_main=seq // ck),
        grid=(batch, heads, nq),
        in_specs=[
            pl.BlockSpec((bq, dq), lambda b, h, i: (b * nq + i, h)),
            pl.BlockSpec((seq, dq), lambda b, h, i: (b, k_col0 + h)),
            pl.BlockSpec((seq, dv), lambda b, h, i: (b, v_col0 + h)),
            pl.BlockSpec((ctx_len, dq), lambda b, h, i: (ctx0 + b, k_col0 + h)),
            pl.BlockSpec((ctx_len, dv), lambda b, h, i: (ctx0 + b, v_col0 + h)),
        ],
        out_specs=pl.BlockSpec((bq, dv), lambda b, h, i: (b * nq + i, h)),
        out_shape=jax.ShapeDtypeStruct((batch * seq, heads * dv), BF16),
        scratch_shapes=[pltpu.VMEM((bq, ck), F32), pltpu.VMEM((bq, ck), F32)],
        compiler_params=_params("arbitrary", "arbitrary", "arbitrary"),
        name=name,
    )(q_arr, k_arr, v_arr, k_arr, v_arr)


def _context_attention_kernel(q_ref, k_ref, v_ref, o_ref):
    _, l, acc = _softmax_first(_dot_nt(q_ref[...], k_ref[...]), v_ref[...])
    o_ref[...] = (acc / l).astype(o_ref.dtype)


def _context_attention(q_arr, k_arr, v_arr, *, batch, heads, dq, dv, row0, ctx_len, k_col0, v_col0, name):
    blk0 = row0 // ctx_len
    return pl.pallas_call(
        _context_attention_kernel,
        grid=(batch, heads),
        in_specs=[
            pl.BlockSpec((ctx_len, dq), lambda b, h: (blk0 + b, h)),
            pl.BlockSpec((ctx_len, dq), lambda b, h: (blk0 + b, k_col0 + h)),
            pl.BlockSpec((ctx_len, dv), lambda b, h: (blk0 + b, v_col0 + h)),
        ],
        out_specs=pl.BlockSpec((ctx_len, dv), lambda b, h: (b, h)),
        out_shape=jax.ShapeDtypeStruct((batch * ctx_len, heads * dv), BF16),
        compiler_params=_params("arbitrary", "arbitrary"),
        name=name,
    )(q_arr, k_arr, v_arr)


def _na_kernel(q_ref, k_ref, v_ref, kc_ref, vc_ref, t_ref, o_ref, *, grid_rows):
    r = pl.program_id(2)
    ks = jnp.clip(r * NA_QROWS - NA_KH // 2, 0, grid_rows - NA_KROWS)
    start = pl.multiple_of(ks * GRID_W, GRID_W)
    nwin = NA_KROWS * GRID_W
    q = q_ref[...]
    s_win = _dot_nt(q, k_ref[pl.ds(start, nwin), :]) + t_ref[...]
    s_ctx = _dot_nt(q, kc_ref[...])
    m = jnp.maximum(jnp.max(s_win, axis=-1, keepdims=True), jnp.max(s_ctx, axis=-1, keepdims=True))
    p_win = jnp.exp2(s_win - m)
    p_ctx = jnp.exp2(s_ctx - m)
    l = jnp.sum(p_win, axis=-1, keepdims=True) + jnp.sum(p_ctx, axis=-1, keepdims=True)
    acc = _dot(p_win.astype(BF16), v_ref[pl.ds(start, nwin), :]) + _dot(p_ctx.astype(BF16), vc_ref[...])
    o_ref[...] = (acc / l).astype(o_ref.dtype)


def _na_table(rpb, grid_rows):
    tabs = []
    for r0 in (0, NA_QROWS, grid_rows - NA_QROWS):
        ks = min(max(r0 - NA_KH // 2, 0), grid_rows - NA_KROWS)
        r = r0 + np.arange(NA_QROWS)[:, None, None, None]
        w = np.arange(GRID_W)[None, :, None, None]
        kr = ks + np.arange(NA_KROWS)[None, None, :, None]
        j = np.arange(GRID_W)[None, None, None, :]
        rs = np.clip(r - NA_KH // 2, 0, grid_rows - NA_KH)
        cs = np.clip(w - NA_KW // 2, 0, GRID_W - NA_KW)
        valid = (kr >= rs) & (kr < rs + NA_KH) & (j >= cs) & (j < cs + NA_KW)
        sel_r = ((kr - r + (NA_KH - 1))[:, 0, :, 0, None] == np.arange(2 * NA_KH - 1)).astype(np.float32)
        sel_c = ((j - w + (NA_KW - 1))[0, :, 0, :, None] == np.arange(2 * NA_KW - 1)).astype(np.float32)
        bias = jnp.einsum("rap,hpq,wjq->hrwaj", sel_r, rpb.astype(F32), sel_c,
                          precision=lax.Precision.HIGHEST)
        tab = jnp.where(np.broadcast_to(valid, bias.shape[1:])[None], bias * LOG2E, MASK_VALUE)
        tabs.append(tab.reshape(rpb.shape[0], NA_QROWS * GRID_W, NA_KROWS * GRID_W))
    return jnp.stack(tabs)


def _neighborhood_attention(on, table, batch, seq, ctx_len):
    grid_rows = seq // GRID_W
    nblk = grid_rows // NA_QROWS
    bq = NA_QROWS * GRID_W
    d = NA_HEAD_DIM
    ctx0 = batch * seq // ctx_len
    return pl.pallas_call(
        functools.partial(_na_kernel, grid_rows=grid_rows),
        grid=(batch, NA_HEADS, nblk),
        in_specs=[
            pl.BlockSpec((bq, d), lambda b, h, r: (b * nblk + r, h)),
            pl.BlockSpec((seq, d), lambda b, h, r: (b, NA_HEADS + h)),
            pl.BlockSpec((seq, d), lambda b, h, r: (b, 2 * NA_HEADS + h)),
            pl.BlockSpec((ctx_len, d), lambda b, h, r: (ctx0 + b, NA_HEADS + h)),
            pl.BlockSpec((ctx_len, d), lambda b, h, r: (ctx0 + b, 2 * NA_HEADS + h)),
            pl.BlockSpec((None, None, bq, NA_KROWS * GRID_W),
                         lambda b, h, r: (jnp.where(r == 0, 0, jnp.where(r == nblk - 1, 2, 1)), h, 0, 0)),
        ],
        out_specs=pl.BlockSpec((bq, d), lambda b, h, r: (b * nblk + r, h)),
        out_shape=jax.ShapeDtypeStruct((batch * seq, NA_WIDTH), BF16),
        compiler_params=_params("arbitrary", "arbitrary", "arbitrary"),
        name="neighborhood_attention",
    )(on, on, on, on, on, table)


def _dft_consts(n):
    jk = (np.arange(n)[:, None] * np.arange(n)[None, :]) % n
    ang = 2.0 * np.pi * jk / n
    return np.cos(ang), np.sin(ang)


def _hi_lo(m):
    m = jnp.asarray(m, F32)
    hi = m.astype(BF16)
    return hi, (m - hi.astype(F32)).astype(BF16)


def _fn_stage1_kernel(x_ref, fh_ref, fl_ref, tc_ref, ts_ref, o_ref):
    n = GRID_W
    b = _dot3_left(fh_ref[...], fl_ref[...], x_ref[...])
    br, bi = b[:n], b[n:]
    tc, ts = tc_ref[...], ts_ref[...]
    o_ref[0] = br * tc + bi * ts
    o_ref[1] = bi * tc - br * ts


def _fn_stage2_kernel(t_ref, gh_ref, gl_ref, ch_ref, cl_ref, w_ref, p_ref, o_ref):
    n = GRID_W
    outs = []
    for j in range(FN_K2):
        t = jnp.concatenate([t_ref[0, j], t_ref[1, j]], axis=0)
        a = _dot3_left(gh_ref[...], gl_ref[...], t)
        outs.append(a)
    ar = jnp.concatenate([a[:n] for a in outs], axis=0)
    ai = jnp.concatenate([a[n:] for a in outs], axis=0)
    ys = []
    for g in range(FN_GROUPS):
        sl = slice(g * FN_CH, (g + 1) * FN_CH)
        z = jnp.concatenate([ar[:, sl], ai[:, sl]], axis=1)
        spec = _dot3_right(z, ch_ref[...], cl_ref[...])
        ys.append(_dot(spec.astype(BF16), w_ref[g]).astype(BF16))
    y = jnp.concatenate(ys, axis=1)
    y = _dot(p_ref[...], y).astype(BF16)
    o_ref[...] = y.reshape(n, FN_K2, FN_WIDTH)


def _fourier_latent(of, w_fnet, batch, seq):
    n = GRID_W
    assert seq == n * n
    cols = n * FN_WIDTH
    c64, s64 = _dft_consts(n)
    f1h, f1l = _hi_lo(np.concatenate([c64, -s64], axis=0))
    tw = 2.0 * np.pi * (np.arange(n)[:, None] * np.arange(n)[None, :]) / (n * n)
    tc = jnp.asarray(np.repeat(np.cos(tw).reshape(n, n, 1), FN_WIDTH, axis=2).reshape(n, cols), F32)
    ts = jnp.asarray(np.repeat(np.sin(tw).reshape(n, n, 1), FN_WIDTH, axis=2).reshape(n, cols), F32)
    x2 = of[:batch * seq].reshape(batch, n, cols)
    cb = 4096
    t = pl.pallas_call(
        _fn_stage1_kernel,
        grid=(batch, cols // cb),
        in_specs=[
            pl.BlockSpec((None, n, cb), lambda b, j: (b, 0, j)),
            pl.BlockSpec(f1h.shape, lambda b, j: (0, 0)),
            pl.BlockSpec(f1l.shape, lambda b, j: (0, 0)),
            pl.BlockSpec((n, cb), lambda b, j: (0, j)),
            pl.BlockSpec((n, cb), lambda b, j: (0, j)),
        ],
        out_specs=pl.BlockSpec((None, 2, n, cb), lambda b, j: (b, 0, 0, j)),
        out_shape=jax.ShapeDtypeStruct((batch, 2, n, cols), F32),
        compiler_params=_params("arbitrary", "arbitrary"),
        name="fourier_rows",
    )(x2, f1h, f1l, tc, ts)
    t = t.reshape(batch, 2, n, n, FN_WIDTH)

    g2h, g2l = _hi_lo(np.block([[c64, s64], [-s64, c64]]))
    cc, sc = _dft_consts(FN_CH)
    norm = 1.0 / np.sqrt(seq * FN_CH)
    c4h, c4l = _hi_lo(np.concatenate([cc, sc], axis=0) * norm)
    rows = FN_K2 * n
    perm = np.zeros((rows, rows), np.float32)
    k1 = np.arange(n)[:, None]
    j = np.arange(FN_K2)[None, :]
    perm[(k1 * FN_K2 + j).ravel(), (j * n + k1).ravel()] = 1.0
    perm = jnp.asarray(perm, BF16)
    const = lambda a: pl.BlockSpec(a.shape, lambda b, i: (0,) * a.ndim)
    y = pl.pallas_call(
        _fn_stage2_kernel,
        grid=(batch, n // FN_K2),
        in_specs=[
            pl.BlockSpec((None, 2, FN_K2, n, FN_WIDTH), lambda b, i: (b, 0, i, 0, 0)),
            const(g2h), const(g2l), const(c4h), const(c4l), const(w_fnet), const(perm),
        ],
        out_specs=pl.BlockSpec((None, n, FN_K2, FN_WIDTH), lambda b, i: (b, 0, i, 0)),
        out_shape=jax.ShapeDtypeStruct((batch, n, n, FN_WIDTH), BF16),
        compiler_params=_params("arbitrary", "arbitrary"),
        name="fourier_cols_channels",
    )(t, g2h, g2l, c4h, c4l, w_fnet, perm)
    return y.reshape(batch * seq, FN_WIDTH)


def _fn_ctx_kernel(x_ref, fh_ref, fl_ref, ch_ref, cl_ref, w_ref, o_ref, *, n):
    a = _dot3_left(fh_ref[...], fl_ref[...], x_ref[...])
    ar, ai = a[:n], a[n:]
    ys = []
    for g in range(FN_GROUPS):
        sl = slice(g * FN_CH, (g + 1) * FN_CH)
        z = jnp.concatenate([ar[:, sl], ai[:, sl]], axis=1)
        spec = _dot3_right(z, ch_ref[...], cl_ref[...])
        ys.append(_dot(spec.astype(BF16), w_ref[g]).astype(BF16))
    o_ref[...] = jnp.concatenate(ys, axis=1)


def _fourier_context(of, w_fnet, batch, row0, n):
    c, s = _dft_consts(n)
    fh, fl = _hi_lo(np.concatenate([c, -s], axis=0))
    cc, sc = _dft_consts(FN_CH)
    c4h, c4l = _hi_lo(np.concatenate([cc, sc], axis=0) / np.sqrt(n * FN_CH))
    const = lambda a: pl.BlockSpec(a.shape, lambda b: (0,) * a.ndim)
    blk0 = row0 // n
    return pl.pallas_call(
        functools.partial(_fn_ctx_kernel, n=n),
        grid=(batch,),
        in_specs=[pl.BlockSpec((n, FN_WIDTH), lambda b: (blk0 + b, 0)),
                  const(fh), const(fl), const(c4h), const(c4l), const(w_fnet)],
        out_specs=pl.BlockSpec((n, FN_WIDTH), lambda b: (b, 0)),
        out_shape=jax.ShapeDtypeStruct((batch * n, FN_WIDTH), BF16),
        compiler_params=_params("arbitrary"),
        name="fourier_context",
    )(of, fh, fl, c4h, c4l, w_fnet)


def _proj_residual_kernel(*refs, n_in, lat_blocks, has_ctx):
    lat = refs[:n_in]
    ctx = refs[n_in:2 * n_in] if has_ctx else ()
    rest = refs[(2 if has_ctx else 1) * n_in:]
    w_refs, (x_ref, gt_ref, o_ref) = rest[:n_in], rest[n_in:]

    def run(a_refs):
        for n in range(0, D_MODEL, PROJ_COLS):
            cols = slice(n, n + PROJ_COLS)
            acc = None
            for a_ref, w_ref in zip(a_refs, w_refs):
                t = _dot(a_ref[...], w_ref[:, cols])
                acc = t if acc is None else acc + t
            o_ref[:, cols] = x_ref[:, cols] + gt_ref[:, cols] * acc

    if has_ctx:
        i = pl.program_id(0)
        pl.when(i < lat_blocks)(lambda: run(lat))
        pl.when(i >= lat_blocks)(lambda: run(ctx))
    else:
        run(lat)


def _proj_residual(xa, lat_ins, ctx_ins, ws, mod_l, gate_k, rows_per_batch, n_rows, bm):
    n_in = len(lat_ins)
    has_ctx = ctx_ins is not None
    lat_blocks = lat_ins[0].shape[0] // bm
    bpb = rows_per_batch // bm
    in_specs = [pl.BlockSpec((bm, a.shape[1]), lambda i: (jnp.minimum(i, lat_blocks - 1), 0)) for a in lat_ins]
    if has_ctx:
        in_specs += [pl.BlockSpec((bm, a.shape[1]), lambda i: (jnp.maximum(i - lat_blocks, 0), 0)) for a in ctx_ins]
    in_specs += [pl.BlockSpec(w.shape, lambda i: (0, 0), pipeline_mode=pl.Buffered(1)) for w in ws]
    in_specs += [
        pl.BlockSpec((bm, D_MODEL), lambda i: (i, 0)),
        pl.BlockSpec((None, 1, D_MODEL), lambda i: (i // bpb, 0, gate_k)),
    ]
    args = list(lat_ins) + (list(ctx_ins) if has_ctx else []) + list(ws) + [xa, mod_l]
    return pl.pallas_call(
        functools.partial(_proj_residual_kernel, n_in=n_in, lat_blocks=lat_blocks, has_ctx=has_ctx),
        grid=(n_rows // bm,),
        in_specs=in_specs,
        out_specs=pl.BlockSpec((bm, D_MODEL), lambda i: (i, 0)),
        out_shape=jax.ShapeDtypeStruct(xa.shape, F32),
        input_output_aliases={len(args) - 2: 0},
        compiler_params=_params("arbitrary"),
        name="projection_residual",
    )(*args)


def _ffn_up_kernel(xm_ref, xp_ref, xn_ref, sh_ref, sc_ref, g_ref, wg_ref, wv_ref, cwg_ref, cwv_ref, cbg_ref,
                   cbv_ref, o_ref, h_ref, *u_refs, lat_blocks, seq, ctx_len):
    bm = ROW_BLOCK
    i = pl.program_id(0)

    @pl.when(pl.program_id(1) == 0)
    def _():
        g, sc, sh = g_ref[...], sc_ref[...], sh_ref[...]
        h_ref[0:HALO, :] = _normmod(xp_ref[...], g, sc, sh).astype(BF16)
        h_ref[HALO:HALO + bm, :] = _normmod(xm_ref[...], g, sc, sh).astype(BF16)
        h_ref[HALO + bm:, :] = _normmod(xn_ref[...], g, sc, sh).astype(BF16)

    rows = lax.broadcasted_iota(jnp.int32, (bm, 1), 0) + i * bm
    period = jnp.where(i < lat_blocks, seq, ctx_len)
    first = (rows & (period - 1)) == 0
    last = ((rows + 1) & (period - 1)) == 0

    t = FFN_TILE
    n_tiles = o_ref.shape[1] // t

    def pair(g_ref, v_ref, rows, k):
        cols = slice(t * k, t * (k + 1))
        return jnp.concatenate([g_ref[rows, cols], v_ref[rows, cols]], axis=1)

    def matmul(k):
        u_refs[k % 2][...] = _dot(h_ref[...], pair(wg_ref, wv_ref, slice(None), k))

    matmul(0)
    for k in range(n_tiles):
        if k + 1 < n_tiles:
            matmul(k + 1)
        src = u_refs[k % 2]
        prev = jnp.where(first, 0.0, src[HALO - 1:HALO - 1 + bm, :])
        nxt = jnp.where(last, 0.0, src[HALO + 1:HALO + 1 + bm, :])
        c = (prev * pair(cwg_ref, cwv_ref, slice(0, 1), k) + src[HALO:HALO + bm, :] * pair(cwg_ref, cwv_ref, slice(1, 2), k)
             + nxt * pair(cwg_ref, cwv_ref, slice(2, 3), k) + pair(cbg_ref, cbv_ref, slice(None), k))
        gate, val = c[:, :t], c[:, t:]
        o_ref[:, t * k:t * (k + 1)] = (gate * _sigmoid(gate) * val).astype(BF16)


def _ffn_up(xa, mod_l, g, w_up, conv_w, conv_b, blocks_per_batch, n_blocks, lat_blocks, seq, ctx_len):
    rows = xa.shape[0]
    bm, bn = ROW_BLOCK, 512
    nj = D_FF // bn
    hb = bm // HALO
    last_halo = rows // HALO - 1
    assert seq & (seq - 1) == 0 and ctx_len & (ctx_len - 1) == 0
    return pl.pallas_call(
        functools.partial(_ffn_up_kernel, lat_blocks=lat_blocks, seq=seq, ctx_len=ctx_len),
        grid=(n_blocks, nj),
        in_specs=[
            pl.BlockSpec((bm, D_MODEL), lambda i, j: (i, 0)),
            pl.BlockSpec((HALO, D_MODEL), lambda i, j: (jnp.maximum(i * hb - 1, 0), 0)),
            pl.BlockSpec((HALO, D_MODEL), lambda i, j: (jnp.minimum((i + 1) * hb, last_halo), 0)),
            _mod_spec(3, blocks_per_batch),
            _mod_spec(4, blocks_per_batch),
            pl.BlockSpec((1, D_MODEL), lambda i, j: (0, 0)),
            pl.BlockSpec((D_MODEL, bn), lambda i, j: (0, j)),
            pl.BlockSpec((D_MODEL, bn), lambda i, j: (0, nj + j)),
            pl.BlockSpec((CONV_W, bn), lambda i, j: (0, j)),
            pl.BlockSpec((CONV_W, bn), lambda i, j: (0, nj + j)),
            pl.BlockSpec((1, bn), lambda i, j: (0, j)),
            pl.BlockSpec((1, bn), lambda i, j: (0, nj + j)),
        ],
        out_specs=pl.BlockSpec((bm, bn), lambda i, j: (i, j)),
        out_shape=jax.ShapeDtypeStruct((n_blocks * bm, D_FF), BF16),
        scratch_shapes=[pltpu.VMEM((bm + 2 * HALO, D_MODEL), BF16)]
        + [pltpu.VMEM((bm + 2 * HALO, 2 * FFN_TILE), F32)] * 2,
        compiler_params=_params("arbitrary", "arbitrary"),
        name="ffn_up_conv_gate",
    )(xa, xa, xa, mod_l, mod_l, g, w_up, w_up, conv_w, conv_w, conv_b, conv_b)


def _final_norm_kernel(x_ref, g_ref, o_ref):
    o_ref[...] = _rms(x_ref[...], g_ref[...])


def _final_norm(xa, g, rows):
    bm = ROW_BLOCK
    return pl.pallas_call(
        _final_norm_kernel,
        grid=(rows // bm,),
        in_specs=[pl.BlockSpec((bm, D_MODEL), lambda i: (i, 0)), pl.BlockSpec((1, D_MODEL), lambda i: (0, 0))],
        out_specs=pl.BlockSpec((bm, D_MODEL), lambda i: (i, 0)),
        out_shape=jax.ShapeDtypeStruct((rows, D_MODEL), F32),
        compiler_params=_params("arbitrary"),
        name="final_norm",
    )(xa, g)


def _rope_table(seq, pad_rows):
    n = MLA_ROPE // 4
    freqs = ROPE_BASE ** (-jnp.arange(n, dtype=F32) / n)
    pos = jnp.arange(seq)
    ang_r = (pos // GRID_W).astype(F32)[:, None] * freqs
    ang_c = (pos % GRID_W).astype(F32)[:, None] * freqs
    cos = jnp.concatenate([jnp.cos(ang_r)] * 2 + [jnp.cos(ang_c)] * 2, axis=1)
    sin = jnp.concatenate([-jnp.sin(ang_r), jnp.sin(ang_r), -jnp.sin(ang_c), jnp.sin(ang_c)], axis=1)
    lat = jnp.concatenate([cos, sin], axis=1)
    ident = jnp.concatenate([jnp.ones((pad_rows, MLA_ROPE), F32), jnp.zeros((pad_rows, MLA_ROPE), F32)], axis=1)
    return jnp.concatenate([lat, ident], axis=0)


def _partner_perm():
    q = MLA_ROPE // 4
    return np.concatenate([np.arange(q, 2 * q), np.arange(0, q), np.arange(3 * q, 4 * q), np.arange(2 * q, 3 * q)])


def _layout_w_in(w):
    cq, ckv, kr, qn, kn, vn, f = jnp.split(w, np.cumsum(
        (MLA_Q_RANK, MLA_KV_RANK, MLA_ROPE, NA_WIDTH, NA_WIDTH, NA_WIDTH))[:].tolist(), axis=1)
    return jnp.concatenate([cq, ckv, qn, kn, vn, f, kr, kr[:, _partner_perm()]], axis=1).astype(BF16)


def _layout_w_uq(w):
    w = w.reshape(MLA_Q_RANK, MLA_HEADS, MLA_NOPE + MLA_ROPE)
    rope = w[:, :, MLA_NOPE:]
    return jnp.concatenate([w, rope[:, :, _partner_perm()]], axis=2).reshape(MLA_Q_RANK, -1).astype(BF16)


def _layout_w_ukv(w):
    w = w.reshape(MLA_KV_RANK, MLA_HEADS, MLA_NOPE + MLA_V)
    return jnp.concatenate([w[:, :, :MLA_NOPE].reshape(MLA_KV_RANK, -1),
                            w[:, :, MLA_NOPE:].reshape(MLA_KV_RANK, -1)], axis=1).astype(BF16)


def kernel(x, c, ctx, c_ctx, w_mod, b_mod, g_attn, g_ffn, w_in, g_q, w_uq, g_kv, w_ukv, na_rpb, w_fnet, w_out,
           w_up, conv_w, conv_b, w_down, g_final):
    batch, seq, d = x.shape
    ctx_len = ctx.shape[1]
    depth = w_mod.shape[0]
    n_lat, n_ctx = batch * seq, batch * ctx_len
    bm = ROW_BLOCK
    lat_blocks = n_lat // bm
    all_blocks = (n_lat + n_ctx) // bm
    blocks_per_batch = seq // bm
    assert n_ctx % bm == 0 and n_lat // bm // blocks_per_batch == batch and batch < 8

    xa = jnp.concatenate([x.reshape(n_lat, d), ctx.reshape(n_ctx, d)], axis=0)
    cin = jnp.zeros((8, d), F32).at[:batch].set(c).at[batch].set(c_ctx)
    mod = _modulation(cin, w_mod, b_mod)
    cs_tab = _rope_table(seq, bm)
    row = lambda v: v.reshape(1, -1)

    for l in range(depth):
        ctx_out = l < depth - 1
        mod_l = mod[l].reshape(8, 1, 6 * d)
        oc, on, of, okr = _inproj(xa, mod_l, row(g_attn[l]), _layout_w_in(w_in[l]), blocks_per_batch)
        qm, km, vm = _mla_up(oc, okr, cs_tab, row(g_q[l]), row(g_kv[l]), _layout_w_uq(w_uq[l]),
                             _layout_w_ukv(w_ukv[l]), lat_blocks, blocks_per_batch)

        o_mla = _attention(qm, km, vm, batch=batch, heads=MLA_HEADS, dq=MLA_QK_PAD, dv=MLA_V, bq=1024, ck=1024,
                           seq=seq, ctx_len=ctx_len, k_col0=0, v_col0=0, name="mla_attention")
        o_na = _neighborhood_attention(on, _na_table(na_rpb[l], seq // GRID_W), batch, seq, ctx_len)
        o_fn = _fourier_latent(of, w_fnet[l].astype(BF16), batch, seq)
        w_o = w_out[l].astype(BF16)
        w_os = (w_o[:MLA_WIDTH], w_o[MLA_WIDTH:MLA_WIDTH + NA_WIDTH], w_o[MLA_WIDTH + NA_WIDTH:])

        ctx_ins = None
        if ctx_out:
            o_mla_c = _context_attention(
                qm, km, vm, batch=batch, heads=MLA_HEADS, dq=MLA_QK_PAD, dv=MLA_V, row0=n_lat, ctx_len=ctx_len,
                k_col0=0, v_col0=0, name="mla_attention_context")
            o_na_c = _context_attention(
                on, on, on, batch=batch, heads=NA_HEADS, dq=NA_HEAD_DIM, dv=NA_HEAD_DIM, row0=n_lat,
                ctx_len=ctx_len, k_col0=NA_HEADS, v_col0=2 * NA_HEADS, name="na_attention_context")
            o_fn_c = _fourier_context(of, w_fnet[l].astype(BF16), batch, n_lat, ctx_len)
            ctx_ins = (o_mla_c, o_na_c, o_fn_c)

        n_blocks = all_blocks if ctx_out else lat_blocks
        n_rows = n_blocks * bm
        xa = _proj_residual(xa, (o_mla, o_na, o_fn), ctx_ins, w_os, mod_l, 2, seq, n_rows, bm)
        a = _ffn_up(xa, mod_l, row(g_ffn[l]), w_up[l].astype(BF16), conv_w[l], row(conv_b[l]),
                    blocks_per_batch, n_blocks, lat_blocks, seq, ctx_len)
        xa = _proj_residual(xa, (a,), None, (w_down[l].astype(BF16),), mod_l, 5, seq, n_rows, bm)

    out = _final_norm(xa, row(g_final), n_lat)
    return out.reshape(batch, seq, d)
```

```python
import functools

import numpy as np
import jax
import jax.numpy as jnp
from jax import lax
from jax.experimental import pallas as pl
from jax.experimental.pallas import tpu as pltpu

F32 = jnp.float32
BF16 = jnp.bfloat16

D_MODEL = 2048
GRID_W = 64
EPS = 1e-6

MLA_HEADS = 8
MLA_NOPE = 128
MLA_ROPE = 64
MLA_V = 128
MLA_Q_RANK = 512
MLA_KV_RANK = 512
MLA_QK_PAD = 256
MLA_SCALE = (MLA_NOPE + MLA_ROPE) ** -0.5
ROPE_BASE = 10000.0

NA_HEADS = 4
NA_HEAD_DIM = 128
NA_KH = 8
NA_KW = 16
NA_SCALE = NA_HEAD_DIM ** -0.5
NA_QROWS = 8
NA_KROWS = NA_QROWS + NA_KH - 1

FN_GROUPS = 4
FN_CH = 128
FN_WIDTH = FN_GROUPS * FN_CH
FN_K2 = 16

MLA_WIDTH = MLA_HEADS * MLA_V
NA_WIDTH = NA_HEADS * NA_HEAD_DIM
D_FF = 5632
FFN_TILE = 128
CONV_W = 3

ROW_BLOCK = 512
FFN_ROWS = 1024
HALO = 16
PROJ_COLS = 512
MASK_VALUE = -1e30
LOG2E = 1.4426950408889634
VMEM_LIMIT = 56 * 1024 * 1024


def _params(*sem, flags=None):
    return pltpu.CompilerParams(dimension_semantics=sem, vmem_limit_bytes=VMEM_LIMIT, flags=flags)


def _dot(a, b):
    return jnp.dot(a, b, preferred_element_type=F32)


def _dot_nt(a, b):
    return lax.dot_general(a, b, (((1,), (1,)), ((), ())), preferred_element_type=F32)


def _split(x):
    hi = x.astype(BF16)
    lo = (x - hi.astype(F32)).astype(BF16)
    return hi, lo


def _dot3_left(m_hi, m_lo, x):
    x_hi, x_lo = _split(x)
    return _dot(m_hi, x_hi) + (_dot(m_hi, x_lo) + _dot(m_lo, x_hi))


def _dot3_right(x, m_hi, m_lo):
    x_hi, x_lo = _split(x)
    return _dot(x_hi, m_hi) + (_dot(x_lo, m_hi) + _dot(x_hi, m_lo))


def _sigmoid(x):
    return 1.0 / (1.0 + jnp.exp(-x))


def _rms(x, g):
    y = x * lax.rsqrt(jnp.mean(x * x, axis=-1, keepdims=True) + EPS)
    return y * g


def _normmod(x, g, sc, sh):
    return _rms(x, g) * (1.0 + sc) + sh


def _mod_kernel(c_ref, w_ref, b_ref, o_ref):
    c = c_ref[...]
    s = (c * _sigmoid(c)).astype(BF16)
    o_ref[...] = _dot(s, w_ref[...].astype(BF16)) + b_ref[...]


def _modulation(cin, w_mod, b_mod):
    depth, d, n = w_mod.shape
    bn = 1024
    return pl.pallas_call(
        _mod_kernel,
        grid=(depth, n // bn),
        in_specs=[
            pl.BlockSpec((8, d), lambda l, j: (0, 0)),
            pl.BlockSpec((None, d, bn), lambda l, j: (l, 0, j)),
            pl.BlockSpec((None, 1, bn), lambda l, j: (l, 0, j)),
        ],
        out_specs=pl.BlockSpec((None, 8, bn), lambda l, j: (l, 0, j)),
        out_shape=jax.ShapeDtypeStruct((depth, 8, n), F32),
        compiler_params=_params("arbitrary", "arbitrary"),
        name="modulation",
    )(cin, w_mod, b_mod.reshape(depth, 1, n))


def _inproj_kernel(x_ref, sh_ref, sc_ref, g_ref, w_ref, oc_ref, on_ref, of_ref, okr_ref):
    h = _normmod(x_ref[...], g_ref[...], sc_ref[...], sh_ref[...]).astype(BF16)
    oc_ref[...] = _dot(h, w_ref[:, 0:1024])
    on_ref[:, 0:NA_WIDTH] = (_dot(h, w_ref[:, 1024:1024 + NA_WIDTH]) * (NA_SCALE * LOG2E)).astype(BF16)
    on_ref[:, NA_WIDTH:] = _dot(h, w_ref[:, 1024 + NA_WIDTH:2560]).astype(BF16)
    of_ref[...] = _dot(h, w_ref[:, 2560:3072])
    okr_ref[...] = _dot(h, w_ref[:, 3072:3200])


def _mod_spec(k, blocks_per_batch):
    return pl.BlockSpec((None, 1, D_MODEL), lambda i, *_: (i // blocks_per_batch, 0, k))


def _inproj(xa, mod_l, g, w_p, blocks_per_batch):
    rows = xa.shape[0]
    bm = ROW_BLOCK
    row = lambda w: pl.BlockSpec((bm, w), lambda i: (i, 0))
    return pl.pallas_call(
        _inproj_kernel,
        grid=(rows // bm,),
        in_specs=[
            row(D_MODEL),
            _mod_spec(0, blocks_per_batch),
            _mod_spec(1, blocks_per_batch),
            pl.BlockSpec((1, D_MODEL), lambda i: (0, 0)),
            pl.BlockSpec(w_p.shape, lambda i: (0, 0), pipeline_mode=pl.Buffered(1)),
        ],
        out_specs=[row(1024), row(1536), row(512), row(128)],
        out_shape=[
            jax.ShapeDtypeStruct((rows, 1024), F32),
            jax.ShapeDtypeStruct((rows, 1536), BF16),
            jax.ShapeDtypeStruct((rows, 512), F32),
            jax.ShapeDtypeStruct((rows, 128), F32),
        ],
        compiler_params=_params("arbitrary"),
        name="in_projection",
    )(xa, mod_l, mod_l, g, w_p)


def _rope(t, cs):
    t = t * cs
    return t + pltpu.roll(t, 64, axis=1)


def _mla_up_kernel(c_ref, kr_ref, cs_ref, gq_ref, gkv_ref, wq_ref, wkv_ref, q_ref, k_ref, v_ref):
    cs = cs_ref[...]
    cq = _rms(c_ref[:, 0:MLA_Q_RANK], gq_ref[...]).astype(BF16)
    ckv = _rms(c_ref[:, MLA_Q_RANK:MLA_Q_RANK + MLA_KV_RANK], gkv_ref[...]).astype(BF16)
    q = _dot(cq, wq_ref[...]) * (MLA_SCALE * LOG2E)
    kv = _dot(ckv, wkv_ref[...])
    lane = lax.broadcasted_iota(jnp.int32, cs.shape, 1)
    k_rope = jnp.where(lane < MLA_ROPE, _rope(kr_ref[...], cs), 0.0).astype(BF16)
    for h in range(MLA_HEADS):
        o = h * MLA_QK_PAD
        q_ref[:, o:o + MLA_NOPE] = q[:, o:o + MLA_NOPE].astype(BF16)
        q_ref[:, o + MLA_NOPE:o + MLA_QK_PAD] = _rope(q[:, o + MLA_NOPE:o + MLA_QK_PAD], cs).astype(BF16)
        k_ref[:, o:o + MLA_NOPE] = kv[:, h * MLA_NOPE:(h + 1) * MLA_NOPE].astype(BF16)
        k_ref[:, o + MLA_NOPE:o + MLA_QK_PAD] = k_rope
    v_ref[...] = kv[:, MLA_HEADS * MLA_NOPE:].astype(BF16)


def _mla_up(oc, okr, cs_tab, g_q, g_kv, wq_p, wkv_p, lat_blocks, pos_blocks):
    rows = oc.shape[0]
    bm = ROW_BLOCK
    row = lambda w: pl.BlockSpec((bm, w), lambda i: (i, 0))
    const = lambda a: pl.BlockSpec(a.shape, lambda i: (0, 0))
    cs_spec = pl.BlockSpec((bm, 128), lambda i: (jnp.where(i < lat_blocks, i % pos_blocks, pos_blocks), 0))
    qk_w = MLA_HEADS * MLA_QK_PAD
    return pl.pallas_call(
        _mla_up_kernel,
        grid=(rows // bm,),
        in_specs=[row(1024), row(128), cs_spec, const(g_q), const(g_kv), const(wq_p), const(wkv_p)],
        out_specs=[row(qk_w), row(qk_w), row(MLA_WIDTH)],
        out_shape=[
            jax.ShapeDtypeStruct((rows, qk_w), BF16),
            jax.ShapeDtypeStruct((rows, qk_w), BF16),
            jax.ShapeDtypeStruct((rows, MLA_WIDTH), BF16),
        ],
        compiler_params=_params("arbitrary"),
        name="mla_up_projection",
    )(oc, okr, cs_tab, g_q, g_kv, wq_p, wkv_p)


def _softmax_first(s, v):
    m = jnp.max(s, axis=-1, keepdims=True)
    p = jnp.exp2(s - m)
    return m, jnp.sum(p, axis=-1, keepdims=True), _dot(p.astype(BF16), v)


def _softmax_next(s, s_max, v, m, l, acc):
    m_new = jnp.maximum(m, s_max)
    p = jnp.exp2(s - m_new)
    alpha = jnp.exp2(m - m_new)
    l = alpha * l + jnp.sum(p, axis=-1, keepdims=True)
    acc = alpha * acc + _dot(p.astype(BF16), v)
    return m_new, l, acc


def _attention_kernel(q_ref, k_ref, v_ref, kc_ref, vc_ref, o_ref, sa_ref, sb_ref, *, ck, n_main):
    q = q_ref[...]
    carry = _softmax_first(_dot_nt(q, kc_ref[...]), vc_ref[...])
    bufs = (sa_ref, sb_ref)

    def scores(c):
        s = _dot_nt(q, k_ref[c * ck:(c + 1) * ck, :])
        bufs[c % 2][...] = s
        return jnp.max(s, axis=-1, keepdims=True)

    s_max = scores(0)
    for c in range(n_main):
        nxt_max = scores(c + 1) if c + 1 < n_main else None
        carry = _softmax_next(bufs[c % 2][...], s_max, v_ref[c * ck:(c + 1) * ck, :], *carry)
        s_max = nxt_max
    _, l, acc = carry
    o_ref[...] = (acc / l).astype(o_ref.dtype)


def _attention(q_arr, k_arr, v_arr, *, batch, heads, dq, dv, bq, seq, ctx_len, k_col0, v_col0, name, ck=512):
    nq = seq // bq
    ctx0 = batch * seq // ctx_len
    return pl.pallas_call(
        functools.partial(_attention_kernel, ck=ck, n_main=seq // ck),
        grid=(batch, heads, nq),
        in_specs=[
            pl.BlockSpec((bq, dq), lambda b, h, i: (b * nq + i, h)),
            pl.BlockSpec((seq, dq), lambda b, h, i: (b, k_col0 + h)),
            pl.BlockSpec((seq, dv), lambda b, h, i: (b, v_col0 + h)),
            pl.BlockSpec((ctx_len, dq), lambda b, h, i: (ctx0 + b, k_col0 + h)),
            pl.BlockSpec((ctx_len, dv), lambda b, h, i: (ctx0 + b, v_col0 + h)),
        ],
        out_specs=pl.BlockSpec((bq, dv), lambda b, h, i: (b * nq + i, h)),
        out_shape=jax.ShapeDtypeStruct((batch * seq, heads * dv), BF16),
        scratch_shapes=[pltpu.VMEM((bq, ck), F32), pltpu.VMEM((bq, ck), F32)],
        compiler_params=_params("arbitrary", "arbitrary", "arbitrary"),
        name=name,
    )(q_arr, k_arr, v_arr, k_arr, v_arr)


def _context_attention_kernel(q_ref, k_ref, v_ref, o_ref):
    _, l, acc = _softmax_first(_dot_nt(q_ref[...], k_ref[...]), v_ref[...])
    o_ref[...] = (acc / l).astype(o_ref.dtype)


def _context_attention(q_arr, k_arr, v_arr, *, batch, heads, dq, dv, row0, ctx_len, k_col0, v_col0, name):
    blk0 = row0 // ctx_len
    return pl.pallas_call(
        _context_attention_kernel,
        grid=(batch, heads),
        in_specs=[
            pl.BlockSpec((ctx_len, dq), lambda b, h: (blk0 + b, h)),
            pl.BlockSpec((ctx_len, dq), lambda b, h: (blk0 + b, k_col0 + h)),
            pl.BlockSpec((ctx_len, dv), lambda b, h: (blk0 + b, v_col0 + h)),
        ],
        out_specs=pl.BlockSpec((ctx_len, dv), lambda b, h: (b, h)),
        out_shape=jax.ShapeDtypeStruct((batch * ctx_len, heads * dv), BF16),
        compiler_params=_params("arbitrary", "arbitrary"),
        name=name,
    )(q_arr, k_arr, v_arr)


def _na_kernel(q_ref, k_ref, v_ref, kc_ref, vc_ref, ta_ref, tb_ref, o_ref, *s_refs, grid_rows):
    bq = NA_QROWS * GRID_W
    nwin = NA_KROWS * GRID_W

    def scores(half, t_ref, sw_ref, sc_ref):
        r = pl.program_id(2) * 2 + half
        ks = jnp.clip(r * NA_QROWS - NA_KH // 2, 0, grid_rows - NA_KROWS)
        keys = pl.ds(pl.multiple_of(ks * GRID_W, GRID_W), nwin)
        q = q_ref[half * bq:(half + 1) * bq, :]
        s_win = _dot_nt(q, k_ref[keys, :]) + t_ref[...]
        s_ctx = _dot_nt(q, kc_ref[...])
        sw_ref[...] = s_win
        sc_ref[...] = s_ctx
        return keys, jnp.maximum(jnp.max(s_win, axis=-1, keepdims=True), jnp.max(s_ctx, axis=-1, keepdims=True))

    def finish(half, keys, m, sw_ref, sc_ref):
        p_win = jnp.exp2(sw_ref[...] - m)
        p_ctx = jnp.exp2(sc_ref[...] - m)
        l = jnp.sum(p_win, axis=-1, keepdims=True) + jnp.sum(p_ctx, axis=-1, keepdims=True)
        acc = _dot(p_win.astype(BF16), v_ref[keys, :]) + _dot(p_ctx.astype(BF16), vc_ref[...])
        o_ref[half * bq:(half + 1) * bq, :] = (acc / l).astype(o_ref.dtype)

    first = scores(0, ta_ref, *s_refs[0:2])
    second = scores(1, tb_ref, *s_refs[2:4])
    finish(0, *first, *s_refs[0:2])
    finish(1, *second, *s_refs[2:4])


def _na_table(rpb, grid_rows):
    tabs = []
    for r0 in (0, NA_QROWS, grid_rows - NA_QROWS):
        ks = min(max(r0 - NA_KH // 2, 0), grid_rows - NA_KROWS)
        r = r0 + np.arange(NA_QROWS)[:, None, None, None]
        w = np.arange(GRID_W)[None, :, None, None]
        kr = ks + np.arange(NA_KROWS)[None, None, :, None]
        j = np.arange(GRID_W)[None, None, None, :]
        rs = np.clip(r - NA_KH // 2, 0, grid_rows - NA_KH)
        cs = np.clip(w - NA_KW // 2, 0, GRID_W - NA_KW)
        valid = (kr >= rs) & (kr < rs + NA_KH) & (j >= cs) & (j < cs + NA_KW)
        sel_r = ((kr - r + (NA_KH - 1))[:, 0, :, 0, None] == np.arange(2 * NA_KH - 1)).astype(np.float32)
        sel_c = ((j - w + (NA_KW - 1))[0, :, 0, :, None] == np.arange(2 * NA_KW - 1)).astype(np.float32)
        bias = jnp.einsum("rap,hpq,wjq->hrwaj", sel_r, rpb.astype(F32), sel_c,
                          precision=lax.Precision.HIGHEST)
        tab = jnp.where(np.broadcast_to(valid, bias.shape[1:])[None], bias * LOG2E, MASK_VALUE)
        tabs.append(tab.reshape(rpb.shape[0], NA_QROWS * GRID_W, NA_KROWS * GRID_W))
    return jnp.stack(tabs)


def _neighborhood_attention(on, table, batch, seq, ctx_len):
    grid_rows = seq // GRID_W
    nstep = grid_rows // (2 * NA_QROWS)
    bq = NA_QROWS * GRID_W
    nwin = NA_KROWS * GRID_W
    d = NA_HEAD_DIM
    ctx0 = batch * seq // ctx_len
    assert nstep >= 2
    return pl.pallas_call(
        functools.partial(_na_kernel, grid_rows=grid_rows),
        grid=(batch, NA_HEADS, nstep),
        in_specs=[
            pl.BlockSpec((2 * bq, d), lambda b, h, r: (b * nstep + r, h)),
            pl.BlockSpec((seq, d), lambda b, h, r: (b, NA_HEADS + h)),
            pl.BlockSpec((seq, d), lambda b, h, r: (b, 2 * NA_HEADS + h)),
            pl.BlockSpec((ctx_len, d), lambda b, h, r: (ctx0 + b, NA_HEADS + h)),
            pl.BlockSpec((ctx_len, d), lambda b, h, r: (ctx0 + b, 2 * NA_HEADS + h)),
            pl.BlockSpec((None, None, bq, nwin), lambda b, h, r: (jnp.where(r == 0, 0, 1), h, 0, 0)),
            pl.BlockSpec((None, None, bq, nwin), lambda b, h, r: (jnp.where(r == nstep - 1, 2, 1), h, 0, 0)),
        ],
        out_specs=pl.BlockSpec((2 * bq, d), lambda b, h, r: (b * nstep + r, h)),
        out_shape=jax.ShapeDtypeStruct((batch * seq, NA_WIDTH), BF16),
        scratch_shapes=[pltpu.VMEM((bq, nwin), F32), pltpu.VMEM((bq, ctx_len), F32)] * 2,
        compiler_params=_params("arbitrary", "arbitrary", "arbitrary"),
        name="neighborhood_attention",
    )(on, on, on, on, on, table, table)


def _dft_consts(n):
    jk = (np.arange(n)[:, None] * np.arange(n)[None, :]) % n
    ang = 2.0 * np.pi * jk / n
    return np.cos(ang), np.sin(ang)


def _hi_lo(m):
    m = jnp.asarray(m, F32)
    hi = m.astype(BF16)
    return hi, (m - hi.astype(F32)).astype(BF16)


def _fn_stage1_kernel(x_ref, fh_ref, fl_ref, tc_ref, ts_ref, o_ref):
    n = GRID_W
    b = _dot3_left(fh_ref[...], fl_ref[...], x_ref[...])
    br, bi = b[:n], b[n:]
    tc, ts = tc_ref[...], ts_ref[...]
    o_ref[0] = br * tc + bi * ts
    o_ref[1] = bi * tc - br * ts


def _fn_stage2_kernel(t_ref, gh_ref, gl_ref, ch_ref, cl_ref, w_ref, p_ref, o_ref):
    n = GRID_W
    outs = []
    for j in range(FN_K2):
        t = jnp.concatenate([t_ref[0, j], t_ref[1, j]], axis=0)
        a = _dot3_left(gh_ref[...], gl_ref[...], t)
        outs.append(a)
    ar = jnp.concatenate([a[:n] for a in outs], axis=0)
    ai = jnp.concatenate([a[n:] for a in outs], axis=0)
    ys = []
    for g in range(FN_GROUPS):
        sl = slice(g * FN_CH, (g + 1) * FN_CH)
        z = jnp.concatenate([ar[:, sl], ai[:, sl]], axis=1)
        spec = _dot3_right(z, ch_ref[...], cl_ref[...])
        ys.append(_dot(spec.astype(BF16), w_ref[g]).astype(BF16))
    y = jnp.concatenate(ys, axis=1)
    y = _dot(p_ref[...], y).astype(BF16)
    o_ref[...] = y.reshape(n, FN_K2, FN_WIDTH)


def _fourier_latent(of, w_fnet, batch, seq):
    n = GRID_W
    assert seq == n * n
    cols = n * FN_WIDTH
    c64, s64 = _dft_consts(n)
    f1h, f1l = _hi_lo(np.concatenate([c64, -s64], axis=0))
    tw = 2.0 * np.pi * (np.arange(n)[:, None] * np.arange(n)[None, :]) / (n * n)
    tc = jnp.asarray(np.repeat(np.cos(tw).reshape(n, n, 1), FN_WIDTH, axis=2).reshape(n, cols), F32)
    ts = jnp.asarray(np.repeat(np.sin(tw).reshape(n, n, 1), FN_WIDTH, axis=2).reshape(n, cols), F32)
    x2 = of[:batch * seq].reshape(batch, n, cols)
    cb = 4096
    t = pl.pallas_call(
        _fn_stage1_kernel,
        grid=(batch, cols // cb),
        in_specs=[
            pl.BlockSpec((None, n, cb), lambda b, j: (b, 0, j)),
            pl.BlockSpec(f1h.shape, lambda b, j: (0, 0)),
            pl.BlockSpec(f1l.shape, lambda b, j: (0, 0)),
            pl.BlockSpec((n, cb), lambda b, j: (0, j)),
            pl.BlockSpec((n, cb), lambda b, j: (0, j)),
        ],
        out_specs=pl.BlockSpec((None, 2, n, cb), lambda b, j: (b, 0, 0, j)),
        out_shape=jax.ShapeDtypeStruct((batch, 2, n, cols), F32),
        compiler_params=_params("arbitrary", "arbitrary"),
        name="fourier_rows",
    )(x2, f1h, f1l, tc, ts)
    t = t.reshape(batch, 2, n, n, FN_WIDTH)

    g2h, g2l = _hi_lo(np.block([[c64, s64], [-s64, c64]]))
    cc, sc = _dft_consts(FN_CH)
    norm = 1.0 / np.sqrt(seq * FN_CH)
    c4h, c4l = _hi_lo(np.concatenate([cc, sc], axis=0) * norm)
    rows = FN_K2 * n
    perm = np.zeros((rows, rows), np.float32)
    k1 = np.arange(n)[:, None]
    j = np.arange(FN_K2)[None, :]
    perm[(k1 * FN_K2 + j).ravel(), (j * n + k1).ravel()] = 1.0
    perm = jnp.asarray(perm, BF16)
    const = lambda a: pl.BlockSpec(a.shape, lambda b, i: (0,) * a.ndim)
    y = pl.pallas_call(
        _fn_stage2_kernel,
        grid=(batch, n // FN_K2),
        in_specs=[
            pl.BlockSpec((None, 2, FN_K2, n, FN_WIDTH), lambda b, i: (b, 0, i, 0, 0)),
            const(g2h), const(g2l), const(c4h), const(c4l), const(w_fnet), const(perm),
        ],
        out_specs=pl.BlockSpec((None, n, FN_K2, FN_WIDTH), lambda b, i: (b, 0, i, 0)),
        out_shape=jax.ShapeDtypeStruct((batch, n, n, FN_WIDTH), BF16),
        compiler_params=_params("arbitrary", "arbitrary"),
        name="fourier_cols_channels",
    )(t, g2h, g2l, c4h, c4l, w_fnet, perm)
    return y.reshape(batch * seq, FN_WIDTH)


def _fn_ctx_kernel(x_ref, fh_ref, fl_ref, ch_ref, cl_ref, w_ref, o_ref, *, n):
    a = _dot3_left(fh_ref[...], fl_ref[...], x_ref[...])
    ar, ai = a[:n], a[n:]
    ys = []
    for g in range(FN_GROUPS):
        sl = slice(g * FN_CH, (g + 1) * FN_CH)
        z = jnp.concatenate([ar[:, sl], ai[:, sl]], axis=1)
        spec = _dot3_right(z, ch_ref[...], cl_ref[...])
        ys.append(_dot(spec.astype(BF16), w_ref[g]).astype(BF16))
    o_ref[...] = jnp.concatenate(ys, axis=1)


def _fourier_context(of, w_fnet, batch, row0, n):
    c, s = _dft_consts(n)
    fh, fl = _hi_lo(np.concatenate([c, -s], axis=0))
    cc, sc = _dft_consts(FN_CH)
    c4h, c4l = _hi_lo(np.concatenate([cc, sc], axis=0) / np.sqrt(n * FN_CH))
    const = lambda a: pl.BlockSpec(a.shape, lambda b: (0,) * a.ndim)
    blk0 = row0 // n
    return pl.pallas_call(
        functools.partial(_fn_ctx_kernel, n=n),
        grid=(batch,),
        in_specs=[pl.BlockSpec((n, FN_WIDTH), lambda b: (blk0 + b, 0)),
                  const(fh), const(fl), const(c4h), const(c4l), const(w_fnet)],
        out_specs=pl.BlockSpec((n, FN_WIDTH), lambda b: (b, 0)),
        out_shape=jax.ShapeDtypeStruct((batch * n, FN_WIDTH), BF16),
        compiler_params=_params("arbitrary"),
        name="fourier_context",
    )(of, fh, fl, c4h, c4l, w_fnet)


def _proj_residual_kernel(*refs, n_in, lat_blocks, has_ctx, final_norm):
    lat = refs[:n_in]
    ctx = refs[n_in:2 * n_in] if has_ctx else ()
    rest = refs[(2 if has_ctx else 1) * n_in:]
    w_refs, (x_ref, gt_ref, *gf_ref, o_ref) = rest[:n_in], rest[n_in:]

    def run(a_refs):
        for n in range(0, D_MODEL, PROJ_COLS):
            cols = slice(n, n + PROJ_COLS)
            acc = None
            for a_ref, w_ref in zip(a_refs, w_refs):
                t = _dot(a_ref[...], w_ref[:, cols])
                acc = t if acc is None else acc + t
            o_ref[:, cols] = x_ref[:, cols] + gt_ref[:, cols] * acc

    if has_ctx:
        i = pl.program_id(0)
        pl.when(i < lat_blocks)(lambda: run(lat))
        pl.when(i >= lat_blocks)(lambda: run(ctx))
    else:
        run(lat)
    if final_norm:
        o_ref[...] = _rms(o_ref[...], gf_ref[0][...])


def _proj_residual(xa, lat_ins, ctx_ins, ws, mod_l, gate_k, rows_per_batch, n_rows, bm, final_gain=None):
    n_in = len(lat_ins)
    has_ctx = ctx_ins is not None
    lat_blocks = lat_ins[0].shape[0] // bm
    bpb = rows_per_batch // bm
    in_specs = [pl.BlockSpec((bm, a.shape[1]), lambda i: (jnp.minimum(i, lat_blocks - 1), 0)) for a in lat_ins]
    if has_ctx:
        in_specs += [pl.BlockSpec((bm, a.shape[1]), lambda i: (jnp.maximum(i - lat_blocks, 0), 0)) for a in ctx_ins]
    in_specs += [pl.BlockSpec(w.shape, lambda i: (0, 0), pipeline_mode=pl.Buffered(1)) for w in ws]
    in_specs += [
        pl.BlockSpec((bm, D_MODEL), lambda i: (i, 0)),
        pl.BlockSpec((None, 1, D_MODEL), lambda i: (i // bpb, 0, gate_k)),
    ]
    args = list(lat_ins) + (list(ctx_ins) if has_ctx else []) + list(ws) + [xa, mod_l]
    final_norm = final_gain is not None
    if final_norm:
        in_specs.append(pl.BlockSpec((1, D_MODEL), lambda i: (0, 0)))
        args.append(final_gain)
    return pl.pallas_call(
        functools.partial(_proj_residual_kernel, n_in=n_in, lat_blocks=lat_blocks, has_ctx=has_ctx,
                          final_norm=final_norm),
        grid=(n_rows // bm,),
        in_specs=in_specs,
        out_specs=pl.BlockSpec((bm, D_MODEL), lambda i: (i, 0)),
        out_shape=jax.ShapeDtypeStruct((n_rows, D_MODEL) if final_norm else xa.shape, F32),
        input_output_aliases={} if final_norm else {n_in * (2 if has_ctx else 1) + n_in: 0},
        compiler_params=_params("arbitrary"),
        name="projection_residual",
    )(*args)


def _ffn_up_kernel(xm_ref, xp_ref, xn_ref, sh_ref, sc_ref, g_ref, wg_ref, wv_ref, cwg_ref, cwv_ref, cbg_ref,
                   cbv_ref, o_ref, h_ref, *u_refs, lat_blocks, seq, ctx_len):
    bm = o_ref.shape[0]
    i = pl.program_id(0)

    @pl.when(pl.program_id(1) == 0)
    def _():
        g, sc, sh = g_ref[...], sc_ref[...], sh_ref[...]
        h_ref[0:HALO, :] = _normmod(xp_ref[...], g, sc, sh).astype(BF16)
        h_ref[HALO:HALO + bm, :] = _normmod(xm_ref[...], g, sc, sh).astype(BF16)
        h_ref[HALO + bm:, :] = _normmod(xn_ref[...], g, sc, sh).astype(BF16)

    t = FFN_TILE
    n_tiles = o_ref.shape[1] // t

    def pair(g_ref, v_ref, rows, k):
        cols = slice(t * k, t * (k + 1))
        return jnp.concatenate([g_ref[rows, cols], v_ref[rows, cols]], axis=1)

    def matmul(k):
        u_refs[k % 2][...] = _dot(h_ref[...], pair(wg_ref, wv_ref, slice(None), k))

    def run(interior_boundaries):
        if interior_boundaries:
            rows = lax.broadcasted_iota(jnp.int32, (bm, 1), 0)
            first = (rows & (ctx_len - 1)) == 0
            last = ((rows + 1) & (ctx_len - 1)) == 0
        else:
            starts = (i * bm) % seq == 0
            ends = ((i + 1) * bm) % seq == 0
        matmul(0)
        for k in range(n_tiles):
            if k + 1 < n_tiles:
                matmul(k + 1)
            src = u_refs[k % 2]
            if interior_boundaries:
                prev = jnp.where(first, 0.0, src[HALO - 1:HALO - 1 + bm, :])
                nxt = jnp.where(last, 0.0, src[HALO + 1:HALO + 1 + bm, :])
            else:
                src[HALO - 1:HALO, :] = jnp.where(starts, 0.0, src[HALO - 1:HALO, :])
                src[HALO + bm:HALO + bm + 1, :] = jnp.where(ends, 0.0, src[HALO + bm:HALO + bm + 1, :])
                prev = src[HALO - 1:HALO - 1 + bm, :]
                nxt = src[HALO + 1:HALO + 1 + bm, :]
            c = (prev * pair(cwg_ref, cwv_ref, slice(0, 1), k)
                 + src[HALO:HALO + bm, :] * pair(cwg_ref, cwv_ref, slice(1, 2), k)
                 + nxt * pair(cwg_ref, cwv_ref, slice(2, 3), k) + pair(cbg_ref, cbv_ref, slice(None), k))
            gate, val = c[:, :t], c[:, t:]
            o_ref[:, t * k:t * (k + 1)] = (gate * _sigmoid(gate) * val).astype(BF16)

    pl.when(i < lat_blocks)(lambda: run(False))
    pl.when(i >= lat_blocks)(lambda: run(True))


def _ffn_up(xa, mod_l, g, w_up, conv_w, conv_b, n_rows, n_lat, seq, ctx_len):
    rows = xa.shape[0]
    bm, bn = FFN_ROWS, 512
    assert seq % bm == 0 and bm % ctx_len == 0 and n_lat % bm == 0 and n_rows % bm == 0
    blocks_per_batch, n_blocks, lat_blocks = seq // bm, n_rows // bm, n_lat // bm
    nj = D_FF // bn
    hb = bm // HALO
    last_halo = rows // HALO - 1
    assert seq & (seq - 1) == 0 and ctx_len & (ctx_len - 1) == 0
    return pl.pallas_call(
        functools.partial(_ffn_up_kernel, lat_blocks=lat_blocks, seq=seq, ctx_len=ctx_len),
        grid=(n_blocks, nj),
        in_specs=[
            pl.BlockSpec((bm, D_MODEL), lambda i, j: (i, 0)),
            pl.BlockSpec((HALO, D_MODEL), lambda i, j: (jnp.maximum(i * hb - 1, 0), 0)),
            pl.BlockSpec((HALO, D_MODEL), lambda i, j: (jnp.minimum((i + 1) * hb, last_halo), 0)),
            _mod_spec(3, blocks_per_batch),
            _mod_spec(4, blocks_per_batch),
            pl.BlockSpec((1, D_MODEL), lambda i, j: (0, 0)),
            pl.BlockSpec((D_MODEL, bn), lambda i, j: (0, j)),
            pl.BlockSpec((D_MODEL, bn), lambda i, j: (0, nj + j)),
            pl.BlockSpec((CONV_W, bn), lambda i, j: (0, j)),
            pl.BlockSpec((CONV_W, bn), lambda i, j: (0, nj + j)),
            pl.BlockSpec((1, bn), lambda i, j: (0, j)),
            pl.BlockSpec((1, bn), lambda i, j: (0, nj + j)),
        ],
        out_specs=pl.BlockSpec((bm, bn), lambda i, j: (i, j)),
        out_shape=jax.ShapeDtypeStruct((n_blocks * bm, D_FF), BF16),
        scratch_shapes=[pltpu.VMEM((bm + 2 * HALO, D_MODEL), BF16)]
        + [pltpu.VMEM((bm + 2 * HALO, 2 * FFN_TILE), F32)] * 2,
        compiler_params=_params("arbitrary", "arbitrary"),
        name="ffn_up_conv_gate",
    )(xa, xa, xa, mod_l, mod_l, g, w_up, w_up, conv_w, conv_w, conv_b, conv_b)


def _rope_table(seq, pad_rows):
    n = MLA_ROPE // 4
    freqs = ROPE_BASE ** (-jnp.arange(n, dtype=F32) / n)
    pos = jnp.arange(seq)
    ang_r = (pos // GRID_W).astype(F32)[:, None] * freqs
    ang_c = (pos % GRID_W).astype(F32)[:, None] * freqs
    cos = jnp.concatenate([jnp.cos(ang_r)] * 2 + [jnp.cos(ang_c)] * 2, axis=1)
    sin = jnp.concatenate([-jnp.sin(ang_r), jnp.sin(ang_r), -jnp.sin(ang_c), jnp.sin(ang_c)], axis=1)
    lat = jnp.concatenate([cos, sin], axis=1)
    ident = jnp.concatenate([jnp.ones((pad_rows, MLA_ROPE), F32), jnp.zeros((pad_rows, MLA_ROPE), F32)], axis=1)
    return jnp.concatenate([lat, ident], axis=0)


def _partner_perm():
    q = MLA_ROPE // 4
    return np.concatenate([np.arange(q, 2 * q), np.arange(0, q), np.arange(3 * q, 4 * q), np.arange(2 * q, 3 * q)])


def _layout_w_in(w):
    cq, ckv, kr, qn, kn, vn, f = jnp.split(w, np.cumsum(
        (MLA_Q_RANK, MLA_KV_RANK, MLA_ROPE, NA_WIDTH, NA_WIDTH, NA_WIDTH))[:].tolist(), axis=1)
    return jnp.concatenate([cq, ckv, qn, kn, vn, f, kr, kr[:, _partner_perm()]], axis=1).astype(BF16)


def _layout_w_uq(w):
    w = w.reshape(MLA_Q_RANK, MLA_HEADS, MLA_NOPE + MLA_ROPE)
    rope = w[:, :, MLA_NOPE:]
    return jnp.concatenate([w, rope[:, :, _partner_perm()]], axis=2).reshape(MLA_Q_RANK, -1).astype(BF16)


def _layout_w_ukv(w):
    w = w.reshape(MLA_KV_RANK, MLA_HEADS, MLA_NOPE + MLA_V)
    return jnp.concatenate([w[:, :, :MLA_NOPE].reshape(MLA_KV_RANK, -1),
                            w[:, :, MLA_NOPE:].reshape(MLA_KV_RANK, -1)], axis=1).astype(BF16)


def kernel(x, c, ctx, c_ctx, w_mod, b_mod, g_attn, g_ffn, w_in, g_q, w_uq, g_kv, w_ukv, na_rpb, w_fnet, w_out,
           w_up, conv_w, conv_b, w_down, g_final):
    batch, seq, d = x.shape
    ctx_len = ctx.shape[1]
    depth = w_mod.shape[0]
    n_lat, n_ctx = batch * seq, batch * ctx_len
    bm = ROW_BLOCK
    lat_blocks = n_lat // bm
    all_blocks = (n_lat + n_ctx) // bm
    blocks_per_batch = seq // bm
    assert n_ctx % bm == 0 and n_lat // bm // blocks_per_batch == batch and batch < 8

    xa = jnp.concatenate([x.reshape(n_lat, d), ctx.reshape(n_ctx, d)], axis=0)
    cin = jnp.zeros((8, d), F32).at[:batch].set(c).at[batch].set(c_ctx)
    mod = _modulation(cin, w_mod, b_mod)
    cs_tab = _rope_table(seq, bm)
    row = lambda v: v.reshape(1, -1)

    for l in range(depth):
        ctx_out = l < depth - 1
        mod_l = mod[l].reshape(8, 1, 6 * d)
        oc, on, of, okr = _inproj(xa, mod_l, row(g_attn[l]), _layout_w_in(w_in[l]), blocks_per_batch)
        qm, km, vm = _mla_up(oc, okr, cs_tab, row(g_q[l]), row(g_kv[l]), _layout_w_uq(w_uq[l]),
                             _layout_w_ukv(w_ukv[l]), lat_blocks, blocks_per_batch)

        o_mla = _attention(qm, km, vm, batch=batch, heads=MLA_HEADS, dq=MLA_QK_PAD, dv=MLA_V, bq=1024, ck=1024,
                           seq=seq, ctx_len=ctx_len, k_col0=0, v_col0=0, name="mla_attention")
        o_na = _neighborhood_attention(on, _na_table(na_rpb[l], seq // GRID_W), batch, seq, ctx_len)
        o_fn = _fourier_latent(of, w_fnet[l].astype(BF16), batch, seq)
        w_o = w_out[l].astype(BF16)
        w_os = (w_o[:MLA_WIDTH], w_o[MLA_WIDTH:MLA_WIDTH + NA_WIDTH], w_o[MLA_WIDTH + NA_WIDTH:])

        ctx_ins = None
        if ctx_out:
            o_mla_c = _context_attention(
                qm, km, vm, batch=batch, heads=MLA_HEADS, dq=MLA_QK_PAD, dv=MLA_V, row0=n_lat, ctx_len=ctx_len,
                k_col0=0, v_col0=0, name="mla_attention_context")
            o_na_c = _context_attention(
                on, on, on, batch=batch, heads=NA_HEADS, dq=NA_HEAD_DIM, dv=NA_HEAD_DIM, row0=n_lat,
                ctx_len=ctx_len, k_col0=NA_HEADS, v_col0=2 * NA_HEADS, name="na_attention_context")
            o_fn_c = _fourier_context(of, w_fnet[l].astype(BF16), batch, n_lat, ctx_len)
            ctx_ins = (o_mla_c, o_na_c, o_fn_c)

        n_blocks = all_blocks if ctx_out else lat_blocks
        n_rows = n_blocks * bm
        xa = _proj_residual(xa, (o_mla, o_na, o_fn), ctx_ins, w_os, mod_l, 2, seq, n_rows, bm)
        a = _ffn_up(xa, mod_l, row(g_ffn[l]), w_up[l].astype(BF16), conv_w[l], row(conv_b[l]),
                    n_rows, n_lat, seq, ctx_len)
        xa = _proj_residual(xa, (a,), None, (w_down[l].astype(BF16),), mod_l, 5, seq, n_rows, bm,
                            final_gain=None if ctx_out else row(g_final))

    return xa.reshape(batch, seq, d)
```

```python
import functools

import numpy as np
import jax
import jax.numpy as jnp
from jax import lax
from jax.experimental import pallas as pl
from jax.experimental.pallas import tpu as pltpu

F32 = jnp.float32
BF16 = jnp.bfloat16

D_MODEL = 2048
GRID_W = 64
EPS = 1e-6

MLA_HEADS = 8
MLA_NOPE = 128
MLA_ROPE = 64
MLA_V = 128
MLA_Q_RANK = 512
MLA_KV_RANK = 512
MLA_QK_PAD = 256
MLA_SCALE = (MLA_NOPE + MLA_ROPE) ** -0.5
ROPE_BASE = 10000.0

NA_HEADS = 4
NA_HEAD_DIM = 128
NA_KH = 8
NA_KW = 16
NA_SCALE = NA_HEAD_DIM ** -0.5
NA_QROWS = 8
NA_KROWS = NA_QROWS + NA_KH - 1

FN_GROUPS = 4
FN_CH = 128
FN_WIDTH = FN_GROUPS * FN_CH
FN_K2 = 16

MLA_WIDTH = MLA_HEADS * MLA_V
NA_WIDTH = NA_HEADS * NA_HEAD_DIM
D_FF = 5632
FFN_TILE = 128
CONV_W = 3

ROW_BLOCK = 512
FFN_ROWS = 1024
HALO = 16
PROJ_COLS = 512
MASK_VALUE = -1e30
LOG2E = 1.4426950408889634
VMEM_LIMIT = 56 * 1024 * 1024


def _params(*sem, flags=None):
    return pltpu.CompilerParams(dimension_semantics=sem, vmem_limit_bytes=VMEM_LIMIT, flags=flags)


def _dot(a, b):
    return jnp.dot(a, b, preferred_element_type=F32)


def _dot_nt(a, b):
    return lax.dot_general(a, b, (((1,), (1,)), ((), ())), preferred_element_type=F32)


def _split(x):
    hi = x.astype(BF16)
    lo = (x - hi.astype(F32)).astype(BF16)
    return hi, lo


def _dot3_left(m_hi, m_lo, x):
    x_hi, x_lo = _split(x)
    return _dot(m_hi, x_hi) + (_dot(m_hi, x_lo) + _dot(m_lo, x_hi))


def _dot3_right(x, m_hi, m_lo):
    x_hi, x_lo = _split(x)
    return _dot(x_hi, m_hi) + (_dot(x_lo, m_hi) + _dot(x_hi, m_lo))


def _sigmoid(x):
    return 1.0 / (1.0 + jnp.exp(-x))


def _rms(x, g):
    y = x * lax.rsqrt(jnp.mean(x * x, axis=-1, keepdims=True) + EPS)
    return y * g


def _normmod(x, g, sc, sh):
    return _rms(x, g) * (1.0 + sc) + sh


def _mod_kernel(c_ref, w_ref, b_ref, o_ref):
    c = c_ref[...]
    s = (c * _sigmoid(c)).astype(BF16)
    o_ref[...] = _dot(s, w_ref[...].astype(BF16)) + b_ref[...]


def _modulation(cin, w_mod, b_mod):
    depth, d, n = w_mod.shape
    bn = 1024
    return pl.pallas_call(
        _mod_kernel,
        grid=(depth, n // bn),
        in_specs=[
            pl.BlockSpec((8, d), lambda l, j: (0, 0)),
            pl.BlockSpec((None, d, bn), lambda l, j: (l, 0, j)),
            pl.BlockSpec((None, 1, bn), lambda l, j: (l, 0, j)),
        ],
        out_specs=pl.BlockSpec((None, 8, bn), lambda l, j: (l, 0, j)),
        out_shape=jax.ShapeDtypeStruct((depth, 8, n), F32),
        compiler_params=_params("arbitrary", "arbitrary"),
        name="modulation",
    )(cin, w_mod, b_mod.reshape(depth, 1, n))


def _inproj_kernel(x_ref, xc_ref, sh_ref, sc_ref, g_ref, w_ref, oc_ref, on_ref, of_ref, okr_ref, *, lat_blocks):
    def run(src_ref):
        h = _normmod(src_ref[...], g_ref[...], sc_ref[...], sh_ref[...]).astype(BF16)
        oc_ref[...] = _dot(h, w_ref[:, 0:1024])
        on_ref[:, 0:NA_WIDTH] = (_dot(h, w_ref[:, 1024:1024 + NA_WIDTH]) * (NA_SCALE * LOG2E)).astype(BF16)
        on_ref[:, NA_WIDTH:] = _dot(h, w_ref[:, 1024 + NA_WIDTH:2560]).astype(BF16)
        of_ref[...] = _dot(h, w_ref[:, 2560:3072])
        okr_ref[...] = _dot(h, w_ref[:, 3072:3200])

    i = pl.program_id(0)
    pl.when(i < lat_blocks)(lambda: run(x_ref))
    pl.when(i >= lat_blocks)(lambda: run(xc_ref))


def _mod_spec(k, blocks_per_batch):
    return pl.BlockSpec((None, 1, D_MODEL), lambda i, *_: (i // blocks_per_batch, 0, k))


def _two_source_specs(bm, width, lat_blocks, ctx_block0):
    return [pl.BlockSpec((bm, width), lambda i, *_: (jnp.minimum(i, lat_blocks - 1), 0)),
            pl.BlockSpec((bm, width), lambda i, *_: (ctx_block0 + jnp.maximum(i - lat_blocks, 0), 0))]


def _inproj(x_lat, x_ctx, ctx_block0, rows, lat_blocks, mod_l, g, w_p, blocks_per_batch):
    bm = ROW_BLOCK
    row = lambda w: pl.BlockSpec((bm, w), lambda i: (i, 0))
    return pl.pallas_call(
        functools.partial(_inproj_kernel, lat_blocks=lat_blocks),
        grid=(rows // bm,),
        in_specs=_two_source_specs(bm, D_MODEL, lat_blocks, ctx_block0) + [
            _mod_spec(0, blocks_per_batch),
            _mod_spec(1, blocks_per_batch),
            pl.BlockSpec((1, D_MODEL), lambda i: (0, 0)),
            pl.BlockSpec(w_p.shape, lambda i: (0, 0), pipeline_mode=pl.Buffered(1)),
        ],
        out_specs=[row(1024), row(1536), row(512), row(128)],
        out_shape=[
            jax.ShapeDtypeStruct((rows, 1024), F32),
            jax.ShapeDtypeStruct((rows, 1536), BF16),
            jax.ShapeDtypeStruct((rows, 512), F32),
            jax.ShapeDtypeStruct((rows, 128), F32),
        ],
        compiler_params=_params("arbitrary"),
        name="in_projection",
    )(x_lat, x_ctx, mod_l, mod_l, g, w_p)


def _rope(t, cs):
    t = t * cs
    return t + pltpu.roll(t, 64, axis=1)


def _mla_up_kernel(c_ref, kr_ref, cs_ref, gq_ref, gkv_ref, wq_ref, wkv_ref, q_ref, k_ref, v_ref):
    cs = cs_ref[...]
    cq = _rms(c_ref[:, 0:MLA_Q_RANK], gq_ref[...]).astype(BF16)
    ckv = _rms(c_ref[:, MLA_Q_RANK:MLA_Q_RANK + MLA_KV_RANK], gkv_ref[...]).astype(BF16)
    q = _dot(cq, wq_ref[...]) * (MLA_SCALE * LOG2E)
    kv = _dot(ckv, wkv_ref[...])
    lane = lax.broadcasted_iota(jnp.int32, cs.shape, 1)
    k_rope = jnp.where(lane < MLA_ROPE, _rope(kr_ref[...], cs), 0.0).astype(BF16)
    for h in range(MLA_HEADS):
        o = h * MLA_QK_PAD
        q_ref[:, o:o + MLA_NOPE] = q[:, o:o + MLA_NOPE].astype(BF16)
        q_ref[:, o + MLA_NOPE:o + MLA_QK_PAD] = _rope(q[:, o + MLA_NOPE:o + MLA_QK_PAD], cs).astype(BF16)
        k_ref[:, o:o + MLA_NOPE] = kv[:, h * MLA_NOPE:(h + 1) * MLA_NOPE].astype(BF16)
        k_ref[:, o + MLA_NOPE:o + MLA_QK_PAD] = k_rope
    v_ref[...] = kv[:, MLA_HEADS * MLA_NOPE:].astype(BF16)


def _mla_up(oc, okr, cs_tab, g_q, g_kv, wq_p, wkv_p, lat_blocks, pos_blocks):
    rows = oc.shape[0]
    bm = ROW_BLOCK
    row = lambda w: pl.BlockSpec((bm, w), lambda i: (i, 0))
    const = lambda a: pl.BlockSpec(a.shape, lambda i: (0, 0))
    cs_spec = pl.BlockSpec((bm, 128), lambda i: (jnp.where(i < lat_blocks, i % pos_blocks, pos_blocks), 0))
    qk_w = MLA_HEADS * MLA_QK_PAD
    return pl.pallas_call(
        _mla_up_kernel,
        grid=(rows // bm,),
        in_specs=[row(1024), row(128), cs_spec, const(g_q), const(g_kv), const(wq_p), const(wkv_p)],
        out_specs=[row(qk_w), row(qk_w), row(MLA_WIDTH)],
        out_shape=[
            jax.ShapeDtypeStruct((rows, qk_w), BF16),
            jax.ShapeDtypeStruct((rows, qk_w), BF16),
            jax.ShapeDtypeStruct((rows, MLA_WIDTH), BF16),
        ],
        compiler_params=_params("arbitrary"),
        name="mla_up_projection",
    )(oc, okr, cs_tab, g_q, g_kv, wq_p, wkv_p)


def _softmax_first(s, v):
    m = jnp.max(s, axis=-1, keepdims=True)
    p = jnp.exp2(s - m)
    return m, jnp.sum(p, axis=-1, keepdims=True), _dot(p.astype(BF16), v)


def _softmax_next(s, s_max, v, m, l, acc):
    m_new = jnp.maximum(m, s_max)
    p = jnp.exp2(s - m_new)
    alpha = jnp.exp2(m - m_new)
    l = alpha * l + jnp.sum(p, axis=-1, keepdims=True)
    acc = alpha * acc + _dot(p.astype(BF16), v)
    return m_new, l, acc


def _attention_kernel(q_ref, k_ref, v_ref, kc_ref, vc_ref, o_ref, sa_ref, sb_ref, *, ck, n_main):
    q = q_ref[...]
    carry = _softmax_first(_dot_nt(q, kc_ref[...]), vc_ref[...])
    bufs = (sa_ref, sb_ref)

    def scores(c):
        s = _dot_nt(q, k_ref[c * ck:(c + 1) * ck, :])
        bufs[c % 2][...] = s
        return jnp.max(s, axis=-1, keepdims=True)

    s_max = scores(0)
    for c in range(n_main):
        nxt_max = scores(c + 1) if c + 1 < n_main else None
        carry = _softmax_next(bufs[c % 2][...], s_max, v_ref[c * ck:(c + 1) * ck, :], *carry)
        s_max = nxt_max
    _, l, acc = carry
    o_ref[...] = (acc / l).astype(o_ref.dtype)


def _attention(q_arr, k_arr, v_arr, *, batch, heads, dq, dv, bq, seq, ctx_len, k_col0, v_col0, name, ck=512):
    nq = seq // bq
    ctx0 = batch * seq // ctx_len
    return pl.pallas_call(
        functools.partial(_attention_kernel, ck=ck, n_main=seq // ck),
        grid=(batch, heads, nq),
        in_specs=[
            pl.BlockSpec((bq, dq), lambda b, h, i: (b * nq + i, h)),
            pl.BlockSpec((seq, dq), lambda b, h, i: (b, k_col0 + h)),
            pl.BlockSpec((seq, dv), lambda b, h, i: (b, v_col0 + h)),
            pl.BlockSpec((ctx_len, dq), lambda b, h, i: (ctx0 + b, k_col0 + h)),
            pl.BlockSpec((ctx_len, dv), lambda b, h, i: (ctx0 + b, v_col0 + h)),
        ],
        out_specs=pl.BlockSpec((bq, dv), lambda b, h, i: (b * nq + i, h)),
        out_shape=jax.ShapeDtypeStruct((batch * seq, heads * dv), BF16),
        scratch_shapes=[pltpu.VMEM((bq, ck), F32), pltpu.VMEM((bq, ck), F32)],
        compiler_params=_params("arbitrary", "arbitrary", "arbitrary"),
        name=name,
    )(q_arr, k_arr, v_arr, k_arr, v_arr)


def _context_attention_kernel(q_ref, k_ref, v_ref, o_ref):
    _, l, acc = _softmax_first(_dot_nt(q_ref[...], k_ref[...]), v_ref[...])
    o_ref[...] = (acc / l).astype(o_ref.dtype)


def _context_attention(q_arr, k_arr, v_arr, *, batch, heads, dq, dv, row0, ctx_len, k_col0, v_col0, name):
    blk0 = row0 // ctx_len
    return pl.pallas_call(
        _context_attention_kernel,
        grid=(batch, heads),
        in_specs=[
            pl.BlockSpec((ctx_len, dq), lambda b, h: (blk0 + b, h)),
            pl.BlockSpec((ctx_len, dq), lambda b, h: (blk0 + b, k_col0 + h)),
            pl.BlockSpec((ctx_len, dv), lambda b, h: (blk0 + b, v_col0 + h)),
        ],
        out_specs=pl.BlockSpec((ctx_len, dv), lambda b, h: (b, h)),
        out_shape=jax.ShapeDtypeStruct((batch * ctx_len, heads * dv), BF16),
        compiler_params=_params("arbitrary", "arbitrary"),
        name=name,
    )(q_arr, k_arr, v_arr)


def _na_kernel(q_ref, k_ref, v_ref, kc_ref, vc_ref, ta_ref, tb_ref, o_ref, *s_refs, grid_rows):
    bq = NA_QROWS * GRID_W
    nwin = NA_KROWS * GRID_W

    def scores(half, t_ref, sw_ref, sc_ref):
        r = pl.program_id(2) * 2 + half
        ks = jnp.clip(r * NA_QROWS - NA_KH // 2, 0, grid_rows - NA_KROWS)
        keys = pl.ds(pl.multiple_of(ks * GRID_W, GRID_W), nwin)
        q = q_ref[half * bq:(half + 1) * bq, :]
        s_win = _dot_nt(q, k_ref[keys, :]) + t_ref[...]
        s_ctx = _dot_nt(q, kc_ref[...])
        sw_ref[...] = s_win
        sc_ref[...] = s_ctx
        return keys, jnp.maximum(jnp.max(s_win, axis=-1, keepdims=True), jnp.max(s_ctx, axis=-1, keepdims=True))

    def finish(half, keys, m, sw_ref, sc_ref):
        p_win = jnp.exp2(sw_ref[...] - m)
        p_ctx = jnp.exp2(sc_ref[...] - m)
        l = jnp.sum(p_win, axis=-1, keepdims=True) + jnp.sum(p_ctx, axis=-1, keepdims=True)
        acc = _dot(p_win.astype(BF16), v_ref[keys, :]) + _dot(p_ctx.astype(BF16), vc_ref[...])
        o_ref[half * bq:(half + 1) * bq, :] = (acc / l).astype(o_ref.dtype)

    first = scores(0, ta_ref, *s_refs[0:2])
    second = scores(1, tb_ref, *s_refs[2:4])
    finish(0, *first, *s_refs[0:2])
    finish(1, *second, *s_refs[2:4])


def _na_table(rpb, grid_rows):
    tabs = []
    n_rel_r, n_rel_c = 2 * NA_KH - 1, 2 * NA_KW - 1
    nwin = NA_KROWS * GRID_W
    jw = np.arange(GRID_W)[None, :] - np.arange(GRID_W)[:, None] + (NA_KW - 1)
    sel_c = (jw[:, :, None] == np.arange(n_rel_c)).astype(np.float32)
    by_col = jnp.einsum("hpq,wjq->hpwj", rpb.astype(F32), sel_c, precision=lax.Precision.HIGHEST)
    by_col = jnp.tile(by_col, (1, 1, 1, NA_KROWS))
    for r0 in (0, NA_QROWS, grid_rows - NA_QROWS):
        ks = min(max(r0 - NA_KH // 2, 0), grid_rows - NA_KROWS)
        r = r0 + np.arange(NA_QROWS)[:, None, None, None]
        w = np.arange(GRID_W)[None, :, None, None]
        kr = ks + np.arange(NA_KROWS)[None, None, :, None]
        j = np.arange(GRID_W)[None, None, None, :]
        rs = np.clip(r - NA_KH // 2, 0, grid_rows - NA_KH)
        cs = np.clip(w - NA_KW // 2, 0, GRID_W - NA_KW)
        valid = (kr >= rs) & (kr < rs + NA_KH) & (j >= cs) & (j < cs + NA_KW)
        sel_r = ((kr - r + (NA_KH - 1))[:, 0, :, 0, None] == np.arange(n_rel_r)).astype(np.float32)
        sel_r = np.repeat(sel_r, GRID_W, axis=1).transpose(2, 0, 1)
        bias = jnp.sum(sel_r[None, :, :, None, :] * by_col[:, :, None, :, :], axis=1)
        valid = np.broadcast_to(valid, (NA_QROWS, GRID_W, NA_KROWS, GRID_W)).reshape(NA_QROWS, GRID_W, nwin)
        tab = jnp.where(valid[None], bias * LOG2E, MASK_VALUE)
        tabs.append(tab.reshape(rpb.shape[0], NA_QROWS * GRID_W, nwin))
    return jnp.stack(tabs)


def _neighborhood_attention(on, table, batch, seq, ctx_len):
    grid_rows = seq // GRID_W
    nstep = grid_rows // (2 * NA_QROWS)
    bq = NA_QROWS * GRID_W
    nwin = NA_KROWS * GRID_W
    d = NA_HEAD_DIM
    ctx0 = batch * seq // ctx_len
    assert nstep >= 2
    return pl.pallas_call(
        functools.partial(_na_kernel, grid_rows=grid_rows),
        grid=(batch, NA_HEADS, nstep),
        in_specs=[
            pl.BlockSpec((2 * bq, d), lambda b, h, r: (b * nstep + r, h)),
            pl.BlockSpec((seq, d), lambda b, h, r: (b, NA_HEADS + h)),
            pl.BlockSpec((seq, d), lambda b, h, r: (b, 2 * NA_HEADS + h)),
            pl.BlockSpec((ctx_len, d), lambda b, h, r: (ctx0 + b, NA_HEADS + h)),
            pl.BlockSpec((ctx_len, d), lambda b, h, r: (ctx0 + b, 2 * NA_HEADS + h)),
            pl.BlockSpec((None, None, bq, nwin), lambda b, h, r: (jnp.where(r == 0, 0, 1), h, 0, 0)),
            pl.BlockSpec((None, None, bq, nwin), lambda b, h, r: (jnp.where(r == nstep - 1, 2, 1), h, 0, 0)),
        ],
        out_specs=pl.BlockSpec((2 * bq, d), lambda b, h, r: (b * nstep + r, h)),
        out_shape=jax.ShapeDtypeStruct((batch * seq, NA_WIDTH), BF16),
        scratch_shapes=[pltpu.VMEM((bq, nwin), F32), pltpu.VMEM((bq, ctx_len), F32)] * 2,
        compiler_params=_params("arbitrary", "arbitrary", "arbitrary"),
        name="neighborhood_attention",
    )(on, on, on, on, on, table, table)


def _dft_consts(n):
    jk = (np.arange(n)[:, None] * np.arange(n)[None, :]) % n
    ang = 2.0 * np.pi * jk / n
    return np.cos(ang), np.sin(ang)


def _hi_lo(m):
    m = jnp.asarray(m, F32)
    hi = m.astype(BF16)
    return hi, (m - hi.astype(F32)).astype(BF16)


def _fn_stage1_kernel(x_ref, fh_ref, fl_ref, tc_ref, ts_ref, o_ref):
    n = GRID_W
    b = _dot3_left(fh_ref[...], fl_ref[...], x_ref[...])
    br, bi = b[:n], b[n:]
    tc, ts = tc_ref[...], ts_ref[...]
    o_ref[0] = br * tc + bi * ts
    o_ref[1] = bi * tc - br * ts


def _fn_stage2_kernel(t_ref, gh_ref, gl_ref, ch_ref, cl_ref, w_ref, p_ref, o_ref):
    n = GRID_W
    outs = []
    for j in range(FN_K2):
        t = jnp.concatenate([t_ref[0, j], t_ref[1, j]], axis=0)
        a = _dot3_left(gh_ref[...], gl_ref[...], t)
        outs.append(a)
    ar = jnp.concatenate([a[:n] for a in outs], axis=0)
    ai = jnp.concatenate([a[n:] for a in outs], axis=0)
    ys = []
    for g in range(FN_GROUPS):
        sl = slice(g * FN_CH, (g + 1) * FN_CH)
        z = jnp.concatenate([ar[:, sl], ai[:, sl]], axis=1)
        spec = _dot3_right(z, ch_ref[...], cl_ref[...])
        ys.append(_dot(spec.astype(BF16), w_ref[g]).astype(BF16))
    y = jnp.concatenate(ys, axis=1)
    y = _dot(p_ref[...], y).astype(BF16)
    o_ref[...] = y.reshape(n, FN_K2, FN_WIDTH)


def _fourier_latent(of, w_fnet, batch, seq):
    n = GRID_W
    assert seq == n * n
    cols = n * FN_WIDTH
    c64, s64 = _dft_consts(n)
    f1h, f1l = _hi_lo(np.concatenate([c64, -s64], axis=0))
    tw = 2.0 * np.pi * (np.arange(n)[:, None] * np.arange(n)[None, :]) / (n * n)
    tc = jnp.asarray(np.repeat(np.cos(tw).reshape(n, n, 1), FN_WIDTH, axis=2).reshape(n, cols), F32)
    ts = jnp.asarray(np.repeat(np.sin(tw).reshape(n, n, 1), FN_WIDTH, axis=2).reshape(n, cols), F32)
    x2 = of[:batch * seq].reshape(batch, n, cols)
    cb = 4096
    t = pl.pallas_call(
        _fn_stage1_kernel,
        grid=(batch, cols // cb),
        in_specs=[
            pl.BlockSpec((None, n, cb), lambda b, j: (b, 0, j)),
            pl.BlockSpec(f1h.shape, lambda b, j: (0, 0)),
            pl.BlockSpec(f1l.shape, lambda b, j: (0, 0)),
            pl.BlockSpec((n, cb), lambda b, j: (0, j)),
            pl.BlockSpec((n, cb), lambda b, j: (0, j)),
        ],
        out_specs=pl.BlockSpec((None, 2, n, cb), lambda b, j: (b, 0, 0, j)),
        out_shape=jax.ShapeDtypeStruct((batch, 2, n, cols), F32),
        compiler_params=_params("arbitrary", "arbitrary"),
        name="fourier_rows",
    )(x2, f1h, f1l, tc, ts)
    t = t.reshape(batch, 2, n, n, FN_WIDTH)

    g2h, g2l = _hi_lo(np.block([[c64, s64], [-s64, c64]]))
    cc, sc = _dft_consts(FN_CH)
    norm = 1.0 / np.sqrt(seq * FN_CH)
    c4h, c4l = _hi_lo(np.concatenate([cc, sc], axis=0) * norm)
    rows = FN_K2 * n
    perm = np.zeros((rows, rows), np.float32)
    k1 = np.arange(n)[:, None]
    j = np.arange(FN_K2)[None, :]
    perm[(k1 * FN_K2 + j).ravel(), (j * n + k1).ravel()] = 1.0
    perm = jnp.asarray(perm, BF16)
    const = lambda a: pl.BlockSpec(a.shape, lambda b, i: (0,) * a.ndim)
    y = pl.pallas_call(
        _fn_stage2_kernel,
        grid=(batch, n // FN_K2),
        in_specs=[
            pl.BlockSpec((None, 2, FN_K2, n, FN_WIDTH), lambda b, i: (b, 0, i, 0, 0)),
            const(g2h), const(g2l), const(c4h), const(c4l), const(w_fnet), const(perm),
        ],
        out_specs=pl.BlockSpec((None, n, FN_K2, FN_WIDTH), lambda b, i: (b, 0, i, 0)),
        out_shape=jax.ShapeDtypeStruct((batch, n, n, FN_WIDTH), BF16),
        compiler_params=_params("arbitrary", "arbitrary"),
        name="fourier_cols_channels",
    )(t, g2h, g2l, c4h, c4l, w_fnet, perm)
    return y.reshape(batch * seq, FN_WIDTH)


def _fn_ctx_kernel(x_ref, fh_ref, fl_ref, ch_ref, cl_ref, w_ref, o_ref, *, n):
    a = _dot3_left(fh_ref[...], fl_ref[...], x_ref[...])
    ar, ai = a[:n], a[n:]
    ys = []
    for g in range(FN_GROUPS):
        sl = slice(g * FN_CH, (g + 1) * FN_CH)
        z = jnp.concatenate([ar[:, sl], ai[:, sl]], axis=1)
        spec = _dot3_right(z, ch_ref[...], cl_ref[...])
        ys.append(_dot(spec.astype(BF16), w_ref[g]).astype(BF16))
    o_ref[...] = jnp.concatenate(ys, axis=1)


def _fourier_context(of, w_fnet, batch, row0, n):
    c, s = _dft_consts(n)
    fh, fl = _hi_lo(np.concatenate([c, -s], axis=0))
    cc, sc = _dft_consts(FN_CH)
    c4h, c4l = _hi_lo(np.concatenate([cc, sc], axis=0) / np.sqrt(n * FN_CH))
    const = lambda a: pl.BlockSpec(a.shape, lambda b: (0,) * a.ndim)
    blk0 = row0 // n
    return pl.pallas_call(
        functools.partial(_fn_ctx_kernel, n=n),
        grid=(batch,),
        in_specs=[pl.BlockSpec((n, FN_WIDTH), lambda b: (blk0 + b, 0)),
                  const(fh), const(fl), const(c4h), const(c4l), const(w_fnet)],
        out_specs=pl.BlockSpec((n, FN_WIDTH), lambda b: (b, 0)),
        out_shape=jax.ShapeDtypeStruct((batch * n, FN_WIDTH), BF16),
        compiler_params=_params("arbitrary"),
        name="fourier_context",
    )(of, fh, fl, c4h, c4l, w_fnet)


def _proj_residual_kernel(*refs, n_in, lat_blocks, has_ctx, split_x, final_norm):
    lat = refs[:n_in]
    ctx = refs[n_in:2 * n_in] if has_ctx else ()
    rest = refs[(2 if has_ctx else 1) * n_in:]
    w_refs, rest = rest[:n_in], rest[n_in:]
    x_ref, xc_ref = (rest[0], rest[1]) if split_x else (rest[0], rest[0])
    gt_ref, *gf_ref, o_ref = rest[2 if split_x else 1:]

    def run(a_refs, res_ref):
        for n in range(0, D_MODEL, PROJ_COLS):
            cols = slice(n, n + PROJ_COLS)
            acc = None
            for a_ref, w_ref in zip(a_refs, w_refs):
                t = _dot(a_ref[...], w_ref[:, cols])
                acc = t if acc is None else acc + t
            o_ref[:, cols] = res_ref[:, cols] + gt_ref[:, cols] * acc

    if has_ctx:
        i = pl.program_id(0)
        pl.when(i < lat_blocks)(lambda: run(lat, x_ref))
        pl.when(i >= lat_blocks)(lambda: run(ctx, xc_ref))
    else:
        run(lat, x_ref)
    if final_norm:
        o_ref[...] = _rms(o_ref[...], gf_ref[0][...])


def _proj_residual(xa, lat_ins, ctx_ins, w, mod_l, gate_k, rows_per_batch, n_rows, bm, final_gain=None, x_ctx=None):
    n_in = len(lat_ins)
    has_ctx = ctx_ins is not None
    split_x = x_ctx is not None
    lat_blocks = lat_ins[0].shape[0] // bm
    bpb = rows_per_batch // bm
    in_specs = [pl.BlockSpec((bm, a.shape[1]), lambda i: (jnp.minimum(i, lat_blocks - 1), 0)) for a in lat_ins]
    if has_ctx:
        in_specs += [pl.BlockSpec((bm, a.shape[1]), lambda i: (jnp.maximum(i - lat_blocks, 0), 0)) for a in ctx_ins]
    offs = np.cumsum([0] + [a.shape[1] for a in lat_ins])
    assert offs[-1] == w.shape[0] and all(o % a.shape[1] == 0 for o, a in zip(offs, lat_ins))
    in_specs += [pl.BlockSpec((a.shape[1], D_MODEL), lambda i, k=int(o) // a.shape[1]: (k, 0),
                              pipeline_mode=pl.Buffered(1)) for o, a in zip(offs, lat_ins)]
    x_index = len(in_specs)
    if split_x:
        assert has_ctx and xa.shape[0] == lat_blocks * bm
        in_specs += _two_source_specs(bm, D_MODEL, lat_blocks, 0)
    else:
        in_specs.append(pl.BlockSpec((bm, D_MODEL), lambda i: (i, 0)))
    in_specs.append(pl.BlockSpec((None, 1, D_MODEL), lambda i: (i // bpb, 0, gate_k)))
    args = list(lat_ins) + (list(ctx_ins) if has_ctx else []) + [w] * n_in + [xa] + ([x_ctx] if split_x else [])
    args.append(mod_l)
    final_norm = final_gain is not None
    if final_norm:
        in_specs.append(pl.BlockSpec((1, D_MODEL), lambda i: (0, 0)))
        args.append(final_gain)
    in_place = not (final_norm or split_x)
    return pl.pallas_call(
        functools.partial(_proj_residual_kernel, n_in=n_in, lat_blocks=lat_blocks, has_ctx=has_ctx,
                          split_x=split_x, final_norm=final_norm),
        grid=(n_rows // bm,),
        in_specs=in_specs,
        out_specs=pl.BlockSpec((bm, D_MODEL), lambda i: (i, 0)),
        out_shape=jax.ShapeDtypeStruct(xa.shape if in_place else (n_rows, D_MODEL), F32),
        input_output_aliases={x_index: 0} if in_place else {},
        compiler_params=_params("arbitrary"),
        name="projection_residual",
    )(*args)


def _ffn_up_kernel(xm_ref, xp_ref, xn_ref, sh_ref, sc_ref, g_ref, wg_ref, wv_ref, cwg_ref, cwv_ref, cbg_ref,
                   cbv_ref, o_ref, h_ref, *u_refs, lat_blocks, seq, ctx_len):
    bm = o_ref.shape[0]
    i = pl.program_id(0)

    @pl.when(pl.program_id(1) == 0)
    def _():
        g, sc, sh = g_ref[...], sc_ref[...], sh_ref[...]
        h_ref[0:HALO, :] = _normmod(xp_ref[...], g, sc, sh).astype(BF16)
        h_ref[HALO:HALO + bm, :] = _normmod(xm_ref[...], g, sc, sh).astype(BF16)
        h_ref[HALO + bm:, :] = _normmod(xn_ref[...], g, sc, sh).astype(BF16)

    t = FFN_TILE
    n_tiles = o_ref.shape[1] // t

    def pair(g_ref, v_ref, rows, k):
        cols = slice(t * k, t * (k + 1))
        return jnp.concatenate([g_ref[rows, cols], v_ref[rows, cols]], axis=1)

    def matmul(k):
        u_refs[k % 2][...] = _dot(h_ref[...], pair(wg_ref, wv_ref, slice(None), k))

    def run(interior_boundaries):
        if interior_boundaries:
            rows = lax.broadcasted_iota(jnp.int32, (bm, 1), 0)
            first = (rows & (ctx_len - 1)) == 0
            last = ((rows + 1) & (ctx_len - 1)) == 0
        else:
            starts = (i * bm) % seq == 0
            ends = ((i + 1) * bm) % seq == 0
        matmul(0)
        for k in range(n_tiles):
            if k + 1 < n_tiles:
                matmul(k + 1)
            src = u_refs[k % 2]
            if interior_boundaries:
                prev = jnp.where(first, 0.0, src[HALO - 1:HALO - 1 + bm, :])
                nxt = jnp.where(last, 0.0, src[HALO + 1:HALO + 1 + bm, :])
            else:
                src[HALO - 1:HALO, :] = jnp.where(starts, 0.0, src[HALO - 1:HALO, :])
                src[HALO + bm:HALO + bm + 1, :] = jnp.where(ends, 0.0, src[HALO + bm:HALO + bm + 1, :])
                prev = src[HALO - 1:HALO - 1 + bm, :]
                nxt = src[HALO + 1:HALO + 1 + bm, :]
            c = (prev * pair(cwg_ref, cwv_ref, slice(0, 1), k)
                 + src[HALO:HALO + bm, :] * pair(cwg_ref, cwv_ref, slice(1, 2), k)
                 + nxt * pair(cwg_ref, cwv_ref, slice(2, 3), k) + pair(cbg_ref, cbv_ref, slice(None), k))
            gate, val = c[:, :t], c[:, t:]
            o_ref[:, t * k:t * (k + 1)] = (gate * _sigmoid(gate) * val).astype(BF16)

    pl.when(i < lat_blocks)(lambda: run(False))
    pl.when(i >= lat_blocks)(lambda: run(True))


def _ffn_up(xa, mod_l, g, w_up, conv_w, conv_b, n_rows, n_lat, seq, ctx_len):
    rows = xa.shape[0]
    bm, bn = FFN_ROWS, 512
    assert seq % bm == 0 and bm % ctx_len == 0 and n_lat % bm == 0 and n_rows % bm == 0
    blocks_per_batch, n_blocks, lat_blocks = seq // bm, n_rows // bm, n_lat // bm
    nj = D_FF // bn
    hb = bm // HALO
    last_halo = rows // HALO - 1
    assert seq & (seq - 1) == 0 and ctx_len & (ctx_len - 1) == 0
    return pl.pallas_call(
        functools.partial(_ffn_up_kernel, lat_blocks=lat_blocks, seq=seq, ctx_len=ctx_len),
        grid=(n_blocks, nj),
        in_specs=[
            pl.BlockSpec((bm, D_MODEL), lambda i, j: (i, 0)),
            pl.BlockSpec((HALO, D_MODEL), lambda i, j: (jnp.maximum(i * hb - 1, 0), 0)),
            pl.BlockSpec((HALO, D_MODEL), lambda i, j: (jnp.minimum((i + 1) * hb, last_halo), 0)),
            _mod_spec(3, blocks_per_batch),
            _mod_spec(4, blocks_per_batch),
            pl.BlockSpec((1, D_MODEL), lambda i, j: (0, 0)),
            pl.BlockSpec((D_MODEL, bn), lambda i, j: (0, j)),
            pl.BlockSpec((D_MODEL, bn), lambda i, j: (0, nj + j)),
            pl.BlockSpec((CONV_W, bn), lambda i, j: (0, j)),
            pl.BlockSpec((CONV_W, bn), lambda i, j: (0, nj + j)),
            pl.BlockSpec((1, bn), lambda i, j: (0, j)),
            pl.BlockSpec((1, bn), lambda i, j: (0, nj + j)),
        ],
        out_specs=pl.BlockSpec((bm, bn), lambda i, j: (i, j)),
        out_shape=jax.ShapeDtypeStruct((n_blocks * bm, D_FF), BF16),
        scratch_shapes=[pltpu.VMEM((bm + 2 * HALO, D_MODEL), BF16)]
        + [pltpu.VMEM((bm + 2 * HALO, 2 * FFN_TILE), F32)] * 2,
        compiler_params=_params("arbitrary", "arbitrary"),
        name="ffn_up_conv_gate",
    )(xa, xa, xa, mod_l, mod_l, g, w_up, w_up, conv_w, conv_w, conv_b, conv_b)


def _rope_table(seq, pad_rows):
    n = MLA_ROPE // 4
    freqs = ROPE_BASE ** (-jnp.arange(n, dtype=F32) / n)
    pos = jnp.arange(seq)
    ang_r = (pos // GRID_W).astype(F32)[:, None] * freqs
    ang_c = (pos % GRID_W).astype(F32)[:, None] * freqs
    cos = jnp.concatenate([jnp.cos(ang_r)] * 2 + [jnp.cos(ang_c)] * 2, axis=1)
    sin = jnp.concatenate([-jnp.sin(ang_r), jnp.sin(ang_r), -jnp.sin(ang_c), jnp.sin(ang_c)], axis=1)
    lat = jnp.concatenate([cos, sin], axis=1)
    ident = jnp.concatenate([jnp.ones((pad_rows, MLA_ROPE), F32), jnp.zeros((pad_rows, MLA_ROPE), F32)], axis=1)
    return jnp.concatenate([lat, ident], axis=0)


def _partner_perm():
    q = MLA_ROPE // 4
    return np.concatenate([np.arange(q, 2 * q), np.arange(0, q), np.arange(3 * q, 4 * q), np.arange(2 * q, 3 * q)])


def _layout_w_in(w):
    cq, ckv, kr, qn, kn, vn, f = jnp.split(w, np.cumsum(
        (MLA_Q_RANK, MLA_KV_RANK, MLA_ROPE, NA_WIDTH, NA_WIDTH, NA_WIDTH))[:].tolist(), axis=1)
    return jnp.concatenate([cq, ckv, qn, kn, vn, f, kr, kr[:, _partner_perm()]], axis=1).astype(BF16)


def _layout_w_uq(w):
    w = w.reshape(MLA_Q_RANK, MLA_HEADS, MLA_NOPE + MLA_ROPE)
    rope = w[:, :, MLA_NOPE:]
    return jnp.concatenate([w, rope[:, :, _partner_perm()]], axis=2).reshape(MLA_Q_RANK, -1).astype(BF16)


def _layout_w_ukv(w):
    w = w.reshape(MLA_KV_RANK, MLA_HEADS, MLA_NOPE + MLA_V)
    return jnp.concatenate([w[:, :, :MLA_NOPE].reshape(MLA_KV_RANK, -1),
                            w[:, :, MLA_NOPE:].reshape(MLA_KV_RANK, -1)], axis=1).astype(BF16)


def kernel(x, c, ctx, c_ctx, w_mod, b_mod, g_attn, g_ffn, w_in, g_q, w_uq, g_kv, w_ukv, na_rpb, w_fnet, w_out,
           w_up, conv_w, conv_b, w_down, g_final):
    batch, seq, d = x.shape
    ctx_len = ctx.shape[1]
    depth = w_mod.shape[0]
    n_lat, n_ctx = batch * seq, batch * ctx_len
    bm = ROW_BLOCK
    lat_blocks = n_lat // bm
    all_blocks = (n_lat + n_ctx) // bm
    blocks_per_batch = seq // bm
    assert n_ctx % bm == 0 and n_lat // bm // blocks_per_batch == batch and batch < 8

    x_lat, x_ctx, ctx_block0 = x.reshape(n_lat, d), ctx.reshape(n_ctx, d), 0
    cin =jnp.zeros((8, d), F32).at[:batch].set(c).at[batch].set(c_ctx)
    mod = _modulation(cin, w_mod, b_mod)
    cs_tab = _rope_table(seq, bm)
    row = lambda v: v.reshape(1, -1)

    for l in range(depth):
        ctx_out = l < depth - 1
        mod_l = mod[l].reshape(8, 1, 6 * d)
        oc, on, of, okr = _inproj(x_lat, x_ctx, ctx_block0, n_lat + n_ctx, lat_blocks, mod_l, row(g_attn[l]),
                                  _layout_w_in(w_in[l]), blocks_per_batch)
        qm, km, vm = _mla_up(oc, okr, cs_tab, row(g_q[l]), row(g_kv[l]), _layout_w_uq(w_uq[l]),
                             _layout_w_ukv(w_ukv[l]), lat_blocks, blocks_per_batch)

        o_mla = _attention(qm, km, vm, batch=batch, heads=MLA_HEADS, dq=MLA_QK_PAD, dv=MLA_V, bq=2048, ck=1024,
                           seq=seq, ctx_len=ctx_len, k_col0=0, v_col0=0, name="mla_attention")
        o_na = _neighborhood_attention(on, _na_table(na_rpb[l], seq // GRID_W), batch, seq, ctx_len)
        o_fn = _fourier_latent(of, w_fnet[l].astype(BF16), batch, seq)
        ctx_ins = None
        if ctx_out:
            o_mla_c = _context_attention(
                qm, km, vm, batch=batch, heads=MLA_HEADS, dq=MLA_QK_PAD, dv=MLA_V, row0=n_lat, ctx_len=ctx_len,
                k_col0=0, v_col0=0, name="mla_attention_context")
            o_na_c = _context_attention(
                on, on, on, batch=batch, heads=NA_HEADS, dq=NA_HEAD_DIM, dv=NA_HEAD_DIM, row0=n_lat,
                ctx_len=ctx_len, k_col0=NA_HEADS, v_col0=2 * NA_HEADS, name="na_attention_context")
            o_fn_c = _fourier_context(of, w_fnet[l].astype(BF16), batch, n_lat, ctx_len)
            ctx_ins = (o_mla_c, o_na_c, o_fn_c)

        n_blocks = all_blocks if ctx_out else lat_blocks
        n_rows = n_blocks * bm
        unified = x_ctx is x_lat
        merge = ctx_out and not unified
        xa = _proj_residual(x_lat, (o_mla, o_na, o_fn), ctx_ins, w_out[l].astype(BF16), mod_l, 2, seq, n_rows, bm,
                            x_ctx=x_ctx if merge else None)
        a = _ffn_up(xa, mod_l, row(g_ffn[l]), w_up[l].astype(BF16), conv_w[l], row(conv_b[l]),
                    n_rows, n_lat, seq, ctx_len)
        xa = _proj_residual(xa, (a,), None, w_down[l].astype(BF16), mod_l, 5, seq, n_rows, bm,
                            final_gain=None if ctx_out else row(g_final))
        if merge or unified:
            x_lat, x_ctx, ctx_block0 = xa, xa, lat_blocks
        else:
            x_lat = xa

    return xa.reshape(batch, seq, d)
```

```python
import functools

import numpy as np
import jax
import jax.numpy as jnp
from jax import lax
from jax.experimental import pallas as pl
from jax.experimental.pallas import tpu as pltpu

F32 = jnp.float32
BF16 = jnp.bfloat16

D_MODEL = 2048
GRID_W = 64
EPS = 1e-6

MLA_HEADS = 8
MLA_NOPE = 128
MLA_ROPE = 64
MLA_V = 128
MLA_Q_RANK = 512
MLA_KV_RANK = 512
MLA_QK_PAD = 256
MLA_SCALE = (MLA_NOPE + MLA_ROPE) ** -0.5
ROPE_BASE = 10000.0

NA_HEADS = 4
NA_HEAD_DIM = 128
NA_KH = 8
NA_KW = 16
NA_SCALE = NA_HEAD_DIM ** -0.5
NA_QROWS = 8
NA_KROWS = NA_QROWS + NA_KH - 1

FN_GROUPS = 4
FN_CH = 128
FN_WIDTH = FN_GROUPS * FN_CH
FN_K2 = 16

MLA_WIDTH = MLA_HEADS * MLA_V
NA_WIDTH = NA_HEADS * NA_HEAD_DIM
D_FF = 5632
FFN_TILE = 128
CONV_W = 3

ROW_BLOCK = 512
FFN_ROWS = 1024
FFN_COLS = 1408
HALO = 16
PROJ_COLS = 512
MASK_VALUE = -1e30
LOG2E = 1.4426950408889634
VMEM_LIMIT = 60 * 1024 * 1024


def _params(*sem, flags=None):
    return pltpu.CompilerParams(dimension_semantics=sem, vmem_limit_bytes=VMEM_LIMIT, flags=flags)


def _dot(a, b):
    return jnp.dot(a, b, preferred_element_type=F32)


def _dot_nt(a, b):
    return lax.dot_general(a, b, (((1,), (1,)), ((), ())), preferred_element_type=F32)


def _split(x):
    hi = x.astype(BF16)
    lo = (x - hi.astype(F32)).astype(BF16)
    return hi, lo


def _dot3_left(m_hi, m_lo, x):
    x_hi, x_lo = _split(x)
    return _dot(m_hi, x_hi) + (_dot(m_hi, x_lo) + _dot(m_lo, x_hi))


def _dot3_right(x, m_hi, m_lo):
    x_hi, x_lo = _split(x)
    return _dot(x_hi, m_hi) + (_dot(x_lo, m_hi) + _dot(x_hi, m_lo))


def _sigmoid(x):
    return 1.0 / (1.0 + jnp.exp(-x))


def _rms(x, g):
    y = x * lax.rsqrt(jnp.mean(x * x, axis=-1, keepdims=True) + EPS)
    return y * g


def _normmod(x, g, sc, sh):
    return _rms(x, g) * (1.0 + sc) + sh


def _mod_kernel(c_ref, w_ref, b_ref, o_ref):
    c = c_ref[...]
    s = (c * _sigmoid(c)).astype(BF16)
    o_ref[...] = _dot(s, w_ref[...].astype(BF16)) + b_ref[...]


def _modulation(cin, w_mod, b_mod):
    depth, d, n = w_mod.shape
    bn = 1024
    return pl.pallas_call(
        _mod_kernel,
        grid=(depth, n // bn),
        in_specs=[
            pl.BlockSpec((8, d), lambda l, j: (0, 0)),
            pl.BlockSpec((None, d, bn), lambda l, j: (l, 0, j)),
            pl.BlockSpec((None, 1, bn), lambda l, j: (l, 0, j)),
        ],
        out_specs=pl.BlockSpec((None, 8, bn), lambda l, j: (l, 0, j)),
        out_shape=jax.ShapeDtypeStruct((depth, 8, n), F32),
        compiler_params=_params("arbitrary", "arbitrary"),
        name="modulation",
    )(cin, w_mod, b_mod.reshape(depth, 1, n))


def _inproj_kernel(x_ref, xc_ref, sh_ref, sc_ref, g_ref, w_ref, oc_ref, on_ref, of_ref, okr_ref, *, lat_blocks):
    def run(src_ref):
        h = _normmod(src_ref[...], g_ref[...], sc_ref[...], sh_ref[...]).astype(BF16)
        oc_ref[...] = _dot(h, w_ref[:, 0:1024])
        on_ref[:, 0:NA_WIDTH] = (_dot(h, w_ref[:, 1024:1024 + NA_WIDTH]) * (NA_SCALE * LOG2E)).astype(BF16)
        on_ref[:, NA_WIDTH:] = _dot(h, w_ref[:, 1024 + NA_WIDTH:2560]).astype(BF16)
        of_ref[...] = _dot(h, w_ref[:, 2560:3072])
        okr_ref[...] = _dot(h, w_ref[:, 3072:3200])

    i = pl.program_id(0)
    pl.when(i < lat_blocks)(lambda: run(x_ref))
    pl.when(i >= lat_blocks)(lambda: run(xc_ref))


def _mod_spec(k, blocks_per_batch):
    return pl.BlockSpec((None, 1, D_MODEL), lambda i, *_: (i // blocks_per_batch, 0, k))


def _two_source_specs(bm, width, lat_blocks, ctx_block0):
    return [pl.BlockSpec((bm, width), lambda i, *_: (jnp.minimum(i, lat_blocks - 1), 0)),
            pl.BlockSpec((bm, width), lambda i, *_: (ctx_block0 + jnp.maximum(i - lat_blocks, 0), 0))]


def _inproj(x_lat, x_ctx, ctx_block0, rows, lat_blocks, mod_l, g, w_p, blocks_per_batch):
    bm = ROW_BLOCK
    row = lambda w: pl.BlockSpec((bm, w), lambda i: (i, 0))
    return pl.pallas_call(
        functools.partial(_inproj_kernel, lat_blocks=lat_blocks),
        grid=(rows // bm,),
        in_specs=_two_source_specs(bm, D_MODEL, lat_blocks, ctx_block0) + [
            _mod_spec(0, blocks_per_batch),
            _mod_spec(1, blocks_per_batch),
            pl.BlockSpec((1, D_MODEL), lambda i: (0, 0)),
            pl.BlockSpec(w_p.shape, lambda i: (0, 0), pipeline_mode=pl.Buffered(1)),
        ],
        out_specs=[row(1024), row(1536), row(512), row(128)],
        out_shape=[
            jax.ShapeDtypeStruct((rows, 1024), F32),
            jax.ShapeDtypeStruct((rows, 1536), BF16),
            jax.ShapeDtypeStruct((rows, 512), F32),
            jax.ShapeDtypeStruct((rows, 128), F32),
        ],
        compiler_params=_params("arbitrary"),
        name="in_projection",
    )(x_lat, x_ctx, mod_l, mod_l, g, w_p)


def _rope(t, cs):
    t = t * cs
    return t + pltpu.roll(t, 64, axis=1)


def _mla_up_kernel(c_ref, kr_ref, cs_ref, gq_ref, gkv_ref, wq_ref, wkv_ref, q_ref, k_ref, v_ref):
    cs = cs_ref[...]
    cq = _rms(c_ref[:, 0:MLA_Q_RANK], gq_ref[...]).astype(BF16)
    ckv = _rms(c_ref[:, MLA_Q_RANK:MLA_Q_RANK + MLA_KV_RANK], gkv_ref[...]).astype(BF16)
    q = _dot(cq, wq_ref[...]) * (MLA_SCALE * LOG2E)
    kv = _dot(ckv, wkv_ref[...])
    lane = lax.broadcasted_iota(jnp.int32, cs.shape, 1)
    k_rope = jnp.where(lane < MLA_ROPE, _rope(kr_ref[...], cs), 0.0).astype(BF16)
    for h in range(MLA_HEADS):
        o = h * MLA_QK_PAD
        q_ref[:, o:o + MLA_NOPE] = q[:, o:o + MLA_NOPE].astype(BF16)
        q_ref[:, o + MLA_NOPE:o + MLA_QK_PAD] = _rope(q[:, o + MLA_NOPE:o + MLA_QK_PAD], cs).astype(BF16)
        k_ref[:, o:o + MLA_NOPE] = kv[:, h * MLA_NOPE:(h + 1) * MLA_NOPE].astype(BF16)
        k_ref[:, o + MLA_NOPE:o + MLA_QK_PAD] = k_rope
    v_ref[...] = kv[:, MLA_HEADS * MLA_NOPE:].astype(BF16)


def _mla_up(oc, okr, cs_tab, g_q, g_kv, wq_p, wkv_p, lat_blocks, pos_blocks):
    rows = oc.shape[0]
    bm = ROW_BLOCK
    row = lambda w: pl.BlockSpec((bm, w), lambda i: (i, 0))
    const = lambda a: pl.BlockSpec(a.shape, lambda i: (0, 0))
    cs_spec = pl.BlockSpec((bm, 128), lambda i: (jnp.where(i < lat_blocks, i % pos_blocks, pos_blocks), 0))
    qk_w = MLA_HEADS * MLA_QK_PAD
    return pl.pallas_call(
        _mla_up_kernel,
        grid=(rows // bm,),
        in_specs=[row(1024), row(128), cs_spec, const(g_q), const(g_kv), const(wq_p), const(wkv_p)],
        out_specs=[row(qk_w), row(qk_w), row(MLA_WIDTH)],
        out_shape=[
            jax.ShapeDtypeStruct((rows, qk_w), BF16),
            jax.ShapeDtypeStruct((rows, qk_w), BF16),
            jax.ShapeDtypeStruct((rows, MLA_WIDTH), BF16),
        ],
        compiler_params=_params("arbitrary"),
        name="mla_up_projection",
    )(oc, okr, cs_tab, g_q, g_kv, wq_p, wkv_p)


def _with_ones(v):
    return jnp.concatenate([v, jnp.ones_like(v)], axis=1)


def _softmax_first(s, v):
    m = jnp.max(s, axis=-1, keepdims=True)
    return m, _dot(jnp.exp2(s - m).astype(BF16), _with_ones(v))


def _softmax_next(s, s_max, v, m, acc):
    m_new = jnp.maximum(m, s_max)
    p = jnp.exp2(s - m_new)
    return m_new, jnp.exp2(m - m_new) * acc + _dot(p.astype(BF16), _with_ones(v))


def _normalised(acc):
    dv = acc.shape[1] // 2
    return acc[:, :dv] / acc[:, dv:]


def _attention_kernel(q_ref, k_ref, v_ref, kc_ref, vc_ref, o_ref, sa_ref, sb_ref, *, ck, n_main):
    q = q_ref[...]
    carry = _softmax_first(_dot_nt(q, kc_ref[...]), vc_ref[...])
    bufs = (sa_ref, sb_ref)

    def scores(c):
        s = _dot_nt(q, k_ref[c * ck:(c + 1) * ck, :])
        bufs[c % 2][...] = s
        return jnp.max(s, axis=-1, keepdims=True)

    s_max = scores(0)
    for c in range(n_main):
        nxt_max = scores(c + 1) if c + 1 < n_main else None
        carry = _softmax_next(bufs[c % 2][...], s_max, v_ref[c * ck:(c + 1) * ck, :], *carry)
        s_max = nxt_max
    o_ref[...] = _normalised(carry[1]).astype(o_ref.dtype)


def _attention(q_arr, k_arr, v_arr, *, batch, heads, dq, dv, bq, seq, ctx_len, k_col0, v_col0, name, ck=512):
    nq = seq // bq
    ctx0 = batch * seq // ctx_len
    return pl.pallas_call(
        functools.partial(_attention_kernel, ck=ck, n_main=seq // ck),
        grid=(batch, heads, nq),
        in_specs=[
            pl.BlockSpec((bq, dq), lambda b, h, i: (b * nq + i, h)),
            pl.BlockSpec((seq, dq), lambda b, h, i: (b, k_col0 + h)),
            pl.BlockSpec((seq, dv), lambda b, h, i: (b, v_col0 + h)),
            pl.BlockSpec((ctx_len, dq), lambda b, h, i: (ctx0 + b, k_col0 + h)),
            pl.BlockSpec((ctx_len, dv), lambda b, h, i: (ctx0 + b, v_col0 + h)),
        ],
        out_specs=pl.BlockSpec((bq, dv), lambda b, h, i: (b * nq + i, h)),
        out_shape=jax.ShapeDtypeStruct((batch * seq, heads * dv), BF16),
        scratch_shapes=[pltpu.VMEM((bq, ck), F32), pltpu.VMEM((bq, ck), F32)],
        compiler_params=_params("arbitrary", "arbitrary", "arbitrary"),
        name=name,
    )(q_arr, k_arr, v_arr, k_arr, v_arr)


def _context_attention_kernel(q_ref, k_ref, v_ref, o_ref):
    _, acc = _softmax_first(_dot_nt(q_ref[...], k_ref[...]), v_ref[...])
    o_ref[...] = _normalised(acc).astype(o_ref.dtype)


def _context_attention(q_arr, k_arr, v_arr, *, batch, heads, dq, dv, row0, ctx_len, k_col0, v_col0, name):
    blk0 = row0 // ctx_len
    return pl.pallas_call(
        _context_attention_kernel,
        grid=(batch, heads),
        in_specs=[
            pl.BlockSpec((ctx_len, dq), lambda b, h: (blk0 + b, h)),
            pl.BlockSpec((ctx_len, dq), lambda b, h: (blk0 + b, k_col0 + h)),
            pl.BlockSpec((ctx_len, dv), lambda b, h: (blk0 + b, v_col0 + h)),
        ],
        out_specs=pl.BlockSpec((ctx_len, dv), lambda b, h: (b, h)),
        out_shape=jax.ShapeDtypeStruct((batch * ctx_len, heads * dv), BF16),
        compiler_params=_params("arbitrary", "arbitrary"),
        name=name,
    )(q_arr, k_arr, v_arr)


def _na_kernel(q_ref, k_ref, v_ref, kc_ref, vc_ref, ta_ref, tb_ref, o_ref, *s_refs, grid_rows):
    bq = NA_QROWS * GRID_W
    nwin = NA_KROWS * GRID_W

    def scores(half, t_ref, sw_ref, sc_ref):
        r = pl.program_id(2) * 2 + half
        ks = jnp.clip(r * NA_QROWS - NA_KH // 2, 0, grid_rows - NA_KROWS)
        keys = pl.ds(pl.multiple_of(ks * GRID_W, GRID_W), nwin)
        q = q_ref[half * bq:(half + 1) * bq, :]
        s_win = _dot_nt(q, k_ref[keys, :]) + t_ref[...]
        s_ctx = _dot_nt(q, kc_ref[...])
        sw_ref[...] = s_win
        sc_ref[...] = s_ctx
        return keys, jnp.maximum(jnp.max(s_win, axis=-1, keepdims=True), jnp.max(s_ctx, axis=-1, keepdims=True))

    def finish(half, keys, m, sw_ref, sc_ref):
        p_win = jnp.exp2(sw_ref[...] - m).astype(BF16)
        p_ctx = jnp.exp2(sc_ref[...] - m).astype(BF16)
        acc = _dot(p_win, _with_ones(v_ref[keys, :])) + _dot(p_ctx, _with_ones(vc_ref[...]))
        o_ref[half * bq:(half + 1) * bq, :] = _normalised(acc).astype(o_ref.dtype)

    first = scores(0, ta_ref, *s_refs[0:2])
    second = scores(1, tb_ref, *s_refs[2:4])
    finish(0, *first, *s_refs[0:2])
    finish(1, *second, *s_refs[2:4])


def _na_table(rpb, grid_rows):
    tabs = []
    n_rel_r, n_rel_c = 2 * NA_KH - 1, 2 * NA_KW - 1
    nwin = NA_KROWS * GRID_W
    jw = np.arange(GRID_W)[None, :] - np.arange(GRID_W)[:, None] + (NA_KW - 1)
    sel_c = (jw[:, :, None] == np.arange(n_rel_c)).astype(np.float32)
    by_col = jnp.einsum("hpq,wjq->hpwj", rpb.astype(F32), sel_c, precision=lax.Precision.HIGHEST)
    by_col = jnp.tile(by_col, (1, 1, 1, NA_KROWS))
    for r0 in (0, NA_QROWS, grid_rows - NA_QROWS):
        ks = min(max(r0 - NA_KH // 2, 0), grid_rows - NA_KROWS)
        r = r0 + np.arange(NA_QROWS)[:, None, None, None]
        w = np.arange(GRID_W)[None, :, None, None]
        kr = ks + np.arange(NA_KROWS)[None, None, :, None]
        j = np.arange(GRID_W)[None, None, None, :]
        rs = np.clip(r - NA_KH // 2, 0, grid_rows - NA_KH)
        cs = np.clip(w - NA_KW // 2, 0, GRID_W - NA_KW)
        valid = (kr >= rs) & (kr < rs + NA_KH) & (j >= cs) & (j < cs + NA_KW)
        sel_r = ((kr - r + (NA_KH - 1))[:, 0, :, 0, None] == np.arange(n_rel_r)).astype(np.float32)
        sel_r = np.repeat(sel_r, GRID_W, axis=1).transpose(2, 0, 1)
        bias = jnp.sum(sel_r[None, :, :, None, :] * by_col[:, :, None, :, :], axis=1)
        valid = np.broadcast_to(valid, (NA_QROWS, GRID_W, NA_KROWS, GRID_W)).reshape(NA_QROWS, GRID_W, nwin)
        tab = jnp.where(valid[None], bias * LOG2E, MASK_VALUE)
        tabs.append(tab.reshape(rpb.shape[0], NA_QROWS * GRID_W, nwin))
    return jnp.stack(tabs)


def _neighborhood_attention(on, table, batch, seq, ctx_len):
    grid_rows = seq // GRID_W
    nstep = grid_rows // (2 * NA_QROWS)
    bq = NA_QROWS * GRID_W
    nwin = NA_KROWS * GRID_W
    d = NA_HEAD_DIM
    ctx0 = batch * seq // ctx_len
    assert nstep >= 2
    return pl.pallas_call(
        functools.partial(_na_kernel, grid_rows=grid_rows),
        grid=(batch, NA_HEADS, nstep),
        in_specs=[
            pl.BlockSpec((2 * bq, d), lambda b, h, r: (b * nstep + r, h)),
            pl.BlockSpec((seq, d), lambda b, h, r: (b, NA_HEADS + h)),
            pl.BlockSpec((seq, d), lambda b, h, r: (b, 2 * NA_HEADS + h)),
            pl.BlockSpec((ctx_len, d), lambda b, h, r: (ctx0 + b, NA_HEADS + h)),
            pl.BlockSpec((ctx_len, d), lambda b, h, r: (ctx0 + b, 2 * NA_HEADS + h)),
            pl.BlockSpec((None, None, bq, nwin), lambda b, h, r: (jnp.where(r == 0, 0, 1), h, 0, 0)),
            pl.BlockSpec((None, None, bq, nwin), lambda b, h, r: (jnp.where(r == nstep - 1, 2, 1), h, 0, 0)),
        ],
        out_specs=pl.BlockSpec((2 * bq, d), lambda b, h, r: (b * nstep + r, h)),
        out_shape=jax.ShapeDtypeStruct((batch * seq, NA_WIDTH), BF16),
        scratch_shapes=[pltpu.VMEM((bq, nwin), F32), pltpu.VMEM((bq, ctx_len), F32)] * 2,
        compiler_params=_params("arbitrary", "arbitrary", "arbitrary"),
        name="neighborhood_attention",
    )(on, on, on, on, on, table, table)


def _dft_consts(n):
    jk = (np.arange(n)[:, None] * np.arange(n)[None, :]) % n
    ang = 2.0 * np.pi * jk / n
    return np.cos(ang), np.sin(ang)


def _hi_lo(m):
    m = jnp.asarray(m, F32)
    hi = m.astype(BF16)
    return hi, (m - hi.astype(F32)).astype(BF16)


def _fn_stage1_kernel(x_ref, fh_ref, fl_ref, tc_ref, ts_ref, o_ref):
    n = GRID_W
    b = _dot3_left(fh_ref[...], fl_ref[...], x_ref[...])
    br, bi = b[:n], b[n:]
    tc, ts = tc_ref[...], ts_ref[...]
    o_ref[0] = br * tc + bi * ts
    o_ref[1] = bi * tc - br * ts


def _fn_stage2_kernel(t_ref, gh_ref, gl_ref, ch_ref, cl_ref, w_ref, p_ref, o_ref):
    n = GRID_W
    outs = []
    for j in range(FN_K2):
        t = jnp.concatenate([t_ref[0, j], t_ref[1, j]], axis=0)
        a = _dot3_left(gh_ref[...], gl_ref[...], t)
        outs.append(a)
    ar = jnp.concatenate([a[:n] for a in outs], axis=0)
    ai = jnp.concatenate([a[n:] for a in outs], axis=0)
    ys = []
    for g in range(FN_GROUPS):
        sl = slice(g * FN_CH, (g + 1) * FN_CH)
        z = jnp.concatenate([ar[:, sl], ai[:, sl]], axis=1)
        spec = _dot3_right(z, ch_ref[...], cl_ref[...])
        ys.append(_dot(spec.astype(BF16), w_ref[g]).astype(BF16))
    y = jnp.concatenate(ys, axis=1)
    y = _dot(p_ref[...], y).astype(BF16)
    o_ref[...] = y.reshape(n, FN_K2, FN_WIDTH)


def _fourier_latent(of, w_fnet, batch, seq):
    n = GRID_W
    assert seq == n * n
    cols = n * FN_WIDTH
    c64, s64 = _dft_consts(n)
    f1h, f1l = _hi_lo(np.concatenate([c64, -s64], axis=0))
    tw = 2.0 * np.pi * (np.arange(n)[:, None] * np.arange(n)[None, :]) / (n * n)
    tc = jnp.asarray(np.repeat(np.cos(tw).reshape(n, n, 1), FN_WIDTH, axis=2).reshape(n, cols), F32)
    ts = jnp.asarray(np.repeat(np.sin(tw).reshape(n, n, 1), FN_WIDTH, axis=2).reshape(n, cols), F32)
    x2 = of[:batch * seq].reshape(batch, n, cols)
    cb = 4096
    t = pl.pallas_call(
        _fn_stage1_kernel,
        grid=(batch, cols // cb),
        in_specs=[
            pl.BlockSpec((None, n, cb), lambda b, j: (b, 0, j)),
            pl.BlockSpec(f1h.shape, lambda b, j: (0, 0)),
            pl.BlockSpec(f1l.shape, lambda b, j: (0, 0)),
            pl.BlockSpec((n, cb), lambda b, j: (0, j)),
            pl.BlockSpec((n, cb), lambda b, j: (0, j)),
        ],
        out_specs=pl.BlockSpec((None, 2, n, cb), lambda b, j: (b, 0, 0, j)),
        out_shape=jax.ShapeDtypeStruct((batch, 2, n, cols), F32),
        compiler_params=_params("arbitrary", "arbitrary"),
        name="fourier_rows",
    )(x2, f1h, f1l, tc, ts)
    t = t.reshape(batch, 2, n, n, FN_WIDTH)

    g2h, g2l = _hi_lo(np.block([[c64, s64], [-s64, c64]]))
    cc, sc = _dft_consts(FN_CH)
    norm = 1.0 / np.sqrt(seq * FN_CH)
    c4h, c4l = _hi_lo(np.concatenate([cc, sc], axis=0) * norm)
    rows = FN_K2 * n
    perm = np.zeros((rows, rows), np.float32)
    k1 = np.arange(n)[:, None]
    j = np.arange(FN_K2)[None, :]
    perm[(k1 * FN_K2 + j).ravel(), (j * n + k1).ravel()] = 1.0
    perm = jnp.asarray(perm, BF16)
    const = lambda a: pl.BlockSpec(a.shape, lambda b, i: (0,) * a.ndim)
    y = pl.pallas_call(
        _fn_stage2_kernel,
        grid=(batch, n // FN_K2),
        in_specs=[
            pl.BlockSpec((None, 2, FN_K2, n, FN_WIDTH), lambda b, i: (b, 0, i, 0, 0)),
            const(g2h), const(g2l), const(c4h), const(c4l), const(w_fnet), const(perm),
        ],
        out_specs=pl.BlockSpec((None, n, FN_K2, FN_WIDTH), lambda b, i: (b, 0, i, 0)),
        out_shape=jax.ShapeDtypeStruct((batch, n, n, FN_WIDTH), BF16),
        compiler_params=_params("arbitrary", "arbitrary"),
        name="fourier_cols_channels",
    )(t, g2h, g2l, c4h, c4l, w_fnet, perm)
    return y.reshape(batch * seq, FN_WIDTH)


def _fn_ctx_kernel(x_ref, fh_ref, fl_ref, ch_ref, cl_ref, w_ref, o_ref, *, n):
    a = _dot3_left(fh_ref[...], fl_ref[...], x_ref[...])
    ar, ai = a[:n], a[n:]
    ys = []
    for g in range(FN_GROUPS):
        sl = slice(g * FN_CH, (g + 1) * FN_CH)
        z = jnp.concatenate([ar[:, sl], ai[:, sl]], axis=1)
        spec = _dot3_right(z, ch_ref[...], cl_ref[...])
        ys.append(_dot(spec.astype(BF16), w_ref[g]).astype(BF16))
    o_ref[...] = jnp.concatenate(ys, axis=1)


def _fourier_context(of, w_fnet, batch, row0, n):
    c, s = _dft_consts(n)
    fh, fl = _hi_lo(np.concatenate([c, -s], axis=0))
    cc, sc = _dft_consts(FN_CH)
    c4h, c4l = _hi_lo(np.concatenate([cc, sc], axis=0) / np.sqrt(n * FN_CH))
    const = lambda a: pl.BlockSpec(a.shape, lambda b: (0,) * a.ndim)
    blk0 = row0 // n
    return pl.pallas_call(
        functools.partial(_fn_ctx_kernel, n=n),
        grid=(batch,),
        in_specs=[pl.BlockSpec((n, FN_WIDTH), lambda b: (blk0 + b, 0)),
                  const(fh), const(fl), const(c4h), const(c4l), const(w_fnet)],
        out_specs=pl.BlockSpec((n, FN_WIDTH), lambda b: (b, 0)),
        out_shape=jax.ShapeDtypeStruct((batch * n, FN_WIDTH), BF16),
        compiler_params=_params("arbitrary"),
        name="fourier_context",
    )(of, fh, fl, c4h, c4l, w_fnet)


def _proj_residual_kernel(*refs, n_in, lat_blocks, has_ctx, split_x, final_norm):
    lat = refs[:n_in]
    ctx = refs[n_in:2 * n_in] if has_ctx else ()
    rest = refs[(2 if has_ctx else 1) * n_in:]
    w_refs, rest = rest[:n_in], rest[n_in:]
    x_ref, xc_ref = (rest[0], rest[1]) if split_x else (rest[0], rest[0])
    gt_ref, *gf_ref, o_ref = rest[2 if split_x else 1:]

    def run(a_refs, res_ref):
        for n in range(0, D_MODEL, PROJ_COLS):
            cols = slice(n, n + PROJ_COLS)
            acc = None
            for a_ref, w_ref in zip(a_refs, w_refs):
                t = _dot(a_ref[...], w_ref[:, cols])
                acc = t if acc is None else acc + t
            o_ref[:, cols] = res_ref[:, cols] + gt_ref[:, cols] * acc

    if has_ctx:
        i = pl.program_id(0)
        pl.when(i < lat_blocks)(lambda: run(lat, x_ref))
        pl.when(i >= lat_blocks)(lambda: run(ctx, xc_ref))
    else:
        run(lat, x_ref)
    if final_norm:
        o_ref[...] = _rms(o_ref[...], gf_ref[0][...])


def _proj_residual(xa, lat_ins, ctx_ins, w, mod_l, gate_k, rows_per_batch, n_rows, bm, final_gain=None, x_ctx=None):
    n_in = len(lat_ins)
    has_ctx = ctx_ins is not None
    split_x = x_ctx is not None
    lat_blocks = lat_ins[0].shape[0] // bm
    bpb = rows_per_batch // bm
    in_specs = [pl.BlockSpec((bm, a.shape[1]), lambda i: (jnp.minimum(i, lat_blocks - 1), 0)) for a in lat_ins]
    if has_ctx:
        in_specs += [pl.BlockSpec((bm, a.shape[1]), lambda i: (jnp.maximum(i - lat_blocks, 0), 0)) for a in ctx_ins]
    offs = np.cumsum([0] + [a.shape[1] for a in lat_ins])
    assert offs[-1] == w.shape[0] and all(o % a.shape[1] == 0 for o, a in zip(offs, lat_ins))
    in_specs += [pl.BlockSpec((a.shape[1], D_MODEL), lambda i, k=int(o) // a.shape[1]: (k, 0),
                              pipeline_mode=pl.Buffered(1)) for o, a in zip(offs, lat_ins)]
    x_index = len(in_specs)
    if split_x:
        assert has_ctx and xa.shape[0] == lat_blocks * bm
        in_specs += _two_source_specs(bm, D_MODEL, lat_blocks, 0)
    else:
        in_specs.append(pl.BlockSpec((bm, D_MODEL), lambda i: (i, 0)))
    in_specs.append(pl.BlockSpec((None, 1, D_MODEL), lambda i: (i // bpb, 0, gate_k)))
    args = list(lat_ins) + (list(ctx_ins) if has_ctx else []) + [w] * n_in + [xa] + ([x_ctx] if split_x else [])
    args.append(mod_l)
    final_norm = final_gain is not None
    if final_norm:
        in_specs.append(pl.BlockSpec((1, D_MODEL), lambda i: (0, 0)))
        args.append(final_gain)
    in_place = not (final_norm or split_x)
    return pl.pallas_call(
        functools.partial(_proj_residual_kernel, n_in=n_in, lat_blocks=lat_blocks, has_ctx=has_ctx,
                          split_x=split_x, final_norm=final_norm),
        grid=(n_rows // bm,),
        in_specs=in_specs,
        out_specs=pl.BlockSpec((bm, D_MODEL), lambda i: (i, 0)),
        out_shape=jax.ShapeDtypeStruct(xa.shape if in_place else (n_rows, D_MODEL), F32),
        input_output_aliases={x_index: 0} if in_place else {},
        compiler_params=_params("arbitrary"),
        name="projection_residual",
    )(*args)


def _ffn_up_kernel(xm_ref, xp_ref, xn_ref, sh_ref, sc_ref, g_ref, wg_ref, wv_ref, cwg_ref, cwv_ref, cbg_ref,
                   cbv_ref, o_ref, h_ref, *u_refs, lat_blocks, seq, ctx_len):
    bm = o_ref.shape[0]
    i = pl.program_id(0)

    @pl.when(pl.program_id(1) == 0)
    def _():
        g, sc, sh = g_ref[...], sc_ref[...], sh_ref[...]
        h_ref[0:HALO, :] = _normmod(xp_ref[...], g, sc, sh).astype(BF16)
        h_ref[HALO:HALO + bm, :] = _normmod(xm_ref[...], g, sc, sh).astype(BF16)
        h_ref[HALO + bm:, :] = _normmod(xn_ref[...], g, sc, sh).astype(BF16)

    t = FFN_TILE
    n_tiles = o_ref.shape[1] // t

    def pair(g_ref, v_ref, rows, k):
        cols = slice(t * k, t * (k + 1))
        return jnp.concatenate([g_ref[rows, cols], v_ref[rows, cols]], axis=1)

    def matmul(k):
        u_refs[k % 2][...] = _dot(h_ref[...], pair(wg_ref, wv_ref, slice(None), k))

    def run(interior_boundaries):
        if interior_boundaries:
            rows = lax.broadcasted_iota(jnp.int32, (bm, 1), 0)
            first = (rows & (ctx_len - 1)) == 0
            last = ((rows + 1) & (ctx_len - 1)) == 0
        else:
            starts = (i * bm) % seq == 0
            ends = ((i + 1) * bm) % seq == 0
        matmul(0)
        for k in range(n_tiles):
            if k + 1 < n_tiles:
                matmul(k + 1)
            src = u_refs[k % 2]
            if interior_boundaries:
                prev = jnp.where(first, 0.0, src[HALO - 1:HALO - 1 + bm, :])
                nxt = jnp.where(last, 0.0, src[HALO + 1:HALO + 1 + bm, :])
            else:
                src[HALO - 1:HALO, :] = jnp.where(starts, 0.0, src[HALO - 1:HALO, :])
                src[HALO + bm:HALO + bm + 1, :] = jnp.where(ends, 0.0, src[HALO + bm:HALO + bm + 1, :])
                prev = src[HALO - 1:HALO - 1 + bm, :]
                nxt = src[HALO + 1:HALO + 1 + bm, :]
            c = (prev * pair(cwg_ref, cwv_ref, slice(0, 1), k)
                 + src[HALO:HALO + bm, :] * pair(cwg_ref, cwv_ref, slice(1, 2), k)
                 + nxt * pair(cwg_ref, cwv_ref, slice(2, 3), k) + pair(cbg_ref, cbv_ref, slice(None), k))
            gate, val = c[:, :t], c[:, t:]
            o_ref[:, t * k:t * (k + 1)] = (gate * _sigmoid(gate) * val).astype(BF16)

    pl.when(i < lat_blocks)(lambda: run(False))
    pl.when(i >= lat_blocks)(lambda: run(True))


def _ffn_up(xa, mod_l, g, w_up, conv_w, conv_b, n_rows, n_lat, seq, ctx_len):
    rows = xa.shape[0]
    bm, bn = FFN_ROWS, FFN_COLS
    assert seq % bm == 0 and bm % ctx_len == 0 and n_lat % bm == 0 and n_rows % bm == 0
    blocks_per_batch, n_blocks, lat_blocks = seq // bm, n_rows // bm, n_lat // bm
    nj = D_FF // bn
    hb = bm // HALO
    last_halo = rows // HALO - 1
    assert seq & (seq - 1) == 0 and ctx_len & (ctx_len - 1) == 0
    return pl.pallas_call(
        functools.partial(_ffn_up_kernel, lat_blocks=lat_blocks, seq=seq, ctx_len=ctx_len),
        grid=(n_blocks, nj),
        in_specs=[
            pl.BlockSpec((bm, D_MODEL), lambda i, j: (i, 0)),
            pl.BlockSpec((HALO, D_MODEL), lambda i, j: (jnp.maximum(i * hb - 1, 0), 0)),
            pl.BlockSpec((HALO, D_MODEL), lambda i, j: (jnp.minimum((i + 1) * hb, last_halo), 0)),
            _mod_spec(3, blocks_per_batch),
            _mod_spec(4, blocks_per_batch),
            pl.BlockSpec((1, D_MODEL), lambda i, j: (0, 0)),
            pl.BlockSpec((D_MODEL, bn), lambda i, j: (0, j)),
            pl.BlockSpec((D_MODEL, bn), lambda i, j: (0, nj + j)),
            pl.BlockSpec((CONV_W, bn), lambda i, j: (0, j)),
            pl.BlockSpec((CONV_W, bn), lambda i, j: (0, nj + j)),
            pl.BlockSpec((1, bn), lambda i, j: (0, j)),
            pl.BlockSpec((1, bn), lambda i, j: (0, nj + j)),
        ],
        out_specs=pl.BlockSpec((bm, bn), lambda i, j: (i, j)),
        out_shape=jax.ShapeDtypeStruct((n_blocks * bm, D_FF), BF16),
        scratch_shapes=[pltpu.VMEM((bm + 2 * HALO, D_MODEL), BF16)]
        + [pltpu.VMEM((bm + 2 * HALO, 2 * FFN_TILE), F32)] * 2,
        compiler_params=_params("arbitrary", "arbitrary"),
        name="ffn_up_conv_gate",
    )(xa, xa, xa, mod_l, mod_l, g, w_up, w_up, conv_w, conv_w, conv_b, conv_b)


def _rope_table(seq, pad_rows):
    n = MLA_ROPE // 4
    freqs = ROPE_BASE ** (-jnp.arange(n, dtype=F32) / n)
    pos = jnp.arange(seq)
    ang_r = (pos // GRID_W).astype(F32)[:, None] * freqs
    ang_c = (pos % GRID_W).astype(F32)[:, None] * freqs
    cos = jnp.concatenate([jnp.cos(ang_r)] * 2 + [jnp.cos(ang_c)] * 2, axis=1)
    sin = jnp.concatenate([-jnp.sin(ang_r), jnp.sin(ang_r), -jnp.sin(ang_c), jnp.sin(ang_c)], axis=1)
    lat = jnp.concatenate([cos, sin], axis=1)
    ident = jnp.concatenate([jnp.ones((pad_rows, MLA_ROPE), F32), jnp.zeros((pad_rows, MLA_ROPE), F32)], axis=1)
    return jnp.concatenate([lat, ident], axis=0)


def _partner_perm():
    q = MLA_ROPE // 4
    return np.concatenate([np.arange(q, 2 * q), np.arange(0, q), np.arange(3 * q, 4 * q), np.arange(2 * q, 3 * q)])


def _layout_w_in(w):
    cq, ckv, kr, qn, kn, vn, f = jnp.split(w, np.cumsum(
        (MLA_Q_RANK, MLA_KV_RANK, MLA_ROPE, NA_WIDTH, NA_WIDTH, NA_WIDTH))[:].tolist(), axis=1)
    return jnp.concatenate([cq, ckv, qn, kn, vn, f, kr, kr[:, _partner_perm()]], axis=1).astype(BF16)


def _layout_w_uq(w):
    w = w.reshape(MLA_Q_RANK, MLA_HEADS, MLA_NOPE + MLA_ROPE)
    rope = w[:, :, MLA_NOPE:]
    return jnp.concatenate([w, rope[:, :, _partner_perm()]], axis=2).reshape(MLA_Q_RANK, -1).astype(BF16)


def _layout_w_ukv(w):
    w = w.reshape(MLA_KV_RANK, MLA_HEADS, MLA_NOPE + MLA_V)
    return jnp.concatenate([w[:, :, :MLA_NOPE].reshape(MLA_KV_RANK, -1),
                            w[:, :, MLA_NOPE:].reshape(MLA_KV_RANK, -1)], axis=1).astype(BF16)


def kernel(x, c, ctx, c_ctx, w_mod, b_mod, g_attn, g_ffn, w_in, g_q, w_uq, g_kv, w_ukv, na_rpb, w_fnet, w_out,
           w_up, conv_w, conv_b, w_down, g_final):
    batch, seq, d = x.shape
    ctx_len = ctx.shape[1]
    depth = w_mod.shape[0]
    n_lat, n_ctx = batch * seq, batch * ctx_len
    bm = ROW_BLOCK
    lat_blocks = n_lat // bm
    all_blocks = (n_lat + n_ctx) // bm
    blocks_per_batch = seq // bm
    assert n_ctx % bm == 0 and n_lat // bm // blocks_per_batch == batch and batch < 8

    x_lat, x_ctx, ctx_block0 = x.reshape(n_lat, d), ctx.reshape(n_ctx, d), 0
    cin =jnp.zeros((8, d), F32).at[:batch].set(c).at[batch].set(c_ctx)
    mod = _modulation(cin, w_mod, b_mod)
    cs_tab = _rope_table(seq, bm)
    row = lambda v: v.reshape(1, -1)

    for l in range(depth):
        ctx_out = l < depth - 1
        mod_l = mod[l].reshape(8, 1, 6 * d)
        oc, on, of, okr = _inproj(x_lat, x_ctx, ctx_block0, n_lat + n_ctx, lat_blocks, mod_l, row(g_attn[l]),
                                  _layout_w_in(w_in[l]), blocks_per_batch)
        qm, km, vm = _mla_up(oc, okr, cs_tab, row(g_q[l]), row(g_kv[l]), _layout_w_uq(w_uq[l]),
                             _layout_w_ukv(w_ukv[l]), lat_blocks, blocks_per_batch)

        o_mla = _attention(qm, km, vm, batch=batch, heads=MLA_HEADS, dq=MLA_QK_PAD, dv=MLA_V, bq=2048, ck=1024,
                           seq=seq, ctx_len=ctx_len, k_col0=0, v_col0=0, name="mla_attention")
        o_na = _neighborhood_attention(on, _na_table(na_rpb[l], seq // GRID_W), batch, seq, ctx_len)
        o_fn = _fourier_latent(of, w_fnet[l].astype(BF16), batch, seq)
        ctx_ins = None
        if ctx_out:
            o_mla_c = _context_attention(
                qm, km, vm, batch=batch, heads=MLA_HEADS, dq=MLA_QK_PAD, dv=MLA_V, row0=n_lat, ctx_len=ctx_len,
                k_col0=0, v_col0=0, name="mla_attention_context")
            o_na_c = _context_attention(
                on, on, on, batch=batch, heads=NA_HEADS, dq=NA_HEAD_DIM, dv=NA_HEAD_DIM, row0=n_lat,
                ctx_len=ctx_len, k_col0=NA_HEADS, v_col0=2 * NA_HEADS, name="na_attention_context")
            o_fn_c = _fourier_context(of, w_fnet[l].astype(BF16), batch, n_lat, ctx_len)
            ctx_ins = (o_mla_c, o_na_c, o_fn_c)

        n_blocks = all_blocks if ctx_out else lat_blocks
        n_rows = n_blocks * bm
        unified = x_ctx is x_lat
        merge = ctx_out and not unified
        xa = _proj_residual(x_lat, (o_mla, o_na, o_fn), ctx_ins, w_out[l].astype(BF16), mod_l, 2, seq, n_rows, bm,
                            x_ctx=x_ctx if merge else None)
        a = _ffn_up(xa, mod_l, row(g_ffn[l]), w_up[l].astype(BF16), conv_w[l], row(conv_b[l]),
                    n_rows, n_lat, seq, ctx_len)
        xa = _proj_residual(xa, (a,), None, w_down[l].astype(BF16), mod_l, 5, seq, n_rows, bm,
                            final_gain=None if ctx_out else row(g_final))
        if merge or unified:
            x_lat, x_ctx, ctx_block0 = xa, xa, lat_blocks
        else:
            x_lat = xa

    return xa.reshape(batch, seq, d)
```

```python
import functools

import numpy as np
import jax
import jax.numpy as jnp
from jax import lax
from jax.experimental import pallas as pl
from jax.experimental.pallas import tpu as pltpu

F32 = jnp.float32
BF16 = jnp.bfloat16

D_MODEL = 2048
GRID_W = 64
EPS = 1e-6

MLA_HEADS = 8
MLA_NOPE = 128
MLA_ROPE = 64
MLA_V = 128
MLA_Q_RANK = 512
MLA_KV_RANK = 512
MLA_QK_PAD = 256
MLA_SCALE = (MLA_NOPE + MLA_ROPE) ** -0.5
ROPE_BASE = 10000.0

NA_HEADS = 4
NA_HEAD_DIM = 128
NA_KH = 8
NA_KW = 16
NA_SCALE = NA_HEAD_DIM ** -0.5
NA_QROWS = 8
NA_KROWS = NA_QROWS + NA_KH - 1

FN_GROUPS = 4
FN_CH = 128
FN_WIDTH = FN_GROUPS * FN_CH
FN_K2 = 16

MLA_WIDTH = MLA_HEADS * MLA_V
NA_WIDTH = NA_HEADS * NA_HEAD_DIM
D_FF = 5632
FFN_TILE = 128
CONV_W = 3

ROW_BLOCK = 512
FFN_ROWS = 1024
FFN_COLS = 1408
HALO = 16
PROJ_COLS = 512
MASK_VALUE = -1e30
LOG2E = 1.4426950408889634
VMEM_LIMIT = 60 * 1024 * 1024


def _params(*sem, flags=None):
    return pltpu.CompilerParams(dimension_semantics=sem, vmem_limit_bytes=VMEM_LIMIT, flags=flags)


def _dot(a, b):
    return jnp.dot(a, b, preferred_element_type=F32)


def _dot_nt(a, b):
    return lax.dot_general(a, b, (((1,), (1,)), ((), ())), preferred_element_type=F32)


def _split(x):
    hi = x.astype(BF16)
    lo = (x - hi.astype(F32)).astype(BF16)
    return hi, lo


def _dot3_left(m_hi, m_lo, x):
    x_hi, x_lo = _split(x)
    return _dot(m_hi, x_hi) + (_dot(m_hi, x_lo) + _dot(m_lo, x_hi))


def _dot3_right(x, m_hi, m_lo):
    x_hi, x_lo = _split(x)
    return _dot(x_hi, m_hi) + (_dot(x_lo, m_hi) + _dot(x_hi, m_lo))


def _sigmoid(x):
    return 1.0 / (1.0 + jnp.exp(-x))


def _rms(x, g):
    y = x * lax.rsqrt(jnp.mean(x * x, axis=-1, keepdims=True) + EPS)
    return y * g


def _normmod(x, g, sc, sh):
    return _rms(x, g) * (1.0 + sc) + sh


def _mod_kernel(c_ref, w_ref, b_ref, o_ref):
    c = c_ref[...]
    s = (c * _sigmoid(c)).astype(BF16)
    o_ref[...] = _dot(s, w_ref[...].astype(BF16)) + b_ref[...]


def _modulation(cin, w_mod, b_mod):
    depth, d, n = w_mod.shape
    bn = 1024
    return pl.pallas_call(
        _mod_kernel,
        grid=(depth, n // bn),
        in_specs=[
            pl.BlockSpec((8, d), lambda l, j: (0, 0)),
            pl.BlockSpec((None, d, bn), lambda l, j: (l, 0, j)),
            pl.BlockSpec((None, 1, bn), lambda l, j: (l, 0, j)),
        ],
        out_specs=pl.BlockSpec((None, 8, bn), lambda l, j: (l, 0, j)),
        out_shape=jax.ShapeDtypeStruct((depth, 8, n), F32),
        compiler_params=_params("arbitrary", "arbitrary"),
        name="modulation",
    )(cin, w_mod, b_mod.reshape(depth, 1, n))


def _inproj_kernel(x_ref, xc_ref, sh_ref, sc_ref, g_ref, w_ref, oc_ref, on_ref, of_ref, okr_ref, *, lat_blocks):
    def run(src_ref):
        h = _normmod(src_ref[...], g_ref[...], sc_ref[...], sh_ref[...]).astype(BF16)
        oc_ref[...] = _dot(h, w_ref[:, 0:1024])
        on_ref[:, 0:NA_WIDTH] = (_dot(h, w_ref[:, 1024:1024 + NA_WIDTH]) * (NA_SCALE * LOG2E)).astype(BF16)
        on_ref[:, NA_WIDTH:] = _dot(h, w_ref[:, 1024 + NA_WIDTH:2560]).astype(BF16)
        of_ref[...] = _dot(h, w_ref[:, 2560:3072])
        okr_ref[...] = _dot(h, w_ref[:, 3072:3200])

    i = pl.program_id(0)
    pl.when(i < lat_blocks)(lambda: run(x_ref))
    pl.when(i >= lat_blocks)(lambda: run(xc_ref))


def _mod_spec(k, blocks_per_batch):
    return pl.BlockSpec((None, 1, D_MODEL), lambda i, *_: (i // blocks_per_batch, 0, k))


def _two_source_specs(bm, width, lat_blocks, ctx_block0):
    return [pl.BlockSpec((bm, width), lambda i, *_: (jnp.minimum(i, lat_blocks - 1), 0)),
            pl.BlockSpec((bm, width), lambda i, *_: (ctx_block0 + jnp.maximum(i - lat_blocks, 0), 0))]


def _inproj(x_lat, x_ctx, ctx_block0, rows, lat_blocks, mod_l, g, w_p, blocks_per_batch):
    bm = ROW_BLOCK
    row = lambda w: pl.BlockSpec((bm, w), lambda i: (i, 0))
    return pl.pallas_call(
        functools.partial(_inproj_kernel, lat_blocks=lat_blocks),
        grid=(rows // bm,),
        in_specs=_two_source_specs(bm, D_MODEL, lat_blocks, ctx_block0) + [
            _mod_spec(0, blocks_per_batch),
            _mod_spec(1, blocks_per_batch),
            pl.BlockSpec((1, D_MODEL), lambda i: (0, 0)),
            pl.BlockSpec(w_p.shape, lambda i: (0, 0), pipeline_mode=pl.Buffered(1)),
        ],
        out_specs=[row(1024), row(1536), row(512), row(128)],
        out_shape=[
            jax.ShapeDtypeStruct((rows, 1024), F32),
            jax.ShapeDtypeStruct((rows, 1536), BF16),
            jax.ShapeDtypeStruct((rows, 512), F32),
            jax.ShapeDtypeStruct((rows, 128), F32),
        ],
        compiler_params=_params("arbitrary"),
        name="in_projection",
    )(x_lat, x_ctx, mod_l, mod_l, g, w_p)


def _rope(t, cs):
    t = t * cs
    return t + pltpu.roll(t, 64, axis=1)


def _mla_up_kernel(c_ref, kr_ref, cs_ref, gq_ref, gkv_ref, wq_ref, wkv_ref, q_ref, k_ref, v_ref):
    cs = cs_ref[...]
    cq = _rms(c_ref[:, 0:MLA_Q_RANK], gq_ref[...]).astype(BF16)
    ckv = _rms(c_ref[:, MLA_Q_RANK:MLA_Q_RANK + MLA_KV_RANK], gkv_ref[...]).astype(BF16)
    q = _dot(cq, wq_ref[...]) * (MLA_SCALE * LOG2E)
    kv = _dot(ckv, wkv_ref[...])
    lane = lax.broadcasted_iota(jnp.int32, cs.shape, 1)
    k_rope = jnp.where(lane < MLA_ROPE, _rope(kr_ref[...], cs), 0.0).astype(BF16)
    for h in range(MLA_HEADS):
        o = h * MLA_QK_PAD
        q_ref[:, o:o + MLA_NOPE] = q[:, o:o + MLA_NOPE].astype(BF16)
        q_ref[:, o + MLA_NOPE:o + MLA_QK_PAD] = _rope(q[:, o + MLA_NOPE:o + MLA_QK_PAD], cs).astype(BF16)
        k_ref[:, o:o + MLA_NOPE] = kv[:, h * MLA_NOPE:(h + 1) * MLA_NOPE].astype(BF16)
        k_ref[:, o + MLA_NOPE:o + MLA_QK_PAD] = k_rope
    v_ref[...] = kv[:, MLA_HEADS * MLA_NOPE:].astype(BF16)


def _mla_up(oc, okr, cs_tab, g_q, g_kv, wq_p, wkv_p, lat_blocks, pos_blocks):
    rows = oc.shape[0]
    bm = ROW_BLOCK
    row = lambda w: pl.BlockSpec((bm, w), lambda i: (i, 0))
    const = lambda a: pl.BlockSpec(a.shape, lambda i: (0, 0))
    cs_spec = pl.BlockSpec((bm, 128), lambda i: (jnp.where(i < lat_blocks, i % pos_blocks, pos_blocks), 0))
    qk_w = MLA_HEADS * MLA_QK_PAD
    return pl.pallas_call(
        _mla_up_kernel,
        grid=(rows // bm,),
        in_specs=[row(1024), row(128), cs_spec, const(g_q), const(g_kv), const(wq_p), const(wkv_p)],
        out_specs=[row(qk_w), row(qk_w), row(MLA_WIDTH)],
        out_shape=[
            jax.ShapeDtypeStruct((rows, qk_w), BF16),
            jax.ShapeDtypeStruct((rows, qk_w), BF16),
            jax.ShapeDtypeStruct((rows, MLA_WIDTH), BF16),
        ],
        compiler_params=_params("arbitrary"),
        name="mla_up_projection",
    )(oc, okr, cs_tab, g_q, g_kv, wq_p, wkv_p)


def _with_ones(v):
    return jnp.concatenate([v, jnp.ones_like(v)], axis=1)


def _softmax_first(s, v):
    m = jnp.max(s, axis=-1, keepdims=True)
    return m, _dot(jnp.exp2(s - m).astype(BF16), _with_ones(v))


def _softmax_next(s, s_max, v, m, acc):
    m_new = jnp.maximum(m, s_max)
    p = jnp.exp2(s - m_new)
    return m_new, jnp.exp2(m - m_new) * acc + _dot(p.astype(BF16), _with_ones(v))


def _normalised(acc):
    dv = acc.shape[1] // 2
    return acc[:, :dv] / acc[:, dv:]


def _attention_kernel(q_ref, k_ref, v_ref, kc_ref, vc_ref, o_ref, sa_ref, sb_ref, *, ck, n_main):
    q = q_ref[...]
    carry = _softmax_first(_dot_nt(q, kc_ref[...]), vc_ref[...])
    bufs = (sa_ref, sb_ref)

    def scores(c):
        s = _dot_nt(q, k_ref[c * ck:(c + 1) * ck, :])
        bufs[c % 2][...] = s
        return jnp.max(s, axis=-1, keepdims=True)

    s_max = scores(0)
    for c in range(n_main):
        nxt_max = scores(c + 1) if c + 1 < n_main else None
        carry = _softmax_next(bufs[c % 2][...], s_max, v_ref[c * ck:(c + 1) * ck, :], *carry)
        s_max = nxt_max
    o_ref[...] = _normalised(carry[1]).astype(o_ref.dtype)


def _attention(q_arr, k_arr, v_arr, *, batch, heads, dq, dv, bq, seq, ctx_len, k_col0, v_col0, name, ck=512):
    nq = seq // bq
    ctx0 = batch * seq // ctx_len
    return pl.pallas_call(
        functools.partial(_attention_kernel, ck=ck, n_main=seq // ck),
        grid=(batch, heads, nq),
        in_specs=[
            pl.BlockSpec((bq, dq), lambda b, h, i: (b * nq + i, h)),
            pl.BlockSpec((seq, dq), lambda b, h, i: (b, k_col0 + h)),
            pl.BlockSpec((seq, dv), lambda b, h, i: (b, v_col0 + h)),
            pl.BlockSpec((ctx_len, dq), lambda b, h, i: (ctx0 + b, k_col0 + h)),
            pl.BlockSpec((ctx_len, dv), lambda b, h, i: (ctx0 + b, v_col0 + h)),
        ],
        out_specs=pl.BlockSpec((bq, dv), lambda b, h, i: (b * nq + i, h)),
        out_shape=jax.ShapeDtypeStruct((batch * seq, heads * dv), BF16),
        scratch_shapes=[pltpu.VMEM((bq, ck), F32), pltpu.VMEM((bq, ck), F32)],
        compiler_params=_params("arbitrary", "arbitrary", "arbitrary"),
        name=name,
    )(q_arr, k_arr, v_arr, k_arr, v_arr)


def _context_attention_kernel(q_ref, k_ref, v_ref, o_ref):
    _, acc = _softmax_first(_dot_nt(q_ref[...], k_ref[...]), v_ref[...])
    o_ref[...] = _normalised(acc).astype(o_ref.dtype)


def _context_attention(q_arr, k_arr, v_arr, *, batch, heads, dq, dv, row0, ctx_len, k_col0, v_col0, name):
    blk0 = row0 // ctx_len
    return pl.pallas_call(
        _context_attention_kernel,
        grid=(batch, heads),
        in_specs=[
            pl.BlockSpec((ctx_len, dq), lambda b, h: (blk0 + b, h)),
            pl.BlockSpec((ctx_len, dq), lambda b, h: (blk0 + b, k_col0 + h)),
            pl.BlockSpec((ctx_len, dv), lambda b, h: (blk0 + b, v_col0 + h)),
        ],
        out_specs=pl.BlockSpec((ctx_len, dv), lambda b, h: (b, h)),
        out_shape=jax.ShapeDtypeStruct((batch * ctx_len, heads * dv), BF16),
        compiler_params=_params("arbitrary", "arbitrary"),
        name=name,
    )(q_arr, k_arr, v_arr)


def _na_kernel(q_ref, k_ref, v_ref, kc_ref, vc_ref, ta_ref, tb_ref, o_ref, *s_refs, grid_rows):
    bq = NA_QROWS * GRID_W
    nwin = NA_KROWS * GRID_W

    def scores(half, t_ref, sw_ref, sc_ref):
        r = pl.program_id(2) * 2 + half
        ks = jnp.clip(r * NA_QROWS - NA_KH // 2, 0, grid_rows - NA_KROWS)
        keys = pl.ds(pl.multiple_of(ks * GRID_W, GRID_W), nwin)
        q = q_ref[half * bq:(half + 1) * bq, :]
        s_win = _dot_nt(q, k_ref[keys, :]) + t_ref[...]
        s_ctx = _dot_nt(q, kc_ref[...])
        sw_ref[...] = s_win
        sc_ref[...] = s_ctx
        return keys, jnp.maximum(jnp.max(s_win, axis=-1, keepdims=True), jnp.max(s_ctx, axis=-1, keepdims=True))

    def finish(half, keys, m, sw_ref, sc_ref):
        p_win = jnp.exp2(sw_ref[...] - m).astype(BF16)
        p_ctx = jnp.exp2(sc_ref[...] - m).astype(BF16)
        acc = _dot(p_win, _with_ones(v_ref[keys, :])) + _dot(p_ctx, _with_ones(vc_ref[...]))
        o_ref[half * bq:(half + 1) * bq, :] = _normalised(acc).astype(o_ref.dtype)

    first = scores(0, ta_ref, *s_refs[0:2])
    second = scores(1, tb_ref, *s_refs[2:4])
    finish(0, *first, *s_refs[0:2])
    finish(1, *second, *s_refs[2:4])


def _na_table(rpb, grid_rows):
    w = np.arange(GRID_W)[:, None]
    j = np.arange(GRID_W)[None, :]
    cs = np.clip(w - NA_KW // 2, 0, GRID_W - NA_KW)
    sel_c = ((j - w + (NA_KW - 1))[:, :, None] == np.arange(2 * NA_KW - 1)).astype(np.float32)
    by_col = jnp.einsum("hpq,wjq->hpwj", rpb.astype(F32), sel_c, precision=lax.Precision.HIGHEST)
    by_col = jnp.where((j >= cs) & (j < cs + NA_KW), by_col * LOG2E, MASK_VALUE)
    masked = jnp.full((rpb.shape[0], GRID_W, GRID_W), MASK_VALUE, F32)
    tabs = []
    for r0 in (0, NA_QROWS, grid_rows - NA_QROWS):
        ks = min(max(r0 - NA_KH // 2, 0), grid_rows - NA_KROWS)
        rows = []
        for r in range(r0, r0 + NA_QROWS):
            rs = min(max(r - NA_KH // 2, 0), grid_rows - NA_KH)
            tiles = [by_col[:, kr - r + NA_KH - 1] if rs <= kr < rs + NA_KH else masked
                     for kr in range(ks, ks + NA_KROWS)]
            rows.append(jnp.concatenate(tiles, axis=-1))
        tabs.append(jnp.concatenate(rows, axis=1))
    return jnp.stack(tabs)


def _neighborhood_attention(on, table, batch, seq, ctx_len):
    grid_rows = seq // GRID_W
    nstep = grid_rows // (2 * NA_QROWS)
    bq = NA_QROWS * GRID_W
    nwin = NA_KROWS * GRID_W
    d = NA_HEAD_DIM
    ctx0 = batch * seq // ctx_len
    assert nstep >= 2
    return pl.pallas_call(
        functools.partial(_na_kernel, grid_rows=grid_rows),
        grid=(batch, NA_HEADS, nstep),
        in_specs=[
            pl.BlockSpec((2 * bq, d), lambda b, h, r: (b * nstep + r, h)),
            pl.BlockSpec((seq, d), lambda b, h, r: (b, NA_HEADS + h)),
            pl.BlockSpec((seq, d), lambda b, h, r: (b, 2 * NA_HEADS + h)),
            pl.BlockSpec((ctx_len, d), lambda b, h, r: (ctx0 + b, NA_HEADS + h)),
            pl.BlockSpec((ctx_len, d), lambda b, h, r: (ctx0 + b, 2 * NA_HEADS + h)),
            pl.BlockSpec((None, None, bq, nwin), lambda b, h, r: (jnp.where(r == 0, 0, 1), h, 0, 0)),
            pl.BlockSpec((None, None, bq, nwin), lambda b, h, r: (jnp.where(r == nstep - 1, 2, 1), h, 0, 0)),
        ],
        out_specs=pl.BlockSpec((2 * bq, d), lambda b, h, r: (b * nstep + r, h)),
        out_shape=jax.ShapeDtypeStruct((batch * seq, NA_WIDTH), BF16),
        scratch_shapes=[pltpu.VMEM((bq, nwin), F32), pltpu.VMEM((bq, ctx_len), F32)] * 2,
        compiler_params=_params("arbitrary", "arbitrary", "arbitrary"),
        name="neighborhood_attention",
    )(on, on, on, on, on, table, table)


def _dft_consts(n):
    jk = (np.arange(n)[:, None] * np.arange(n)[None, :]) % n
    ang = 2.0 * np.pi * jk / n
    return np.cos(ang), np.sin(ang)


def _hi_lo(m):
    m = jnp.asarray(m, F32)
    hi = m.astype(BF16)
    return hi, (m - hi.astype(F32)).astype(BF16)


def _fn_stage1_kernel(x_ref, fh_ref, fl_ref, tc_ref, ts_ref, o_ref):
    n = GRID_W
    b = _dot3_left(fh_ref[...], fl_ref[...], x_ref[...])
    br, bi = b[:n], b[n:]
    tc, ts = tc_ref[...], ts_ref[...]
    o_ref[0] = br * tc + bi * ts
    o_ref[1] = bi * tc - br * ts


def _fn_stage2_kernel(t_ref, gh_ref, gl_ref, ch_ref, cl_ref, w_ref, p_ref, o_ref):
    n = GRID_W
    outs = []
    for j in range(FN_K2):
        t = jnp.concatenate([t_ref[0, j], t_ref[1, j]], axis=0)
        a = _dot3_left(gh_ref[...], gl_ref[...], t)
        outs.append(a)
    ar = jnp.concatenate([a[:n] for a in outs], axis=0)
    ai = jnp.concatenate([a[n:] for a in outs], axis=0)
    ys = []
    for g in range(FN_GROUPS):
        sl = slice(g * FN_CH, (g + 1) * FN_CH)
        z = jnp.concatenate([ar[:, sl], ai[:, sl]], axis=1)
        spec = _dot3_right(z, ch_ref[...], cl_ref[...])
        ys.append(_dot(spec.astype(BF16), w_ref[g]).astype(BF16))
    y = jnp.concatenate(ys, axis=1)
    y = _dot(p_ref[...], y).astype(BF16)
    o_ref[...] = y.reshape(n, FN_K2, FN_WIDTH)


def _fourier_latent(of, w_fnet, batch, seq):
    n = GRID_W
    assert seq == n * n
    cols = n * FN_WIDTH
    c64, s64 = _dft_consts(n)
    f1h, f1l = _hi_lo(np.concatenate([c64, -s64], axis=0))
    tw = 2.0 * np.pi * (np.arange(n)[:, None] * np.arange(n)[None, :]) / (n * n)
    tc = jnp.asarray(np.repeat(np.cos(tw).reshape(n, n, 1), FN_WIDTH, axis=2).reshape(n, cols), F32)
    ts = jnp.asarray(np.repeat(np.sin(tw).reshape(n, n, 1), FN_WIDTH, axis=2).reshape(n, cols), F32)
    x2 = of[:batch * seq].reshape(batch, n, cols)
    cb = 4096
    t = pl.pallas_call(
        _fn_stage1_kernel,
        grid=(batch, cols // cb),
        in_specs=[
            pl.BlockSpec((None, n, cb), lambda b, j: (b, 0, j)),
            pl.BlockSpec(f1h.shape, lambda b, j: (0, 0)),
            pl.BlockSpec(f1l.shape, lambda b, j: (0, 0)),
            pl.BlockSpec((n, cb), lambda b, j: (0, j)),
            pl.BlockSpec((n, cb), lambda b, j: (0, j)),
        ],
        out_specs=pl.BlockSpec((None, 2, n, cb), lambda b, j: (b, 0, 0, j)),
        out_shape=jax.ShapeDtypeStruct((batch, 2, n, cols), F32),
        compiler_params=_params("arbitrary", "arbitrary"),
        name="fourier_rows",
    )(x2, f1h, f1l, tc, ts)
    t = t.reshape(batch, 2, n, n, FN_WIDTH)

    g2h, g2l = _hi_lo(np.block([[c64, s64], [-s64, c64]]))
    cc, sc = _dft_consts(FN_CH)
    norm = 1.0 / np.sqrt(seq * FN_CH)
    c4h, c4l = _hi_lo(np.concatenate([cc, sc], axis=0) * norm)
    rows = FN_K2 * n
    perm = np.zeros((rows, rows), np.float32)
    k1 = np.arange(n)[:, None]
    j = np.arange(FN_K2)[None, :]
    perm[(k1 * FN_K2 + j).ravel(), (j * n + k1).ravel()] = 1.0
    perm = jnp.asarray(perm, BF16)
    const = lambda a: pl.BlockSpec(a.shape, lambda b, i: (0,) * a.ndim)
    y = pl.pallas_call(
        _fn_stage2_kernel,
        grid=(batch, n // FN_K2),
        in_specs=[
            pl.BlockSpec((None, 2, FN_K2, n, FN_WIDTH), lambda b, i: (b, 0, i, 0, 0)),
            const(g2h), const(g2l), const(c4h), const(c4l), const(w_fnet), const(perm),
        ],
        out_specs=pl.BlockSpec((None, n, FN_K2, FN_WIDTH), lambda b, i: (b, 0, i, 0)),
        out_shape=jax.ShapeDtypeStruct((batch, n, n, FN_WIDTH), BF16),
        compiler_params=_params("arbitrary", "arbitrary"),
        name="fourier_cols_channels",
    )(t, g2h, g2l, c4h, c4l, w_fnet, perm)
    return y.reshape(batch * seq, FN_WIDTH)


def _fn_ctx_kernel(x_ref, fh_ref, fl_ref, ch_ref, cl_ref, w_ref, o_ref, *, n):
    a = _dot3_left(fh_ref[...], fl_ref[...], x_ref[...])
    ar, ai = a[:n], a[n:]
    ys = []
    for g in range(FN_GROUPS):
        sl = slice(g * FN_CH, (g + 1) * FN_CH)
        z = jnp.concatenate([ar[:, sl], ai[:, sl]], axis=1)
        spec = _dot3_right(z, ch_ref[...], cl_ref[...])
        ys.append(_dot(spec.astype(BF16), w_ref[g]).astype(BF16))
    o_ref[...] = jnp.concatenate(ys, axis=1)


def _fourier_context(of, w_fnet, batch, row0, n):
    c, s = _dft_consts(n)
    fh, fl = _hi_lo(np.concatenate([c, -s], axis=0))
    cc, sc = _dft_consts(FN_CH)
    c4h, c4l = _hi_lo(np.concatenate([cc, sc], axis=0) / np.sqrt(n * FN_CH))
    const = lambda a: pl.BlockSpec(a.shape, lambda b: (0,) * a.ndim)
    blk0 = row0 // n
    return pl.pallas_call(
        functools.partial(_fn_ctx_kernel, n=n),
        grid=(batch,),
        in_specs=[pl.BlockSpec((n, FN_WIDTH), lambda b: (blk0 + b, 0)),
                  const(fh), const(fl), const(c4h), const(c4l), const(w_fnet)],
        out_specs=pl.BlockSpec((n, FN_WIDTH), lambda b: (b, 0)),
        out_shape=jax.ShapeDtypeStruct((batch * n, FN_WIDTH), BF16),
        compiler_params=_params("arbitrary"),
        name="fourier_context",
    )(of, fh, fl, c4h, c4l, w_fnet)


def _proj_residual_kernel(*refs, n_in, lat_blocks, has_ctx, split_x, final_norm):
    lat = refs[:n_in]
    ctx = refs[n_in:2 * n_in] if has_ctx else ()
    rest = refs[(2 if has_ctx else 1) * n_in:]
    w_refs, rest = rest[:n_in], rest[n_in:]
    x_ref, xc_ref = (rest[0], rest[1]) if split_x else (rest[0], rest[0])
    gt_ref, *gf_ref, o_ref = rest[2 if split_x else 1:]

    def run(a_refs, res_ref):
        for n in range(0, D_MODEL, PROJ_COLS):
            cols = slice(n, n + PROJ_COLS)
            acc = None
            for a_ref, w_ref in zip(a_refs, w_refs):
                t = _dot(a_ref[...], w_ref[:, cols])
                acc = t if acc is None else acc + t
            o_ref[:, cols] = res_ref[:, cols] + gt_ref[:, cols] * acc

    if has_ctx:
        i = pl.program_id(0)
        pl.when(i < lat_blocks)(lambda: run(lat, x_ref))
        pl.when(i >= lat_blocks)(lambda: run(ctx, xc_ref))
    else:
        run(lat, x_ref)
    if final_norm:
        o_ref[...] = _rms(o_ref[...], gf_ref[0][...])


def _proj_residual(xa, lat_ins, ctx_ins, w, mod_l, gate_k, rows_per_batch, n_rows, bm, final_gain=None, x_ctx=None):
    n_in = len(lat_ins)
    has_ctx = ctx_ins is not None
    split_x = x_ctx is not None
    lat_blocks = lat_ins[0].shape[0] // bm
    bpb = rows_per_batch // bm
    in_specs = [pl.BlockSpec((bm, a.shape[1]), lambda i: (jnp.minimum(i, lat_blocks - 1), 0)) for a in lat_ins]
    if has_ctx:
        in_specs += [pl.BlockSpec((bm, a.shape[1]), lambda i: (jnp.maximum(i - lat_blocks, 0), 0)) for a in ctx_ins]
    offs = np.cumsum([0] + [a.shape[1] for a in lat_ins])
    assert offs[-1] == w.shape[0] and all(o % a.shape[1] == 0 for o, a in zip(offs, lat_ins))
    in_specs += [pl.BlockSpec((a.shape[1], D_MODEL), lambda i, k=int(o) // a.shape[1]: (k, 0),
                              pipeline_mode=pl.Buffered(1)) for o, a in zip(offs, lat_ins)]
    x_index = len(in_specs)
    if split_x:
        assert has_ctx and xa.shape[0] == lat_blocks * bm
        in_specs += _two_source_specs(bm, D_MODEL, lat_blocks, 0)
    else:
        in_specs.append(pl.BlockSpec((bm, D_MODEL), lambda i: (i, 0)))
    in_specs.append(pl.BlockSpec((None, 1, D_MODEL), lambda i: (i // bpb, 0, gate_k)))
    args = list(lat_ins) + (list(ctx_ins) if has_ctx else []) + [w] * n_in + [xa] + ([x_ctx] if split_x else [])
    args.append(mod_l)
    final_norm = final_gain is not None
    if final_norm:
        in_specs.append(pl.BlockSpec((1, D_MODEL), lambda i: (0, 0)))
        args.append(final_gain)
    in_place = not (final_norm or split_x)
    return pl.pallas_call(
        functools.partial(_proj_residual_kernel, n_in=n_in, lat_blocks=lat_blocks, has_ctx=has_ctx,
                          split_x=split_x, final_norm=final_norm),
        grid=(n_rows // bm,),
        in_specs=in_specs,
        out_specs=pl.BlockSpec((bm, D_MODEL), lambda i: (i, 0)),
        out_shape=jax.ShapeDtypeStruct(xa.shape if in_place else (n_rows, D_MODEL), F32),
        input_output_aliases={x_index: 0} if in_place else {},
        compiler_params=_params("arbitrary"),
        name="projection_residual",
    )(*args)


def _ffn_up_kernel(xm_ref, xp_ref, xn_ref, sh_ref, sc_ref, g_ref, wg_ref, wv_ref, cwg_ref, cwv_ref, cbg_ref,
                   cbv_ref, o_ref, h_ref, *u_refs, lat_blocks, seq, ctx_len):
    bm = o_ref.shape[0]
    i = pl.program_id(0)

    @pl.when(pl.program_id(1) == 0)
    def _():
        g, sc, sh = g_ref[...], sc_ref[...], sh_ref[...]
        h_ref[0:HALO, :] = _normmod(xp_ref[...], g, sc, sh).astype(BF16)
        h_ref[HALO:HALO + bm, :] = _normmod(xm_ref[...], g, sc, sh).astype(BF16)
        h_ref[HALO + bm:, :] = _normmod(xn_ref[...], g, sc, sh).astype(BF16)

    t = FFN_TILE
    n_tiles = o_ref.shape[1] // t

    def pair(g_ref, v_ref, rows, k):
        cols = slice(t * k, t * (k + 1))
        return jnp.concatenate([g_ref[rows, cols], v_ref[rows, cols]], axis=1)

    def matmul(k):
        u_refs[k % 2][...] = _dot(h_ref[...], pair(wg_ref, wv_ref, slice(None), k))

    def run(interior_boundaries):
        if interior_boundaries:
            rows = lax.broadcasted_iota(jnp.int32, (bm, 1), 0)
            first = (rows & (ctx_len - 1)) == 0
            last = ((rows + 1) & (ctx_len - 1)) == 0
        else:
            starts = (i * bm) % seq == 0
            ends = ((i + 1) * bm) % seq == 0
        matmul(0)
        for k in range(n_tiles):
            if k + 1 < n_tiles:
                matmul(k + 1)
            src = u_refs[k % 2]
            if interior_boundaries:
                prev = jnp.where(first, 0.0, src[HALO - 1:HALO - 1 + bm, :])
                nxt = jnp.where(last, 0.0, src[HALO + 1:HALO + 1 + bm, :])
            else:
                src[HALO - 1:HALO, :] = jnp.where(starts, 0.0, src[HALO - 1:HALO, :])
                src[HALO + bm:HALO + bm + 1, :] = jnp.where(ends, 0.0, src[HALO + bm:HALO + bm + 1, :])
                prev = src[HALO - 1:HALO - 1 + bm, :]
                nxt = src[HALO + 1:HALO + 1 + bm, :]
            c = (prev * pair(cwg_ref, cwv_ref, slice(0, 1), k)
                 + src[HALO:HALO + bm, :] * pair(cwg_ref, cwv_ref, slice(1, 2), k)
                 + nxt * pair(cwg_ref, cwv_ref, slice(2, 3), k) + pair(cbg_ref, cbv_ref, slice(None), k))
            gate, val = c[:, :t], c[:, t:]
            o_ref[:, t * k:t * (k + 1)] = (gate * _sigmoid(gate) * val).astype(BF16)

    pl.when(i < lat_blocks)(lambda: run(False))
    pl.when(i >= lat_blocks)(lambda: run(True))


def _ffn_up(xa, mod_l, g, w_up, conv_w, conv_b, n_rows, n_lat, seq, ctx_len):
    rows = xa.shape[0]
    bm, bn = FFN_ROWS, FFN_COLS
    assert seq % bm == 0 and bm % ctx_len == 0 and n_lat % bm == 0 and n_rows % bm == 0
    blocks_per_batch, n_blocks, lat_blocks = seq // bm, n_rows // bm, n_lat // bm
    nj = D_FF // bn
    hb = bm // HALO
    last_halo = rows // HALO - 1
    assert seq & (seq - 1) == 0 and ctx_len & (ctx_len - 1) == 0
    return pl.pallas_call(
        functools.partial(_ffn_up_kernel, lat_blocks=lat_blocks, seq=seq, ctx_len=ctx_len),
        grid=(n_blocks, nj),
        in_specs=[
            pl.BlockSpec((bm, D_MODEL), lambda i, j: (i, 0)),
            pl.BlockSpec((HALO, D_MODEL), lambda i, j: (jnp.maximum(i * hb - 1, 0), 0)),
            pl.BlockSpec((HALO, D_MODEL), lambda i, j: (jnp.minimum((i + 1) * hb, last_halo), 0)),
            _mod_spec(3, blocks_per_batch),
            _mod_spec(4, blocks_per_batch),
            pl.BlockSpec((1, D_MODEL), lambda i, j: (0, 0)),
            pl.BlockSpec((D_MODEL, bn), lambda i, j: (0, j)),
            pl.BlockSpec((D_MODEL, bn), lambda i, j: (0, nj + j)),
            pl.BlockSpec((CONV_W, bn), lambda i, j: (0, j)),
            pl.BlockSpec((CONV_W, bn), lambda i, j: (0, nj + j)),
            pl.BlockSpec((1, bn), lambda i, j: (0, j)),
            pl.BlockSpec((1, bn), lambda i, j: (0, nj + j)),
        ],
        out_specs=pl.BlockSpec((bm, bn), lambda i, j: (i, j)),
        out_shape=jax.ShapeDtypeStruct((n_blocks * bm, D_FF), BF16),
        scratch_shapes=[pltpu.VMEM((bm + 2 * HALO, D_MODEL), BF16)]
        + [pltpu.VMEM((bm + 2 * HALO, 2 * FFN_TILE), F32)] * 2,
        compiler_params=_params("arbitrary", "arbitrary"),
        name="ffn_up_conv_gate",
    )(xa, xa, xa, mod_l, mod_l, g, w_up, w_up, conv_w, conv_w, conv_b, conv_b)


def _rope_table(seq, pad_rows):
    n = MLA_ROPE // 4
    freqs = ROPE_BASE ** (-jnp.arange(n, dtype=F32) / n)
    pos = jnp.arange(seq)
    ang_r = (pos // GRID_W).astype(F32)[:, None] * freqs
    ang_c = (pos % GRID_W).astype(F32)[:, None] * freqs
    cos = jnp.concatenate([jnp.cos(ang_r)] * 2 + [jnp.cos(ang_c)] * 2, axis=1)
    sin = jnp.concatenate([-jnp.sin(ang_r), jnp.sin(ang_r), -jnp.sin(ang_c), jnp.sin(ang_c)], axis=1)
    lat = jnp.concatenate([cos, sin], axis=1)
    ident = jnp.concatenate([jnp.ones((pad_rows, MLA_ROPE), F32), jnp.zeros((pad_rows, MLA_ROPE), F32)], axis=1)
    return jnp.concatenate([lat, ident], axis=0)


def _partner_perm():
    q = MLA_ROPE // 4
    return np.concatenate([np.arange(q, 2 * q), np.arange(0, q), np.arange(3 * q, 4 * q), np.arange(2 * q, 3 * q)])


def _layout_w_in(w, l):
    offs = np.cumsum((0, MLA_Q_RANK, MLA_KV_RANK, MLA_ROPE, NA_WIDTH, NA_WIDTH, NA_WIDTH, FN_WIDTH))
    cq, ckv, kr, qn, kn, vn, f = (w[l, :, a:b].astype(BF16) for a, b in zip(offs[:-1], offs[1:]))
    return jnp.concatenate([cq, ckv, qn, kn, vn, f, kr, kr[:, _partner_perm()]], axis=1)


def _layout_w_uq(w):
    w = w.reshape(MLA_Q_RANK, MLA_HEADS, MLA_NOPE + MLA_ROPE)
    rope = w[:, :, MLA_NOPE:]
    return jnp.concatenate([w, rope[:, :, _partner_perm()]], axis=2).reshape(MLA_Q_RANK, -1).astype(BF16)


def _layout_w_ukv(w):
    w = w.reshape(MLA_KV_RANK, MLA_HEADS, MLA_NOPE + MLA_V)
    return jnp.concatenate([w[:, :, :MLA_NOPE].reshape(MLA_KV_RANK, -1),
                            w[:, :, MLA_NOPE:].reshape(MLA_KV_RANK, -1)], axis=1).astype(BF16)


def kernel(x, c, ctx, c_ctx, w_mod, b_mod, g_attn, g_ffn, w_in, g_q, w_uq, g_kv, w_ukv, na_rpb, w_fnet, w_out,
           w_up, conv_w, conv_b, w_down, g_final):
    batch, seq, d = x.shape
    ctx_len = ctx.shape[1]
    depth = w_mod.shape[0]
    n_lat, n_ctx = batch * seq, batch * ctx_len
    bm = ROW_BLOCK
    lat_blocks = n_lat // bm
    all_blocks = (n_lat + n_ctx) // bm
    blocks_per_batch = seq // bm
    assert n_ctx % bm == 0 and n_lat // bm // blocks_per_batch == batch and batch < 8

    x_lat, x_ctx, ctx_block0 = x.reshape(n_lat, d), ctx.reshape(n_ctx, d), 0
    cin =jnp.zeros((8, d), F32).at[:batch].set(c).at[batch].set(c_ctx)
    mod = _modulation(cin, w_mod, b_mod)
    cs_tab = _rope_table(seq, bm)
    row = lambda v: v.reshape(1, -1)

    for l in range(depth):
        ctx_out = l < depth - 1
        mod_l = mod[l].reshape(8, 1, 6 * d)
        oc, on, of, okr = _inproj(x_lat, x_ctx, ctx_block0, n_lat + n_ctx, lat_blocks, mod_l, row(g_attn[l]),
                                  _layout_w_in(w_in, l), blocks_per_batch)
        qm, km, vm = _mla_up(oc, okr, cs_tab, row(g_q[l]), row(g_kv[l]), _layout_w_uq(w_uq[l]),
                             _layout_w_ukv(w_ukv[l]), lat_blocks, blocks_per_batch)

        o_mla = _attention(qm, km, vm, batch=batch, heads=MLA_HEADS, dq=MLA_QK_PAD, dv=MLA_V, bq=2048, ck=1024,
                           seq=seq, ctx_len=ctx_len, k_col0=0, v_col0=0, name="mla_attention")
        o_na = _neighborhood_attention(on, _na_table(na_rpb[l], seq // GRID_W), batch, seq, ctx_len)
        o_fn = _fourier_latent(of, w_fnet[l].astype(BF16), batch, seq)
        ctx_ins = None
        if ctx_out:
            o_mla_c = _context_attention(
                qm, km, vm, batch=batch, heads=MLA_HEADS, dq=MLA_QK_PAD, dv=MLA_V, row0=n_lat, ctx_len=ctx_len,
                k_col0=0, v_col0=0, name="mla_attention_context")
            o_na_c = _context_attention(
                on, on, on, batch=batch, heads=NA_HEADS, dq=NA_HEAD_DIM, dv=NA_HEAD_DIM, row0=n_lat,
                ctx_len=ctx_len, k_col0=NA_HEADS, v_col0=2 * NA_HEADS, name="na_attention_context")
            o_fn_c = _fourier_context(of, w_fnet[l].astype(BF16), batch, n_lat, ctx_len)
            ctx_ins = (o_mla_c, o_na_c, o_fn_c)

        n_blocks = all_blocks if ctx_out else lat_blocks
        n_rows = n_blocks * bm
        unified = x_ctx is x_lat
        merge = ctx_out and not unified
        xa = _proj_residual(x_lat, (o_mla, o_na, o_fn), ctx_ins, w_out[l].astype(BF16), mod_l, 2, seq, n_rows, bm,
                            x_ctx=x_ctx if merge else None)
        a = _ffn_up(xa, mod_l, row(g_ffn[l]), w_up[l].astype(BF16), conv_w[l], row(conv_b[l]),
                    n_rows, n_lat, seq, ctx_len)
        xa = _proj_residual(xa, (a,), None, w_down[l].astype(BF16), mod_l, 5, seq, n_rows, bm,
                            final_gain=None if ctx_out else row(g_final))
        if merge or unified:
            x_lat, x_ctx, ctx_block0 = xa, xa, lat_blocks
        else:
            x_lat = xa

    return xa.reshape(batch, seq, d)
```

```python
import functools

import numpy as np
import jax
import jax.numpy as jnp
from jax import lax
from jax.experimental import pallas as pl
from jax.experimental.pallas import tpu as pltpu

F32 = jnp.float32
BF16 = jnp.bfloat16

D_MODEL = 2048
GRID_W = 64
EPS = 1e-6

MLA_HEADS = 8
MLA_NOPE = 128
MLA_ROPE = 64
MLA_V = 128
MLA_Q_RANK = 512
MLA_KV_RANK = 512
MLA_QK_PAD = 256
MLA_SCALE = (MLA_NOPE + MLA_ROPE) ** -0.5
ROPE_BASE = 10000.0

NA_HEADS = 4
NA_HEAD_DIM = 128
NA_KH = 8
NA_KW = 16
NA_SCALE = NA_HEAD_DIM ** -0.5
NA_QROWS = 8
NA_KROWS = NA_QROWS + NA_KH - 1

FN_GROUPS = 4
FN_CH = 128
FN_WIDTH = FN_GROUPS * FN_CH
FN_K2 = 16

MLA_WIDTH = MLA_HEADS * MLA_V
NA_WIDTH = NA_HEADS * NA_HEAD_DIM
D_FF = 5632
FFN_TILE = 128
CONV_W = 3

ROW_BLOCK = 512
FFN_ROWS = 1024
FFN_COLS = 1408
HALO = 16
PROJ_COLS = 512
MASK_VALUE = -1e30
LOG2E = 1.4426950408889634
VMEM_LIMIT = 60 * 1024 * 1024


def _params(*sem, flags=None):
    return pltpu.CompilerParams(dimension_semantics=sem, vmem_limit_bytes=VMEM_LIMIT, flags=flags)


def _dot(a, b):
    return jnp.dot(a, b, preferred_element_type=F32)


def _dot_nt(a, b):
    return lax.dot_general(a, b, (((1,), (1,)), ((), ())), preferred_element_type=F32)


def _split(x):
    hi = x.astype(BF16)
    lo = (x - hi.astype(F32)).astype(BF16)
    return hi, lo


def _dot3_left(m_hi, m_lo, x):
    x_hi, x_lo = _split(x)
    return _dot(m_hi, x_hi) + (_dot(m_hi, x_lo) + _dot(m_lo, x_hi))


def _dot3_right(x, m_hi, m_lo):
    x_hi, x_lo = _split(x)
    return _dot(x_hi, m_hi) + (_dot(x_lo, m_hi) + _dot(x_hi, m_lo))


def _sigmoid(x):
    return 1.0 / (1.0 + jnp.exp(-x))


def _rms(x, g):
    y = x * lax.rsqrt(jnp.mean(x * x, axis=-1, keepdims=True) + EPS)
    return y * g


def _normmod(x, g, sc, sh):
    return _rms(x, g) * (1.0 + sc) + sh


def _mod_kernel(c_ref, w_ref, b_ref, o_ref):
    c = c_ref[...]
    s = (c * _sigmoid(c)).astype(BF16)
    o_ref[...] = _dot(s, w_ref[...].astype(BF16)) + b_ref[...]


def _modulation(cin, w_mod, b_mod):
    depth, d, n = w_mod.shape
    bn = 1024
    return pl.pallas_call(
        _mod_kernel,
        grid=(depth, n // bn),
        in_specs=[
            pl.BlockSpec((8, d), lambda l, j: (0, 0)),
            pl.BlockSpec((None, d, bn), lambda l, j: (l, 0, j)),
            pl.BlockSpec((None, 1, bn), lambda l, j: (l, 0, j)),
        ],
        out_specs=pl.BlockSpec((None, 8, bn), lambda l, j: (l, 0, j)),
        out_shape=jax.ShapeDtypeStruct((depth, 8, n), F32),
        compiler_params=_params("arbitrary", "arbitrary"),
        name="modulation",
    )(cin, w_mod, b_mod.reshape(depth, 1, n))


def _inproj_kernel(x_ref, xc_ref, sh_ref, sc_ref, g_ref, w_ref, oc_ref, on_ref, of_ref, okr_ref, *, lat_blocks):
    def run(src_ref):
        h = _normmod(src_ref[...], g_ref[...], sc_ref[...], sh_ref[...]).astype(BF16)
        oc_ref[...] = _dot(h, w_ref[:, 0:1024])
        on_ref[:, 0:NA_WIDTH] = (_dot(h, w_ref[:, 1024:1024 + NA_WIDTH]) * (NA_SCALE * LOG2E)).astype(BF16)
        on_ref[:, NA_WIDTH:] = _dot(h, w_ref[:, 1024 + NA_WIDTH:2560]).astype(BF16)
        of_ref[...] = _dot(h, w_ref[:, 2560:3072])
        okr_ref[...] = _dot(h, w_ref[:, 3072:3200])

    i = pl.program_id(0)
    pl.when(i < lat_blocks)(lambda: run(x_ref))
    pl.when(i >= lat_blocks)(lambda: run(xc_ref))


def _mod_spec(k, blocks_per_batch):
    return pl.BlockSpec((None, 1, D_MODEL), lambda i, *_: (i // blocks_per_batch, 0, k))


def _two_source_specs(bm, width, lat_blocks, ctx_block0):
    return [pl.BlockSpec((bm, width), lambda i, *_: (jnp.minimum(i, lat_blocks - 1), 0)),
            pl.BlockSpec((bm, width), lambda i, *_: (ctx_block0 + jnp.maximum(i - lat_blocks, 0), 0))]


def _inproj(x_lat, x_ctx, ctx_block0, rows, lat_blocks, mod_l, g, w_p, blocks_per_batch):
    bm = ROW_BLOCK
    row = lambda w: pl.BlockSpec((bm, w), lambda i: (i, 0))
    return pl.pallas_call(
        functools.partial(_inproj_kernel, lat_blocks=lat_blocks),
        grid=(rows // bm,),
        in_specs=_two_source_specs(bm, D_MODEL, lat_blocks, ctx_block0) + [
            _mod_spec(0, blocks_per_batch),
            _mod_spec(1, blocks_per_batch),
            pl.BlockSpec((1, D_MODEL), lambda i: (0, 0)),
            pl.BlockSpec(w_p.shape, lambda i: (0, 0), pipeline_mode=pl.Buffered(1)),
        ],
        out_specs=[row(1024), row(1536), row(512), row(128)],
        out_shape=[
            jax.ShapeDtypeStruct((rows, 1024), F32),
            jax.ShapeDtypeStruct((rows, 1536), BF16),
            jax.ShapeDtypeStruct((rows, 512), F32),
            jax.ShapeDtypeStruct((rows, 128), F32),
        ],
        compiler_params=_params("arbitrary"),
        name="in_projection",
    )(x_lat, x_ctx, mod_l, mod_l, g, w_p)


def _rope(t, cs):
    t = t * cs
    return t + pltpu.roll(t, 64, axis=1)


def _mla_up_kernel(c_ref, kr_ref, cs_ref, gq_ref, gkv_ref, wq_ref, wkv_ref, q_ref, k_ref, v_ref):
    cs = cs_ref[...]
    cq = _rms(c_ref[:, 0:MLA_Q_RANK], gq_ref[...]).astype(BF16)
    ckv = _rms(c_ref[:, MLA_Q_RANK:MLA_Q_RANK + MLA_KV_RANK], gkv_ref[...]).astype(BF16)
    q = _dot(cq, wq_ref[...]) * (MLA_SCALE * LOG2E)
    kv = _dot(ckv, wkv_ref[...])
    lane = lax.broadcasted_iota(jnp.int32, cs.shape, 1)
    k_rope = jnp.where(lane < MLA_ROPE, _rope(kr_ref[...], cs), 0.0).astype(BF16)
    for h in range(MLA_HEADS):
        o = h * MLA_QK_PAD
        q_ref[:, o:o + MLA_NOPE] = q[:, o:o + MLA_NOPE].astype(BF16)
        q_ref[:, o + MLA_NOPE:o + MLA_QK_PAD] = _rope(q[:, o + MLA_NOPE:o + MLA_QK_PAD], cs).astype(BF16)
        k_ref[:, o:o + MLA_NOPE] = kv[:, h * MLA_NOPE:(h + 1) * MLA_NOPE].astype(BF16)
        k_ref[:, o + MLA_NOPE:o + MLA_QK_PAD] = k_rope
    v_ref[...] = kv[:, MLA_HEADS * MLA_NOPE:].astype(BF16)


def _mla_up(oc, okr, cs_tab, g_q, g_kv, wq_p, wkv_p, lat_blocks, pos_blocks):
    rows = oc.shape[0]
    bm = ROW_BLOCK
    row = lambda w: pl.BlockSpec((bm, w), lambda i: (i, 0))
    const = lambda a: pl.BlockSpec(a.shape, lambda i: (0, 0))
    cs_spec = pl.BlockSpec((bm, 128), lambda i: (jnp.where(i < lat_blocks, i % pos_blocks, pos_blocks), 0))
    qk_w = MLA_HEADS * MLA_QK_PAD
    return pl.pallas_call(
        _mla_up_kernel,
        grid=(rows // bm,),
        in_specs=[row(1024), row(128), cs_spec, const(g_q), const(g_kv), const(wq_p), const(wkv_p)],
        out_specs=[row(qk_w), row(qk_w), row(MLA_WIDTH)],
        out_shape=[
            jax.ShapeDtypeStruct((rows, qk_w), BF16),
            jax.ShapeDtypeStruct((rows, qk_w), BF16),
            jax.ShapeDtypeStruct((rows, MLA_WIDTH), BF16),
        ],
        compiler_params=_params("arbitrary"),
        name="mla_up_projection",
    )(oc, okr, cs_tab, g_q, g_kv, wq_p, wkv_p)


def _with_ones(v):
    return jnp.concatenate([v, jnp.ones_like(v)], axis=1)


def _softmax_first(s, v):
    m = jnp.max(s, axis=-1, keepdims=True)
    return m, _dot(jnp.exp2(s - m).astype(BF16), _with_ones(v))


def _softmax_next(s, s_max, v, m, acc):
    m_new = jnp.maximum(m, s_max)
    p = jnp.exp2(s - m_new)
    return m_new, jnp.exp2(m - m_new) * acc + _dot(p.astype(BF16), _with_ones(v))


def _normalised(acc):
    dv = acc.shape[1] // 2
    return acc[:, :dv] / acc[:, dv:]


def _attention_kernel(q_ref, k_ref, v_ref, kc_ref, vc_ref, o_ref, sa_ref, sb_ref, *, ck, n_main):
    q = q_ref[...]
    carry = _softmax_first(_dot_nt(q, kc_ref[...]), vc_ref[...])
    bufs = (sa_ref, sb_ref)

    def scores(c):
        s = _dot_nt(q, k_ref[c * ck:(c + 1) * ck, :])
        bufs[c % 2][...] = s
        return jnp.max(s, axis=-1, keepdims=True)

    s_max = scores(0)
    for c in range(n_main):
        nxt_max = scores(c + 1) if c + 1 < n_main else None
        carry = _softmax_next(bufs[c % 2][...], s_max, v_ref[c * ck:(c + 1) * ck, :], *carry)
        s_max = nxt_max
    o_ref[...] = _normalised(carry[1]).astype(o_ref.dtype)


def _attention(q_arr, k_arr, v_arr, *, batch, heads, dq, dv, bq, seq, ctx_len, k_col0, v_col0, name, ck=512):
    nq = seq // bq
    ctx0 = batch * seq // ctx_len
    return pl.pallas_call(
        functools.partial(_attention_kernel, ck=ck, n_main=seq // ck),
        grid=(batch, heads, nq),
        in_specs=[
            pl.BlockSpec((bq, dq), lambda b, h, i: (b * nq + i, h)),
            pl.BlockSpec((seq, dq), lambda b, h, i: (b, k_col0 + h)),
            pl.BlockSpec((seq, dv), lambda b, h, i: (b, v_col0 + h)),
            pl.BlockSpec((ctx_len, dq), lambda b, h, i: (ctx0 + b, k_col0 + h)),
            pl.BlockSpec((ctx_len, dv), lambda b, h, i: (ctx0 + b, v_col0 + h)),
        ],
        out_specs=pl.BlockSpec((bq, dv), lambda b, h, i: (b * nq + i, h)),
        out_shape=jax.ShapeDtypeStruct((batch * seq, heads * dv), BF16),
        scratch_shapes=[pltpu.VMEM((bq, ck), F32), pltpu.VMEM((bq, ck), F32)],
        compiler_params=_params("arbitrary", "arbitrary", "arbitrary"),
        name=name,
    )(q_arr, k_arr, v_arr, k_arr, v_arr)


def _context_attention_kernel(q_ref, k_ref, v_ref, o_ref):
    _, acc = _softmax_first(_dot_nt(q_ref[...], k_ref[...]), v_ref[...])
    o_ref[...] = _normalised(acc).astype(o_ref.dtype)


def _context_attention(q_arr, k_arr, v_arr, *, batch, heads, dq, dv, row0, ctx_len, k_col0, v_col0, name):
    blk0 = row0 // ctx_len
    return pl.pallas_call(
        _context_attention_kernel,
        grid=(batch, heads),
        in_specs=[
            pl.BlockSpec((ctx_len, dq), lambda b, h: (blk0 + b, h)),
            pl.BlockSpec((ctx_len, dq), lambda b, h: (blk0 + b, k_col0 + h)),
            pl.BlockSpec((ctx_len, dv), lambda b, h: (blk0 + b, v_col0 + h)),
        ],
        out_specs=pl.BlockSpec((ctx_len, dv), lambda b, h: (b, h)),
        out_shape=jax.ShapeDtypeStruct((batch * ctx_len, heads * dv), BF16),
        compiler_params=_params("arbitrary", "arbitrary"),
        name=name,
    )(q_arr, k_arr, v_arr)


def _na_kernel(q_ref, k_ref, v_ref, kc_ref, vc_ref, ta_ref, tb_ref, o_ref, *s_refs, grid_rows):
    bq = NA_QROWS * GRID_W
    nwin = NA_KROWS * GRID_W

    def scores(half, t_ref, sw_ref, sc_ref):
        r = pl.program_id(2) * 2 + half
        ks = jnp.clip(r * NA_QROWS - NA_KH // 2, 0, grid_rows - NA_KROWS)
        keys = pl.ds(pl.multiple_of(ks * GRID_W, GRID_W), nwin)
        q = q_ref[half * bq:(half + 1) * bq, :]
        s_win = _dot_nt(q, k_ref[keys, :]) + t_ref[...]
        s_ctx = _dot_nt(q, kc_ref[...])
        sw_ref[...] = s_win
        sc_ref[...] = s_ctx
        return keys, jnp.maximum(jnp.max(s_win, axis=-1, keepdims=True), jnp.max(s_ctx, axis=-1, keepdims=True))

    def finish(half, keys, m, sw_ref, sc_ref):
        p_win = jnp.exp2(sw_ref[...] - m).astype(BF16)
        p_ctx = jnp.exp2(sc_ref[...] - m).astype(BF16)
        acc = _dot(p_win, _with_ones(v_ref[keys, :])) + _dot(p_ctx, _with_ones(vc_ref[...]))
        o_ref[half * bq:(half + 1) * bq, :] = _normalised(acc).astype(o_ref.dtype)

    first = scores(0, ta_ref, *s_refs[0:2])
    second = scores(1, tb_ref, *s_refs[2:4])
    finish(0, *first, *s_refs[0:2])
    finish(1, *second, *s_refs[2:4])


def _na_table(rpb, grid_rows):
    w = np.arange(GRID_W)[:, None]
    j = np.arange(GRID_W)[None, :]
    cs = np.clip(w - NA_KW // 2, 0, GRID_W - NA_KW)
    sel_c = ((j - w + (NA_KW - 1))[:, :, None] == np.arange(2 * NA_KW - 1)).astype(np.float32)
    by_col = jnp.einsum("hpq,wjq->hpwj", rpb.astype(F32), sel_c, precision=lax.Precision.HIGHEST)
    by_col = jnp.where((j >= cs) & (j < cs + NA_KW), by_col * LOG2E, MASK_VALUE)
    masked = jnp.full((rpb.shape[0], GRID_W, GRID_W), MASK_VALUE, F32)
    tabs = []
    for r0 in (0, NA_QROWS, grid_rows - NA_QROWS):
        ks = min(max(r0 - NA_KH // 2, 0), grid_rows - NA_KROWS)
        rows = []
        for r in range(r0, r0 + NA_QROWS):
            rs = min(max(r - NA_KH // 2, 0), grid_rows - NA_KH)
            tiles = [by_col[:, kr - r + NA_KH - 1] if rs <= kr < rs + NA_KH else masked
                     for kr in range(ks, ks + NA_KROWS)]
            rows.append(jnp.concatenate(tiles, axis=-1))
        tabs.append(jnp.concatenate(rows, axis=1))
    return jnp.stack(tabs)


def _neighborhood_attention(on, table, batch, seq, ctx_len):
    grid_rows = seq // GRID_W
    nstep = grid_rows // (2 * NA_QROWS)
    bq = NA_QROWS * GRID_W
    nwin = NA_KROWS * GRID_W
    d = NA_HEAD_DIM
    ctx0 = batch * seq // ctx_len
    assert nstep >= 2
    return pl.pallas_call(
        functools.partial(_na_kernel, grid_rows=grid_rows),
        grid=(batch, NA_HEADS, nstep),
        in_specs=[
            pl.BlockSpec((2 * bq, d), lambda b, h, r: (b * nstep + r, h)),
            pl.BlockSpec((seq, d), lambda b, h, r: (b, NA_HEADS + h)),
            pl.BlockSpec((seq, d), lambda b, h, r: (b, 2 * NA_HEADS + h)),
            pl.BlockSpec((ctx_len, d), lambda b, h, r: (ctx0 + b, NA_HEADS + h)),
            pl.BlockSpec((ctx_len, d), lambda b, h, r: (ctx0 + b, 2 * NA_HEADS + h)),
            pl.BlockSpec((None, None, bq, nwin), lambda b, h, r: (jnp.where(r == 0, 0, 1), h, 0, 0)),
            pl.BlockSpec((None, None, bq, nwin), lambda b, h, r: (jnp.where(r == nstep - 1, 2, 1), h, 0, 0)),
        ],
        out_specs=pl.BlockSpec((2 * bq, d), lambda b, h, r: (b * nstep + r, h)),
        out_shape=jax.ShapeDtypeStruct((batch * seq, NA_WIDTH), BF16),
        scratch_shapes=[pltpu.VMEM((bq, nwin), F32), pltpu.VMEM((bq, ctx_len), F32)] * 2,
        compiler_params=_params("arbitrary", "arbitrary", "arbitrary"),
        name="neighborhood_attention",
    )(on, on, on, on, on, table, table)


def _dft_consts(n):
    jk = (np.arange(n)[:, None] * np.arange(n)[None, :]) % n
    ang = 2.0 * np.pi * jk / n
    return np.cos(ang), np.sin(ang)


def _hi_lo(m):
    m = jnp.asarray(m, F32)
    hi = m.astype(BF16)
    return hi, (m - hi.astype(F32)).astype(BF16)


def _fn_stage1_kernel(x_ref, fh_ref, fl_ref, tc_ref, ts_ref, o_ref):
    n = GRID_W
    b = _dot3_left(fh_ref[...], fl_ref[...], x_ref[...])
    br, bi = b[:n], b[n:]
    tc, ts = tc_ref[...], ts_ref[...]
    o_ref[0] = br * tc + bi * ts
    o_ref[1] = bi * tc - br * ts


def _fn_stage2_kernel(t_ref, gh_ref, gl_ref, ch_ref, cl_ref, w_ref, p_ref, o_ref):
    n = GRID_W
    outs = []
    for j in range(FN_K2):
        t = jnp.concatenate([t_ref[0, j], t_ref[1, j]], axis=0)
        a = _dot3_left(gh_ref[...], gl_ref[...], t)
        outs.append(a)
    ar = jnp.concatenate([a[:n] for a in outs], axis=0)
    ai = jnp.concatenate([a[n:] for a in outs], axis=0)
    ys = []
    for g in range(FN_GROUPS):
        sl = slice(g * FN_CH, (g + 1) * FN_CH)
        z = jnp.concatenate([ar[:, sl], ai[:, sl]], axis=1)
        spec = _dot3_right(z, ch_ref[...], cl_ref[...])
        ys.append(_dot(spec.astype(BF16), w_ref[g]).astype(BF16))
    y = jnp.concatenate(ys, axis=1)
    y = _dot(p_ref[...], y).astype(BF16)
    o_ref[...] = y.reshape(n, FN_K2, FN_WIDTH)


def _fourier_latent(of, w_fnet, batch, seq):
    n = GRID_W
    assert seq == n * n
    cols = n * FN_WIDTH
    c64, s64 = _dft_consts(n)
    f1h, f1l = _hi_lo(np.concatenate([c64, -s64], axis=0))
    tw = 2.0 * np.pi * (np.arange(n)[:, None] * np.arange(n)[None, :]) / (n * n)
    tc = jnp.asarray(np.repeat(np.cos(tw).reshape(n, n, 1), FN_WIDTH, axis=2).reshape(n, cols), F32)
    ts = jnp.asarray(np.repeat(np.sin(tw).reshape(n, n, 1), FN_WIDTH, axis=2).reshape(n, cols), F32)
    x2 = of.reshape(of.shape[0] // n, cols)
    cb = 4096
    t = pl.pallas_call(
        _fn_stage1_kernel,
        grid=(batch, cols // cb),
        in_specs=[
            pl.BlockSpec((n, cb), lambda b, j: (b, j)),
            pl.BlockSpec(f1h.shape, lambda b, j: (0, 0)),
            pl.BlockSpec(f1l.shape, lambda b, j: (0, 0)),
            pl.BlockSpec((n, cb), lambda b, j: (0, j)),
            pl.BlockSpec((n, cb), lambda b, j: (0, j)),
        ],
        out_specs=pl.BlockSpec((None, 2, n, cb), lambda b, j: (b, 0, 0, j)),
        out_shape=jax.ShapeDtypeStruct((batch, 2, n, cols), F32),
        compiler_params=_params("arbitrary", "arbitrary"),
        name="fourier_rows",
    )(x2, f1h, f1l, tc, ts)
    t = t.reshape(batch, 2, n, n, FN_WIDTH)

    g2h, g2l = _hi_lo(np.block([[c64, s64], [-s64, c64]]))
    cc, sc = _dft_consts(FN_CH)
    norm = 1.0 / np.sqrt(seq * FN_CH)
    c4h, c4l = _hi_lo(np.concatenate([cc, sc], axis=0) * norm)
    rows = FN_K2 * n
    perm = np.zeros((rows, rows), np.float32)
    k1 = np.arange(n)[:, None]
    j = np.arange(FN_K2)[None, :]
    perm[(k1 * FN_K2 + j).ravel(), (j * n + k1).ravel()] = 1.0
    perm = jnp.asarray(perm, BF16)
    const = lambda a: pl.BlockSpec(a.shape, lambda b, i: (0,) * a.ndim)
    y = pl.pallas_call(
        _fn_stage2_kernel,
        grid=(batch, n // FN_K2),
        in_specs=[
            pl.BlockSpec((None, 2, FN_K2, n, FN_WIDTH), lambda b, i: (b, 0, i, 0, 0)),
            const(g2h), const(g2l), const(c4h), const(c4l), const(w_fnet), const(perm),
        ],
        out_specs=pl.BlockSpec((None, n, FN_K2, FN_WIDTH), lambda b, i: (b, 0, i, 0)),
        out_shape=jax.ShapeDtypeStruct((batch, n, n, FN_WIDTH), BF16),
        compiler_params=_params("arbitrary", "arbitrary"),
        name="fourier_cols_channels",
    )(t, g2h, g2l, c4h, c4l, w_fnet, perm)
    return y.reshape(batch * seq, FN_WIDTH)


def _fn_ctx_kernel(x_ref, fh_ref, fl_ref, ch_ref, cl_ref, w_ref, o_ref, *, n):
    a = _dot3_left(fh_ref[...], fl_ref[...], x_ref[...])
    ar, ai = a[:n], a[n:]
    ys = []
    for g in range(FN_GROUPS):
        sl = slice(g * FN_CH, (g + 1) * FN_CH)
        z = jnp.concatenate([ar[:, sl], ai[:, sl]], axis=1)
        spec = _dot3_right(z, ch_ref[...], cl_ref[...])
        ys.append(_dot(spec.astype(BF16), w_ref[g]).astype(BF16))
    o_ref[...] = jnp.concatenate(ys, axis=1)


def _fourier_context(of, w_fnet, batch, row0, n):
    c, s = _dft_consts(n)
    fh, fl = _hi_lo(np.concatenate([c, -s], axis=0))
    cc, sc = _dft_consts(FN_CH)
    c4h, c4l = _hi_lo(np.concatenate([cc, sc], axis=0) / np.sqrt(n * FN_CH))
    const = lambda a: pl.BlockSpec(a.shape, lambda b: (0,) * a.ndim)
    blk0 = row0 // n
    return pl.pallas_call(
        functools.partial(_fn_ctx_kernel, n=n),
        grid=(batch,),
        in_specs=[pl.BlockSpec((n, FN_WIDTH), lambda b: (blk0 + b, 0)),
                  const(fh), const(fl), const(c4h), const(c4l), const(w_fnet)],
        out_specs=pl.BlockSpec((n, FN_WIDTH), lambda b: (b, 0)),
        out_shape=jax.ShapeDtypeStruct((batch * n, FN_WIDTH), BF16),
        compiler_params=_params("arbitrary"),
        name="fourier_context",
    )(of, fh, fl, c4h, c4l, w_fnet)


def _proj_residual_kernel(*refs, n_in, lat_blocks, has_ctx, split_x, final_norm):
    lat = refs[:n_in]
    ctx = refs[n_in:2 * n_in] if has_ctx else ()
    rest = refs[(2 if has_ctx else 1) * n_in:]
    w_refs, rest = rest[:n_in], rest[n_in:]
    x_ref, xc_ref = (rest[0], rest[1]) if split_x else (rest[0], rest[0])
    gt_ref, *gf_ref, o_ref = rest[2 if split_x else 1:]

    def run(a_refs, res_ref):
        for n in range(0, D_MODEL, PROJ_COLS):
            cols = slice(n, n + PROJ_COLS)
            acc = None
            for a_ref, w_ref in zip(a_refs, w_refs):
                t = _dot(a_ref[...], w_ref[:, cols])
                acc = t if acc is None else acc + t
            o_ref[:, cols] = res_ref[:, cols] + gt_ref[:, cols] * acc

    if has_ctx:
        i = pl.program_id(0)
        pl.when(i < lat_blocks)(lambda: run(lat, x_ref))
        pl.when(i >= lat_blocks)(lambda: run(ctx, xc_ref))
    else:
        run(lat, x_ref)
    if final_norm:
        o_ref[...] = _rms(o_ref[...], gf_ref[0][...])


def _proj_residual(xa, lat_ins, ctx_ins, w, l, mod_l, gate_k, rows_per_batch, n_rows, bm, final_gain=None,
                   x_ctx=None):
    n_in = len(lat_ins)
    has_ctx = ctx_ins is not None
    split_x = x_ctx is not None
    lat_blocks = lat_ins[0].shape[0] // bm
    bpb = rows_per_batch // bm
    in_specs = [pl.BlockSpec((bm, a.shape[1]), lambda i: (jnp.minimum(i, lat_blocks - 1), 0)) for a in lat_ins]
    if has_ctx:
        in_specs += [pl.BlockSpec((bm, a.shape[1]), lambda i: (jnp.maximum(i - lat_blocks, 0), 0)) for a in ctx_ins]
    offs = np.cumsum([0] + [a.shape[1] for a in lat_ins])
    assert offs[-1] == w.shape[1] and all(o % a.shape[1] == 0 for o, a in zip(offs, lat_ins))
    in_specs += [pl.BlockSpec((None, a.shape[1], D_MODEL), lambda i, k=int(o) // a.shape[1]: (l, k, 0),
                              pipeline_mode=pl.Buffered(1)) for o, a in zip(offs, lat_ins)]
    x_index = len(in_specs)
    if split_x:
        assert has_ctx and xa.shape[0] == lat_blocks * bm
        in_specs += _two_source_specs(bm, D_MODEL, lat_blocks, 0)
    else:
        in_specs.append(pl.BlockSpec((bm, D_MODEL), lambda i: (i, 0)))
    in_specs.append(pl.BlockSpec((None, 1, D_MODEL), lambda i: (i // bpb, 0, gate_k)))
    args = list(lat_ins) + (list(ctx_ins) if has_ctx else []) + [w] * n_in + [xa] + ([x_ctx] if split_x else [])
    args.append(mod_l)
    final_norm = final_gain is not None
    if final_norm:
        in_specs.append(pl.BlockSpec((1, D_MODEL), lambda i: (0, 0)))
        args.append(final_gain)
    in_place = not (final_norm or split_x)
    return pl.pallas_call(
        functools.partial(_proj_residual_kernel, n_in=n_in, lat_blocks=lat_blocks, has_ctx=has_ctx,
                          split_x=split_x, final_norm=final_norm),
        grid=(n_rows // bm,),
        in_specs=in_specs,
        out_specs=pl.BlockSpec((bm, D_MODEL), lambda i: (i, 0)),
        out_shape=jax.ShapeDtypeStruct(xa.shape if in_place else (n_rows, D_MODEL), F32),
        input_output_aliases={x_index: 0} if in_place else {},
        compiler_params=_params("arbitrary"),
        name="projection_residual",
    )(*args)


def _ffn_up_kernel(xm_ref, xp_ref, xn_ref, sh_ref, sc_ref, g_ref, wg_ref, wv_ref, cwg_ref, cwv_ref, cbg_ref,
                   cbv_ref, o_ref, h_ref, *u_refs, lat_blocks, seq, ctx_len):
    bm = o_ref.shape[0]
    i = pl.program_id(0)

    @pl.when(pl.program_id(1) == 0)
    def _():
        g, sc, sh = g_ref[...], sc_ref[...], sh_ref[...]
        h_ref[0:HALO, :] = _normmod(xp_ref[...], g, sc, sh).astype(BF16)
        h_ref[HALO:HALO + bm, :] = _normmod(xm_ref[...], g, sc, sh).astype(BF16)
        h_ref[HALO + bm:, :] = _normmod(xn_ref[...], g, sc, sh).astype(BF16)

    t = FFN_TILE
    n_tiles = o_ref.shape[1] // t

    def pair(g_ref, v_ref, rows, k):
        cols = slice(t * k, t * (k + 1))
        return jnp.concatenate([g_ref[rows, cols], v_ref[rows, cols]], axis=1)

    def matmul(k):
        u_refs[k % 2][...] = _dot(h_ref[...], pair(wg_ref, wv_ref, slice(None), k))

    def run(interior_boundaries):
        if interior_boundaries:
            rows = lax.broadcasted_iota(jnp.int32, (bm, 1), 0)
            first = (rows & (ctx_len - 1)) == 0
            last = ((rows + 1) & (ctx_len - 1)) == 0
        else:
            starts = (i * bm) % seq == 0
            ends = ((i + 1) * bm) % seq == 0
        matmul(0)
        for k in range(n_tiles):
            if k + 1 < n_tiles:
                matmul(k + 1)
            src = u_refs[k % 2]
            if interior_boundaries:
                prev = jnp.where(first, 0.0, src[HALO - 1:HALO - 1 + bm, :])
                nxt = jnp.where(last, 0.0, src[HALO + 1:HALO + 1 + bm, :])
            else:
                src[HALO - 1:HALO, :] = jnp.where(starts, 0.0, src[HALO - 1:HALO, :])
                src[HALO + bm:HALO + bm + 1, :] = jnp.where(ends, 0.0, src[HALO + bm:HALO + bm + 1, :])
                prev = src[HALO - 1:HALO - 1 + bm, :]
                nxt = src[HALO + 1:HALO + 1 + bm, :]
            c = (prev * pair(cwg_ref, cwv_ref, slice(0, 1), k)
                 + src[HALO:HALO + bm, :] * pair(cwg_ref, cwv_ref, slice(1, 2), k)
                 + nxt * pair(cwg_ref, cwv_ref, slice(2, 3), k) + pair(cbg_ref, cbv_ref, slice(None), k))
            gate, val = c[:, :t], c[:, t:]
            o_ref[:, t * k:t * (k + 1)] = (gate * _sigmoid(gate) * val).astype(BF16)

    pl.when(i < lat_blocks)(lambda: run(False))
    pl.when(i >= lat_blocks)(lambda: run(True))


def _ffn_up(xa, mod_l, g, l, w_up, conv_w, conv_b, n_rows, n_lat, seq, ctx_len):
    rows = xa.shape[0]
    bm, bn = FFN_ROWS, FFN_COLS
    assert seq % bm == 0 and bm % ctx_len == 0 and n_lat % bm == 0 and n_rows % bm == 0
    blocks_per_batch, n_blocks, lat_blocks = seq // bm, n_rows // bm, n_lat // bm
    nj = D_FF // bn
    hb = bm // HALO
    last_halo = rows // HALO - 1
    assert seq & (seq - 1) == 0 and ctx_len & (ctx_len - 1) == 0
    return pl.pallas_call(
        functools.partial(_ffn_up_kernel, lat_blocks=lat_blocks, seq=seq, ctx_len=ctx_len),
        grid=(n_blocks, nj),
        in_specs=[
            pl.BlockSpec((bm, D_MODEL), lambda i, j: (i, 0)),
            pl.BlockSpec((HALO, D_MODEL), lambda i, j: (jnp.maximum(i * hb - 1, 0), 0)),
            pl.BlockSpec((HALO, D_MODEL), lambda i, j: (jnp.minimum((i + 1) * hb, last_halo), 0)),
            _mod_spec(3, blocks_per_batch),
            _mod_spec(4, blocks_per_batch),
            pl.BlockSpec((1, D_MODEL), lambda i, j: (0, 0)),
            pl.BlockSpec((None, D_MODEL, bn), lambda i, j: (l, 0, j)),
            pl.BlockSpec((None, D_MODEL, bn), lambda i, j: (l, 0, nj + j)),
            pl.BlockSpec((None, CONV_W, bn), lambda i, j: (l, 0, j)),
            pl.BlockSpec((None, CONV_W, bn), lambda i, j: (l, 0, nj + j)),
            pl.BlockSpec((None, 1, bn), lambda i, j: (l, 0, j)),
            pl.BlockSpec((None, 1, bn), lambda i, j: (l, 0, nj + j)),
        ],
        out_specs=pl.BlockSpec((bm, bn), lambda i, j: (i, j)),
        out_shape=jax.ShapeDtypeStruct((n_blocks * bm, D_FF), BF16),
        scratch_shapes=[pltpu.VMEM((bm + 2 * HALO, D_MODEL), BF16)]
        + [pltpu.VMEM((bm + 2 * HALO, 2 * FFN_TILE), F32)] * 2,
        compiler_params=_params("arbitrary", "arbitrary"),
        name="ffn_up_conv_gate",
    )(xa, xa, xa, mod_l, mod_l, g, w_up, w_up, conv_w, conv_w, conv_b, conv_b)


def _rope_table(seq, pad_rows):
    n = MLA_ROPE // 4
    freqs = ROPE_BASE ** (-jnp.arange(n, dtype=F32) / n)
    pos = jnp.arange(seq)
    ang_r = (pos // GRID_W).astype(F32)[:, None] * freqs
    ang_c = (pos % GRID_W).astype(F32)[:, None] * freqs
    cos = jnp.concatenate([jnp.cos(ang_r)] * 2 + [jnp.cos(ang_c)] * 2, axis=1)
    sin = jnp.concatenate([-jnp.sin(ang_r), jnp.sin(ang_r), -jnp.sin(ang_c), jnp.sin(ang_c)], axis=1)
    lat = jnp.concatenate([cos, sin], axis=1)
    ident = jnp.concatenate([jnp.ones((pad_rows, MLA_ROPE), F32), jnp.zeros((pad_rows, MLA_ROPE), F32)], axis=1)
    return jnp.concatenate([lat, ident], axis=0)


def _partner_perm():
    q = MLA_ROPE // 4
    return np.concatenate([np.arange(q, 2 * q), np.arange(0, q), np.arange(3 * q, 4 * q), np.arange(2 * q, 3 * q)])


def _layout_w_in(w, l):
    offs = np.cumsum((0, MLA_Q_RANK, MLA_KV_RANK, MLA_ROPE, NA_WIDTH, NA_WIDTH, NA_WIDTH, FN_WIDTH))
    cq, ckv, kr, qn, kn, vn, f = (w[l, :, a:b].astype(BF16) for a, b in zip(offs[:-1], offs[1:]))
    return jnp.concatenate([cq, ckv, qn, kn, vn, f, kr, kr[:, _partner_perm()]], axis=1)


def _layout_w_uq(w):
    w = w.reshape(MLA_Q_RANK, MLA_HEADS, MLA_NOPE + MLA_ROPE)
    rope = w[:, :, MLA_NOPE:]
    return jnp.concatenate([w, rope[:, :, _partner_perm()]], axis=2).reshape(MLA_Q_RANK, -1).astype(BF16)


def _layout_w_ukv(w):
    w = w.reshape(MLA_KV_RANK, MLA_HEADS, MLA_NOPE + MLA_V)
    return jnp.concatenate([w[:, :, :MLA_NOPE].reshape(MLA_KV_RANK, -1),
                            w[:, :, MLA_NOPE:].reshape(MLA_KV_RANK, -1)], axis=1).astype(BF16)


def kernel(x, c, ctx, c_ctx, w_mod, b_mod, g_attn, g_ffn, w_in, g_q, w_uq, g_kv, w_ukv, na_rpb, w_fnet, w_out,
           w_up, conv_w, conv_b, w_down, g_final):
    batch, seq, d = x.shape
    ctx_len = ctx.shape[1]
    depth = w_mod.shape[0]
    n_lat, n_ctx = batch * seq, batch * ctx_len
    bm = ROW_BLOCK
    lat_blocks = n_lat // bm
    all_blocks = (n_lat + n_ctx) // bm
    blocks_per_batch = seq // bm
    assert n_ctx % bm == 0 and n_lat // bm // blocks_per_batch == batch and batch < 8

    x_lat, x_ctx, ctx_block0 = x.reshape(n_lat, d), ctx.reshape(n_ctx, d), 0
    cin =jnp.zeros((8, d), F32).at[:batch].set(c).at[batch].set(c_ctx)
    mod = _modulation(cin, w_mod, b_mod)
    cs_tab = _rope_table(seq, bm)
    row = lambda v: v.reshape(1, -1)
    w_out_bf, w_up_bf, w_down_bf = w_out.astype(BF16), w_up.astype(BF16), w_down.astype(BF16)
    conv_b3 = conv_b.reshape(depth, 1, -1)

    for l in range(depth):
        ctx_out = l < depth - 1
        mod_l = mod[l].reshape(8, 1, 6 * d)
        oc, on, of, okr = _inproj(x_lat, x_ctx, ctx_block0, n_lat + n_ctx, lat_blocks, mod_l, row(g_attn[l]),
                                  _layout_w_in(w_in, l), blocks_per_batch)
        qm, km, vm = _mla_up(oc, okr, cs_tab, row(g_q[l]), row(g_kv[l]), _layout_w_uq(w_uq[l]),
                             _layout_w_ukv(w_ukv[l]), lat_blocks, blocks_per_batch)

        o_mla = _attention(qm, km, vm, batch=batch, heads=MLA_HEADS, dq=MLA_QK_PAD, dv=MLA_V, bq=2048, ck=1024,
                           seq=seq, ctx_len=ctx_len, k_col0=0, v_col0=0, name="mla_attention")
        o_na = _neighborhood_attention(on, _na_table(na_rpb[l], seq // GRID_W), batch, seq, ctx_len)
        o_fn = _fourier_latent(of, w_fnet[l].astype(BF16), batch, seq)
        ctx_ins = None
        if ctx_out:
            o_mla_c = _context_attention(
                qm, km, vm, batch=batch, heads=MLA_HEADS, dq=MLA_QK_PAD, dv=MLA_V, row0=n_lat, ctx_len=ctx_len,
                k_col0=0, v_col0=0, name="mla_attention_context")
            o_na_c = _context_attention(
                on, on, on, batch=batch, heads=NA_HEADS, dq=NA_HEAD_DIM, dv=NA_HEAD_DIM, row0=n_lat,
                ctx_len=ctx_len, k_col0=NA_HEADS, v_col0=2 * NA_HEADS, name="na_attention_context")
            o_fn_c = _fourier_context(of, w_fnet[l].astype(BF16), batch, n_lat, ctx_len)
            ctx_ins = (o_mla_c, o_na_c, o_fn_c)

        n_blocks = all_blocks if ctx_out else lat_blocks
        n_rows = n_blocks * bm
        unified = x_ctx is x_lat
        merge = ctx_out and not unified
        xa = _proj_residual(x_lat, (o_mla, o_na, o_fn), ctx_ins, w_out_bf, l, mod_l, 2, seq, n_rows, bm,
                            x_ctx=x_ctx if merge else None)
        a = _ffn_up(xa, mod_l, row(g_ffn[l]), l, w_up_bf, conv_w, conv_b3, n_rows, n_lat, seq, ctx_len)
        xa = _proj_residual(xa, (a,), None, w_down_bf, l, mod_l, 5, seq, n_rows, bm,
                            final_gain=None if ctx_out else row(g_final))
        if merge or unified:
            x_lat, x_ctx, ctx_block0 = xa, xa, lat_blocks
        else:
            x_lat = xa

    return xa.reshape(batch, seq, d)
```

```python
import functools

import numpy as np
import jax
import jax.numpy as jnp
from jax import lax
from jax.experimental import pallas as pl
from jax.experimental.pallas import tpu as pltpu

F32 = jnp.float32
BF16 = jnp.bfloat16

D_MODEL = 2048
GRID_W = 64
EPS = 1e-6

MLA_HEADS = 8
MLA_NOPE = 128
MLA_ROPE = 64
MLA_V = 128
MLA_Q_RANK = 512
MLA_KV_RANK = 512
MLA_QK_PAD = 256
MLA_SCALE = (MLA_NOPE + MLA_ROPE) ** -0.5
ROPE_BASE = 10000.0

NA_HEADS = 4
NA_HEAD_DIM = 128
NA_KH = 8
NA_KW = 16
NA_SCALE = NA_HEAD_DIM ** -0.5
NA_QROWS = 8
NA_KROWS = NA_QROWS + NA_KH - 1

FN_GROUPS = 4
FN_CH = 128
FN_WIDTH = FN_GROUPS * FN_CH
FN_K2 = 16

MLA_WIDTH = MLA_HEADS * MLA_V
NA_WIDTH = NA_HEADS * NA_HEAD_DIM
D_FF = 5632
FFN_TILE = 128
CONV_W = 3

ROW_BLOCK = 512
FFN_ROWS = 1024
FFN_COLS = 1408
HALO = 16
PROJ_COLS = 512
MASK_VALUE = -1e30
LOG2E = 1.4426950408889634
VMEM_LIMIT = 60 * 1024 * 1024


def _params(*sem, flags=None):
    return pltpu.CompilerParams(dimension_semantics=sem, vmem_limit_bytes=VMEM_LIMIT, flags=flags)


def _dot(a, b):
    return jnp.dot(a, b, preferred_element_type=F32)


def _dot_nt(a, b):
    return lax.dot_general(a, b, (((1,), (1,)), ((), ())), preferred_element_type=F32)


def _split(x):
    hi = x.astype(BF16)
    lo = (x - hi.astype(F32)).astype(BF16)
    return hi, lo


def _dot3_left(m_hi, m_lo, x):
    x_hi, x_lo = _split(x)
    return _dot(m_hi, x_hi) + (_dot(m_hi, x_lo) + _dot(m_lo, x_hi))


def _dot3_right(x, m_hi, m_lo):
    x_hi, x_lo = _split(x)
    return _dot(x_hi, m_hi) + (_dot(x_lo, m_hi) + _dot(x_hi, m_lo))


def _sigmoid(x):
    return 1.0 / (1.0 + jnp.exp(-x))


def _rms(x, g):
    y = x * lax.rsqrt(jnp.mean(x * x, axis=-1, keepdims=True) + EPS)
    return y * g


def _normmod(x, g, sc, sh):
    return _rms(x, g) * (1.0 + sc) + sh


def _mod_kernel(c_ref, w_ref, b_ref, o_ref):
    c = c_ref[...]
    s = (c * _sigmoid(c)).astype(BF16)
    o_ref[...] = _dot(s, w_ref[...].astype(BF16)) + b_ref[...]


def _modulation(cin, w_mod, b_mod):
    depth, d, n = w_mod.shape
    bn = 1024
    return pl.pallas_call(
        _mod_kernel,
        grid=(depth, n // bn),
        in_specs=[
            pl.BlockSpec((8, d), lambda l, j: (0, 0)),
            pl.BlockSpec((None, d, bn), lambda l, j: (l, 0, j)),
            pl.BlockSpec((None, 1, bn), lambda l, j: (l, 0, j)),
        ],
        out_specs=pl.BlockSpec((None, 8, bn), lambda l, j: (l, 0, j)),
        out_shape=jax.ShapeDtypeStruct((depth, 8, n), F32),
        compiler_params=_params("arbitrary", "arbitrary"),
        name="modulation",
    )(cin, w_mod, b_mod.reshape(depth, 1, n))


def _inproj_kernel(x_ref, xc_ref, sh_ref, sc_ref, g_ref, w_ref, oc_ref, on_ref, of_ref, okr_ref, *, lat_blocks):
    def run(src_ref):
        h = _normmod(src_ref[...], g_ref[...], sc_ref[...], sh_ref[...]).astype(BF16)
        oc_ref[...] = _dot(h, w_ref[:, 0:1024])
        on_ref[:, 0:NA_WIDTH] = (_dot(h, w_ref[:, 1024:1024 + NA_WIDTH]) * (NA_SCALE * LOG2E)).astype(BF16)
        on_ref[:, NA_WIDTH:] = _dot(h, w_ref[:, 1024 + NA_WIDTH:2560]).astype(BF16)
        of_ref[...] = _dot(h, w_ref[:, 2560:3072])
        okr_ref[...] = _dot(h, w_ref[:, 3072:3200])

    i = pl.program_id(0)
    pl.when(i < lat_blocks)(lambda: run(x_ref))
    pl.when(i >= lat_blocks)(lambda: run(xc_ref))


def _mod_spec(k, blocks_per_batch):
    return pl.BlockSpec((None, 1, D_MODEL), lambda i, *_: (i // blocks_per_batch, 0, k))


def _two_source_specs(bm, width, lat_blocks, ctx_block0):
    return [pl.BlockSpec((bm, width), lambda i, *_: (jnp.minimum(i, lat_blocks - 1), 0)),
            pl.BlockSpec((bm, width), lambda i, *_: (ctx_block0 + jnp.maximum(i - lat_blocks, 0), 0))]


def _inproj(x_lat, x_ctx, ctx_block0, rows, lat_blocks, mod_l, g, w_p, blocks_per_batch):
    bm = ROW_BLOCK
    row = lambda w: pl.BlockSpec((bm, w), lambda i: (i, 0))
    return pl.pallas_call(
        functools.partial(_inproj_kernel, lat_blocks=lat_blocks),
        grid=(rows // bm,),
        in_specs=_two_source_specs(bm, D_MODEL, lat_blocks, ctx_block0) + [
            _mod_spec(0, blocks_per_batch),
            _mod_spec(1, blocks_per_batch),
            pl.BlockSpec((1, D_MODEL), lambda i: (0, 0)),
            pl.BlockSpec(w_p.shape, lambda i: (0, 0), pipeline_mode=pl.Buffered(1)),
        ],
        out_specs=[row(1024), row(1536), row(512), row(128)],
        out_shape=[
            jax.ShapeDtypeStruct((rows, 1024), F32),
            jax.ShapeDtypeStruct((rows, 1536), BF16),
            jax.ShapeDtypeStruct((rows, 512), F32),
            jax.ShapeDtypeStruct((rows, 128), F32),
        ],
        compiler_params=_params("arbitrary"),
        name="in_projection",
    )(x_lat, x_ctx, mod_l, mod_l, g, w_p)


def _rope(t, cs):
    t = t * cs
    return t + pltpu.roll(t, 64, axis=1)


def _mla_up_kernel(c_ref, kr_ref, cs_ref, gq_ref, gkv_ref, wq_ref, wkv_ref, q_ref, k_ref, v_ref):
    cs = cs_ref[...]
    cq = _rms(c_ref[:, 0:MLA_Q_RANK], gq_ref[...]).astype(BF16)
    ckv = _rms(c_ref[:, MLA_Q_RANK:MLA_Q_RANK + MLA_KV_RANK], gkv_ref[...]).astype(BF16)
    q = _dot(cq, wq_ref[...]) * (MLA_SCALE * LOG2E)
    kv = _dot(ckv, wkv_ref[...])
    lane = lax.broadcasted_iota(jnp.int32, cs.shape, 1)
    k_rope = jnp.where(lane < MLA_ROPE, _rope(kr_ref[...], cs), 0.0).astype(BF16)
    for h in range(MLA_HEADS):
        o = h * MLA_QK_PAD
        q_ref[:, o:o + MLA_NOPE] = q[:, o:o + MLA_NOPE].astype(BF16)
        q_ref[:, o + MLA_NOPE:o + MLA_QK_PAD] = _rope(q[:, o + MLA_NOPE:o + MLA_QK_PAD], cs).astype(BF16)
        k_ref[:, o:o + MLA_NOPE] = kv[:, h * MLA_NOPE:(h + 1) * MLA_NOPE].astype(BF16)
        k_ref[:, o + MLA_NOPE:o + MLA_QK_PAD] = k_rope
    v_ref[...] = kv[:, MLA_HEADS * MLA_NOPE:].astype(BF16)


def _mla_up(oc, okr, cs_tab, g_q, g_kv, wq_p, wkv_p, lat_blocks, pos_blocks):
    rows = oc.shape[0]
    bm = ROW_BLOCK
    row = lambda w: pl.BlockSpec((bm, w), lambda i: (i, 0))
    const = lambda a: pl.BlockSpec(a.shape, lambda i: (0, 0))
    cs_spec = pl.BlockSpec((bm, 128), lambda i: (jnp.where(i < lat_blocks, i % pos_blocks, pos_blocks), 0))
    qk_w = MLA_HEADS * MLA_QK_PAD
    return pl.pallas_call(
        _mla_up_kernel,
        grid=(rows // bm,),
        in_specs=[row(1024), row(128), cs_spec, const(g_q), const(g_kv), const(wq_p), const(wkv_p)],
        out_specs=[row(qk_w), row(qk_w), row(MLA_WIDTH)],
        out_shape=[
            jax.ShapeDtypeStruct((rows, qk_w), BF16),
            jax.ShapeDtypeStruct((rows, qk_w), BF16),
            jax.ShapeDtypeStruct((rows, MLA_WIDTH), BF16),
        ],
        compiler_params=_params("arbitrary"),
        name="mla_up_projection",
    )(oc, okr, cs_tab, g_q, g_kv, wq_p, wkv_p)


def _with_ones(v):
    return jnp.concatenate([v, jnp.ones_like(v)], axis=1)


def _softmax_first(s, v, s_max=None):
    m = jnp.max(s, axis=-1, keepdims=True) if s_max is None else s_max
    return m, _dot(jnp.exp2(s - m).astype(BF16), _with_ones(v))


def _softmax_next(s, s_max, v, m, acc):
    m_new = jnp.maximum(m, s_max)
    p = jnp.exp2(s - m_new)
    return m_new, jnp.exp2(m - m_new) * acc + _dot(p.astype(BF16), _with_ones(v))


def _normalised(acc):
    dv = acc.shape[1] // 2
    return acc[:, :dv] / acc[:, dv:]


def _attention_kernel(q_ref, k_ref, v_ref, kc_ref, vc_ref, o_ref, sa_ref, sb_ref, *, ck, n_main):
    q = q_ref[...]
    bufs = (sa_ref, sb_ref)
    chunks = [(k_ref, v_ref, c * ck, ck) for c in range(n_main)] + [(kc_ref, vc_ref, 0, kc_ref.shape[0])]

    def scores(i):
        keys, _, start, size = chunks[i]
        s = _dot_nt(q, keys[start:start + size, :])
        bufs[i % 2][:, :size] = s
        return jnp.max(s, axis=-1, keepdims=True)

    s_max = scores(0)
    carry = None
    for i, (_, values, start, size) in enumerate(chunks):
        nxt_max = scores(i + 1) if i + 1 < len(chunks) else None
        s, v = bufs[i % 2][:, :size], values[start:start + size, :]
        carry = _softmax_first(s, v, s_max) if carry is None else _softmax_next(s, s_max, v, *carry)
        s_max = nxt_max
    o_ref[...] = _normalised(carry[1]).astype(o_ref.dtype)


def _attention(q_arr, k_arr, v_arr, *, batch, heads, dq, dv, bq, seq, ctx_len, k_col0, v_col0, name, ck=512):
    nq = seq // bq
    ctx0 = batch * seq // ctx_len
    return pl.pallas_call(
        functools.partial(_attention_kernel, ck=ck, n_main=seq // ck),
        grid=(batch, heads, nq),
        in_specs=[
            pl.BlockSpec((bq, dq), lambda b, h, i: (b * nq + i, h)),
            pl.BlockSpec((seq, dq), lambda b, h, i: (b, k_col0 + h)),
            pl.BlockSpec((seq, dv), lambda b, h, i: (b, v_col0 + h)),
            pl.BlockSpec((ctx_len, dq), lambda b, h, i: (ctx0 + b, k_col0 + h)),
            pl.BlockSpec((ctx_len, dv), lambda b, h, i: (ctx0 + b, v_col0 + h)),
        ],
        out_specs=pl.BlockSpec((bq, dv), lambda b, h, i: (b * nq + i, h)),
        out_shape=jax.ShapeDtypeStruct((batch * seq, heads * dv), BF16),
        scratch_shapes=[pltpu.VMEM((bq, ck), F32), pltpu.VMEM((bq, ck), F32)],
        compiler_params=_params("arbitrary", "arbitrary", "arbitrary"),
        name=name,
    )(q_arr, k_arr, v_arr, k_arr, v_arr)


def _context_attention_kernel(q_ref, k_ref, v_ref, o_ref):
    _, acc = _softmax_first(_dot_nt(q_ref[...], k_ref[...]), v_ref[...])
    o_ref[...] = _normalised(acc).astype(o_ref.dtype)


def _context_attention(q_arr, k_arr, v_arr, *, batch, heads, dq, dv, row0, ctx_len, k_col0, v_col0, name):
    blk0 = row0 // ctx_len
    return pl.pallas_call(
        _context_attention_kernel,
        grid=(batch, heads),
        in_specs=[
            pl.BlockSpec((ctx_len, dq), lambda b, h: (blk0 + b, h)),
            pl.BlockSpec((ctx_len, dq), lambda b, h: (blk0 + b, k_col0 + h)),
            pl.BlockSpec((ctx_len, dv), lambda b, h: (blk0 + b, v_col0 + h)),
        ],
        out_specs=pl.BlockSpec((ctx_len, dv), lambda b, h: (b, h)),
        out_shape=jax.ShapeDtypeStruct((batch * ctx_len, heads * dv), BF16),
        compiler_params=_params("arbitrary", "arbitrary"),
        name=name,
    )(q_arr, k_arr, v_arr)


def _na_kernel(q_ref, k_ref, v_ref, kc_ref, vc_ref, ta_ref, tb_ref, o_ref, *s_refs, grid_rows):
    bq = NA_QROWS * GRID_W
    nwin = NA_KROWS * GRID_W

    def scores(half, t_ref, sw_ref, sc_ref):
        r = pl.program_id(2) * 2 + half
        ks = jnp.clip(r * NA_QROWS - NA_KH // 2, 0, grid_rows - NA_KROWS)
        keys = pl.ds(pl.multiple_of(ks * GRID_W, GRID_W), nwin)
        q = q_ref[half * bq:(half + 1) * bq, :]
        s_win = _dot_nt(q, k_ref[keys, :]) + t_ref[...]
        s_ctx = _dot_nt(q, kc_ref[...])
        sw_ref[...] = s_win
        sc_ref[...] = s_ctx
        return keys, jnp.maximum(jnp.max(s_win, axis=-1, keepdims=True), jnp.max(s_ctx, axis=-1, keepdims=True))

    def finish(half, keys, m, sw_ref, sc_ref):
        p_win = jnp.exp2(sw_ref[...] - m).astype(BF16)
        p_ctx = jnp.exp2(sc_ref[...] - m).astype(BF16)
        acc = _dot(p_win, _with_ones(v_ref[keys, :])) + _dot(p_ctx, _with_ones(vc_ref[...]))
        o_ref[half * bq:(half + 1) * bq, :] = _normalised(acc).astype(o_ref.dtype)

    first = scores(0, ta_ref, *s_refs[0:2])
    second = scores(1, tb_ref, *s_refs[2:4])
    finish(0, *first, *s_refs[0:2])
    finish(1, *second, *s_refs[2:4])


def _na_table(rpb, grid_rows):
    w = np.arange(GRID_W)[:, None]
    j = np.arange(GRID_W)[None, :]
    cs = np.clip(w - NA_KW // 2, 0, GRID_W - NA_KW)
    sel_c = ((j - w + (NA_KW - 1))[:, :, None] == np.arange(2 * NA_KW - 1)).astype(np.float32)
    by_col = jnp.einsum("hpq,wjq->hpwj", rpb.astype(F32), sel_c, precision=lax.Precision.HIGHEST)
    by_col = jnp.where((j >= cs) & (j < cs + NA_KW), by_col * LOG2E, MASK_VALUE)
    masked = jnp.full((rpb.shape[0], GRID_W, GRID_W), MASK_VALUE, F32)
    tabs = []
    for r0 in (0, NA_QROWS, grid_rows - NA_QROWS):
        ks = min(max(r0 - NA_KH // 2, 0), grid_rows - NA_KROWS)
        rows = []
        for r in range(r0, r0 + NA_QROWS):
            rs = min(max(r - NA_KH // 2, 0), grid_rows - NA_KH)
            tiles = [by_col[:, kr - r + NA_KH - 1] if rs <= kr < rs + NA_KH else masked
                     for kr in range(ks, ks + NA_KROWS)]
            rows.append(jnp.concatenate(tiles, axis=-1))
        tabs.append(jnp.concatenate(rows, axis=1))
    return jnp.stack(tabs)


def _neighborhood_attention(on, table, batch, seq, ctx_len):
    grid_rows = seq // GRID_W
    nstep = grid_rows // (2 * NA_QROWS)
    bq = NA_QROWS * GRID_W
    nwin = NA_KROWS * GRID_W
    d = NA_HEAD_DIM
    ctx0 = batch * seq // ctx_len
    assert nstep >= 2
    return pl.pallas_call(
        functools.partial(_na_kernel, grid_rows=grid_rows),
        grid=(batch, NA_HEADS, nstep),
        in_specs=[
            pl.BlockSpec((2 * bq, d), lambda b, h, r: (b * nstep + r, h)),
            pl.BlockSpec((seq, d), lambda b, h, r: (b, NA_HEADS + h)),
            pl.BlockSpec((seq, d), lambda b, h, r: (b, 2 * NA_HEADS + h)),
            pl.BlockSpec((ctx_len, d), lambda b, h, r: (ctx0 + b, NA_HEADS + h)),
            pl.BlockSpec((ctx_len, d), lambda b, h, r: (ctx0 + b, 2 * NA_HEADS + h)),
            pl.BlockSpec((None, None, bq, nwin), lambda b, h, r: (jnp.where(r == 0, 0, 1), h, 0, 0)),
            pl.BlockSpec((None, None, bq, nwin), lambda b, h, r: (jnp.where(r == nstep - 1, 2, 1), h, 0, 0)),
        ],
        out_specs=pl.BlockSpec((2 * bq, d), lambda b, h, r: (b * nstep + r, h)),
        out_shape=jax.ShapeDtypeStruct((batch * seq, NA_WIDTH), BF16),
        scratch_shapes=[pltpu.VMEM((bq, nwin), F32), pltpu.VMEM((bq, ctx_len), F32)] * 2,
        compiler_params=_params("arbitrary", "arbitrary", "arbitrary"),
        name="neighborhood_attention",
    )(on, on, on, on, on, table, table)


def _dft_consts(n):
    jk = (np.arange(n)[:, None] * np.arange(n)[None, :]) % n
    ang = 2.0 * np.pi * jk / n
    return np.cos(ang), np.sin(ang)


def _hi_lo(m):
    m = jnp.asarray(m, F32)
    hi = m.astype(BF16)
    return hi, (m - hi.astype(F32)).astype(BF16)


def _fn_stage1_kernel(x_ref, fh_ref, fl_ref, tc_ref, ts_ref, o_ref):
    n = GRID_W
    b = _dot3_left(fh_ref[...], fl_ref[...], x_ref[...])
    br, bi = b[:n], b[n:]
    tc, ts = tc_ref[...], ts_ref[...]
    o_ref[0] = br * tc + bi * ts
    o_ref[1] = bi * tc - br * ts


def _fn_stage2_kernel(t_ref, gh_ref, gl_ref, ch_ref, cl_ref, w_ref, p_ref, o_ref):
    n = GRID_W
    outs = []
    for j in range(FN_K2):
        t = jnp.concatenate([t_ref[0, j], t_ref[1, j]], axis=0)
        a = _dot3_left(gh_ref[...], gl_ref[...], t)
        outs.append(a)
    ar = jnp.concatenate([a[:n] for a in outs], axis=0)
    ai = jnp.concatenate([a[n:] for a in outs], axis=0)
    ys = []
    for g in range(FN_GROUPS):
        sl = slice(g * FN_CH, (g + 1) * FN_CH)
        z = jnp.concatenate([ar[:, sl], ai[:, sl]], axis=1)
        spec = _dot3_right(z, ch_ref[...], cl_ref[...])
        ys.append(_dot(spec.astype(BF16), w_ref[g]).astype(BF16))
    y = jnp.concatenate(ys, axis=1)
    y = _dot(p_ref[...], y).astype(BF16)
    o_ref[...] = y.reshape(n, FN_K2, FN_WIDTH)


def _fourier_latent(of, w_fnet, batch, seq):
    n = GRID_W
    assert seq == n * n
    cols = n * FN_WIDTH
    c64, s64 = _dft_consts(n)
    f1h, f1l = _hi_lo(np.concatenate([c64, -s64], axis=0))
    tw = 2.0 * np.pi * (np.arange(n)[:, None] * np.arange(n)[None, :]) / (n * n)
    tc = jnp.asarray(np.repeat(np.cos(tw).reshape(n, n, 1), FN_WIDTH, axis=2).reshape(n, cols), F32)
    ts = jnp.asarray(np.repeat(np.sin(tw).reshape(n, n, 1), FN_WIDTH, axis=2).reshape(n, cols), F32)
    x2 = of.reshape(of.shape[0] // n, cols)
    cb = 4096
    t = pl.pallas_call(
        _fn_stage1_kernel,
        grid=(batch, cols // cb),
        in_specs=[
            pl.BlockSpec((n, cb), lambda b, j: (b, j)),
            pl.BlockSpec(f1h.shape, lambda b, j: (0, 0)),
            pl.BlockSpec(f1l.shape, lambda b, j: (0, 0)),
            pl.BlockSpec((n, cb), lambda b, j: (0, j)),
            pl.BlockSpec((n, cb), lambda b, j: (0, j)),
        ],
        out_specs=pl.BlockSpec((None, 2, n, cb), lambda b, j: (b, 0, 0, j)),
        out_shape=jax.ShapeDtypeStruct((batch, 2, n, cols), F32),
        compiler_params=_params("arbitrary", "arbitrary"),
        name="fourier_rows",
    )(x2, f1h, f1l, tc, ts)
    t = t.reshape(batch, 2, n, n, FN_WIDTH)

    g2h, g2l = _hi_lo(np.block([[c64, s64], [-s64, c64]]))
    cc, sc = _dft_consts(FN_CH)
    norm = 1.0 / np.sqrt(seq * FN_CH)
    c4h, c4l = _hi_lo(np.concatenate([cc, sc], axis=0) * norm)
    rows = FN_K2 * n
    perm = np.zeros((rows, rows), np.float32)
    k1 = np.arange(n)[:, None]
    j = np.arange(FN_K2)[None, :]
    perm[(k1 * FN_K2 + j).ravel(), (j * n + k1).ravel()] = 1.0
    perm = jnp.asarray(perm, BF16)
    const = lambda a: pl.BlockSpec(a.shape, lambda b, i: (0,) * a.ndim)
    y = pl.pallas_call(
        _fn_stage2_kernel,
        grid=(batch, n // FN_K2),
        in_specs=[
            pl.BlockSpec((None, 2, FN_K2, n, FN_WIDTH), lambda b, i: (b, 0, i, 0, 0)),
            const(g2h), const(g2l), const(c4h), const(c4l), const(w_fnet), const(perm),
        ],
        out_specs=pl.BlockSpec((None, n, FN_K2, FN_WIDTH), lambda b, i: (b, 0, i, 0)),
        out_shape=jax.ShapeDtypeStruct((batch, n, n, FN_WIDTH), BF16),
        compiler_params=_params("arbitrary", "arbitrary"),
        name="fourier_cols_channels",
    )(t, g2h, g2l, c4h, c4l, w_fnet, perm)
    return y.reshape(batch * seq, FN_WIDTH)


def _fn_ctx_kernel(x_ref, fh_ref, fl_ref, ch_ref, cl_ref, w_ref, o_ref, *, n):
    a = _dot3_left(fh_ref[...], fl_ref[...], x_ref[...])
    ar, ai = a[:n], a[n:]
    ys = []
    for g in range(FN_GROUPS):
        sl = slice(g * FN_CH, (g + 1) * FN_CH)
        z = jnp.concatenate([ar[:, sl], ai[:, sl]], axis=1)
        spec = _dot3_right(z, ch_ref[...], cl_ref[...])
        ys.append(_dot(spec.astype(BF16), w_ref[g]).astype(BF16))
    o_ref[...] = jnp.concatenate(ys, axis=1)


def _fourier_context(of, w_fnet, batch, row0, n):
    c, s = _dft_consts(n)
    fh, fl = _hi_lo(np.concatenate([c, -s], axis=0))
    cc, sc = _dft_consts(FN_CH)
    c4h, c4l = _hi_lo(np.concatenate([cc, sc], axis=0) / np.sqrt(n * FN_CH))
    const = lambda a: pl.BlockSpec(a.shape, lambda b: (0,) * a.ndim)
    blk0 = row0 // n
    return pl.pallas_call(
        functools.partial(_fn_ctx_kernel, n=n),
        grid=(batch,),
        in_specs=[pl.BlockSpec((n, FN_WIDTH), lambda b: (blk0 + b, 0)),
                  const(fh), const(fl), const(c4h), const(c4l), const(w_fnet)],
        out_specs=pl.BlockSpec((n, FN_WIDTH), lambda b: (b, 0)),
        out_shape=jax.ShapeDtypeStruct((batch * n, FN_WIDTH), BF16),
        compiler_params=_params("arbitrary"),
        name="fourier_context",
    )(of, fh, fl, c4h, c4l, w_fnet)


def _proj_residual_kernel(*refs, n_in, lat_blocks, has_ctx, split_x, final_norm):
    lat = refs[:n_in]
    ctx = refs[n_in:2 * n_in] if has_ctx else ()
    rest = refs[(2 if has_ctx else 1) * n_in:]
    w_refs, rest = rest[:n_in], rest[n_in:]
    x_ref, xc_ref = (rest[0], rest[1]) if split_x else (rest[0], rest[0])
    gt_ref, *gf_ref, o_ref = rest[2 if split_x else 1:]

    def run(a_refs, res_ref):
        for n in range(0, D_MODEL, PROJ_COLS):
            cols = slice(n, n + PROJ_COLS)
            acc = None
            for a_ref, w_ref in zip(a_refs, w_refs):
                t = _dot(a_ref[...], w_ref[:, cols])
                acc = t if acc is None else acc + t
            o_ref[:, cols] = res_ref[:, cols] + gt_ref[:, cols] * acc

    if has_ctx:
        i = pl.program_id(0)
        pl.when(i < lat_blocks)(lambda: run(lat, x_ref))
        pl.when(i >= lat_blocks)(lambda: run(ctx, xc_ref))
    else:
        run(lat, x_ref)
    if final_norm:
        o_ref[...] = _rms(o_ref[...], gf_ref[0][...])


def _proj_residual(xa, lat_ins, ctx_ins, w, l, mod_l, gate_k, rows_per_batch, n_rows, bm, final_gain=None,
                   x_ctx=None):
    n_in = len(lat_ins)
    has_ctx = ctx_ins is not None
    split_x = x_ctx is not None
    lat_blocks = lat_ins[0].shape[0] // bm
    bpb = rows_per_batch // bm
    in_specs = [pl.BlockSpec((bm, a.shape[1]), lambda i: (jnp.minimum(i, lat_blocks - 1), 0)) for a in lat_ins]
    if has_ctx:
        in_specs += [pl.BlockSpec((bm, a.shape[1]), lambda i: (jnp.maximum(i - lat_blocks, 0), 0)) for a in ctx_ins]
    offs = np.cumsum([0] + [a.shape[1] for a in lat_ins])
    assert offs[-1] == w.shape[1] and all(o % a.shape[1] == 0 for o, a in zip(offs, lat_ins))
    in_specs += [pl.BlockSpec((None, a.shape[1], D_MODEL), lambda i, k=int(o) // a.shape[1]: (l, k, 0),
                              pipeline_mode=pl.Buffered(1)) for o, a in zip(offs, lat_ins)]
    x_index = len(in_specs)
    if split_x:
        assert has_ctx and xa.shape[0] == lat_blocks * bm
        in_specs += _two_source_specs(bm, D_MODEL, lat_blocks, 0)
    else:
        in_specs.append(pl.BlockSpec((bm, D_MODEL), lambda i: (i, 0)))
    in_specs.append(pl.BlockSpec((None, 1, D_MODEL), lambda i: (i // bpb, 0, gate_k)))
    args = list(lat_ins) + (list(ctx_ins) if has_ctx else []) + [w] * n_in + [xa] + ([x_ctx] if split_x else [])
    args.append(mod_l)
    final_norm = final_gain is not None
    if final_norm:
        in_specs.append(pl.BlockSpec((1, D_MODEL), lambda i: (0, 0)))
        args.append(final_gain)
    in_place = not (final_norm or split_x)
    return pl.pallas_call(
        functools.partial(_proj_residual_kernel, n_in=n_in, lat_blocks=lat_blocks, has_ctx=has_ctx,
                          split_x=split_x, final_norm=final_norm),
        grid=(n_rows // bm,),
        in_specs=in_specs,
        out_specs=pl.BlockSpec((bm, D_MODEL), lambda i: (i, 0)),
        out_shape=jax.ShapeDtypeStruct(xa.shape if in_place else (n_rows, D_MODEL), F32),
        input_output_aliases={x_index: 0} if in_place else {},
        compiler_params=_params("arbitrary"),
        name="projection_residual",
    )(*args)


def _ffn_up_kernel(xm_ref, xp_ref, xn_ref, sh_ref, sc_ref, g_ref, wg_ref, wv_ref, cwg_ref, cwv_ref, cbg_ref,
                   cbv_ref, o_ref, h_ref, *u_refs, lat_blocks, seq, ctx_len):
    bm = o_ref.shape[0]
    i = pl.program_id(0)

    @pl.when(pl.program_id(1) == 0)
    def _():
        g, sc, sh = g_ref[...], sc_ref[...], sh_ref[...]
        h_ref[0:HALO, :] = _normmod(xp_ref[...], g, sc, sh).astype(BF16)
        h_ref[HALO:HALO + bm, :] = _normmod(xm_ref[...], g, sc, sh).astype(BF16)
        h_ref[HALO + bm:, :] = _normmod(xn_ref[...], g, sc, sh).astype(BF16)

    t = FFN_TILE
    n_tiles = o_ref.shape[1] // t

    def pair(g_ref, v_ref, rows, k):
        cols = slice(t * k, t * (k + 1))
        return jnp.concatenate([g_ref[rows, cols], v_ref[rows, cols]], axis=1)

    def matmul(k):
        u_refs[k % 2][...] = _dot(h_ref[...], pair(wg_ref, wv_ref, slice(None), k))

    def run(interior_boundaries):
        if interior_boundaries:
            rows = lax.broadcasted_iota(jnp.int32, (bm, 1), 0)
            first = (rows & (ctx_len - 1)) == 0
            last = ((rows + 1) & (ctx_len - 1)) == 0
        else:
            starts = (i * bm) % seq == 0
            ends = ((i + 1) * bm) % seq == 0
        matmul(0)
        for k in range(n_tiles):
            if k + 1 < n_tiles:
                matmul(k + 1)
            src = u_refs[k % 2]
            if interior_boundaries:
                prev = jnp.where(first, 0.0, src[HALO - 1:HALO - 1 + bm, :])
                nxt = jnp.where(last, 0.0, src[HALO + 1:HALO + 1 + bm, :])
            else:
                src[HALO - 1:HALO, :] = jnp.where(starts, 0.0, src[HALO - 1:HALO, :])
                src[HALO + bm:HALO + bm + 1, :] = jnp.where(ends, 0.0, src[HALO + bm:HALO + bm + 1, :])
                prev = src[HALO - 1:HALO - 1 + bm, :]
                nxt = src[HALO + 1:HALO + 1 + bm, :]
            c = (prev * pair(cwg_ref, cwv_ref, slice(0, 1), k)
                 + src[HALO:HALO + bm, :] * pair(cwg_ref, cwv_ref, slice(1, 2), k)
                 + nxt * pair(cwg_ref, cwv_ref, slice(2, 3), k) + pair(cbg_ref, cbv_ref, slice(None), k))
            gate, val = c[:, :t], c[:, t:]
            o_ref[:, t * k:t * (k + 1)] = (gate * _sigmoid(gate) * val).astype(BF16)

    pl.when(i < lat_blocks)(lambda: run(False))
    pl.when(i >= lat_blocks)(lambda: run(True))


def _ffn_up(xa, mod_l, g, l, w_up, conv_w, conv_b, n_rows, n_lat, seq, ctx_len):
    rows = xa.shape[0]
    bm, bn = FFN_ROWS, FFN_COLS
    assert seq % bm == 0 and bm % ctx_len == 0 and n_lat % bm == 0 and n_rows % bm == 0
    blocks_per_batch, n_blocks, lat_blocks = seq // bm, n_rows // bm, n_lat // bm
    nj = D_FF // bn
    hb = bm // HALO
    last_halo = rows // HALO - 1
    assert seq & (seq - 1) == 0 and ctx_len & (ctx_len - 1) == 0
    return pl.pallas_call(
        functools.partial(_ffn_up_kernel, lat_blocks=lat_blocks, seq=seq, ctx_len=ctx_len),
        grid=(n_blocks, nj),
        in_specs=[
            pl.BlockSpec((bm, D_MODEL), lambda i, j: (i, 0)),
            pl.BlockSpec((HALO, D_MODEL), lambda i, j: (jnp.maximum(i * hb - 1, 0), 0)),
            pl.BlockSpec((HALO, D_MODEL), lambda i, j: (jnp.minimum((i + 1) * hb, last_halo), 0)),
            _mod_spec(3, blocks_per_batch),
            _mod_spec(4, blocks_per_batch),
            pl.BlockSpec((1, D_MODEL), lambda i, j: (0, 0)),
            pl.BlockSpec((None, D_MODEL, bn), lambda i, j: (l, 0, j)),
            pl.BlockSpec((None, D_MODEL, bn), lambda i, j: (l, 0, nj + j)),
            pl.BlockSpec((None, CONV_W, bn), lambda i, j: (l, 0, j)),
            pl.BlockSpec((None, CONV_W, bn), lambda i, j: (l, 0, nj + j)),
            pl.BlockSpec((None, 1, bn), lambda i, j: (l, 0, j)),
            pl.BlockSpec((None, 1, bn), lambda i, j: (l, 0, nj + j)),
        ],
        out_specs=pl.BlockSpec((bm, bn), lambda i, j: (i, j)),
        out_shape=jax.ShapeDtypeStruct((n_blocks * bm, D_FF), BF16),
        scratch_shapes=[pltpu.VMEM((bm + 2 * HALO, D_MODEL), BF16)]
        + [pltpu.VMEM((bm + 2 * HALO, 2 * FFN_TILE), F32)] * 2,
        compiler_params=_params("arbitrary", "arbitrary"),
        name="ffn_up_conv_gate",
    )(xa, xa, xa, mod_l, mod_l, g, w_up, w_up, conv_w, conv_w, conv_b, conv_b)


def _rope_table(seq, pad_rows):
    n = MLA_ROPE // 4
    freqs = ROPE_BASE ** (-jnp.arange(n, dtype=F32) / n)
    pos = jnp.arange(seq)
    ang_r = (pos // GRID_W).astype(F32)[:, None] * freqs
    ang_c = (pos % GRID_W).astype(F32)[:, None] * freqs
    cos = jnp.concatenate([jnp.cos(ang_r)] * 2 + [jnp.cos(ang_c)] * 2, axis=1)
    sin = jnp.concatenate([-jnp.sin(ang_r), jnp.sin(ang_r), -jnp.sin(ang_c), jnp.sin(ang_c)], axis=1)
    lat = jnp.concatenate([cos, sin], axis=1)
    ident = jnp.concatenate([jnp.ones((pad_rows, MLA_ROPE), F32), jnp.zeros((pad_rows, MLA_ROPE), F32)], axis=1)
    return jnp.concatenate([lat, ident], axis=0)


def _partner_perm():
    q = MLA_ROPE // 4
    return np.concatenate([np.arange(q, 2 * q), np.arange(0, q), np.arange(3 * q, 4 * q), np.arange(2 * q, 3 * q)])


def _layout_w_in(w, l):
    offs = np.cumsum((0, MLA_Q_RANK, MLA_KV_RANK, MLA_ROPE, NA_WIDTH, NA_WIDTH, NA_WIDTH, FN_WIDTH))
    cq, ckv, kr, qn, kn, vn, f = (w[l, :, a:b].astype(BF16) for a, b in zip(offs[:-1], offs[1:]))
    return jnp.concatenate([cq, ckv, qn, kn, vn, f, kr, kr[:, _partner_perm()]], axis=1)


def _layout_w_uq(w):
    w = w.reshape(MLA_Q_RANK, MLA_HEADS, MLA_NOPE + MLA_ROPE)
    rope = w[:, :, MLA_NOPE:]
    return jnp.concatenate([w, rope[:, :, _partner_perm()]], axis=2).reshape(MLA_Q_RANK, -1).astype(BF16)


def _layout_w_ukv(w):
    w = w.reshape(MLA_KV_RANK, MLA_HEADS, MLA_NOPE + MLA_V)
    return jnp.concatenate([w[:, :, :MLA_NOPE].reshape(MLA_KV_RANK, -1),
                            w[:, :, MLA_NOPE:].reshape(MLA_KV_RANK, -1)], axis=1).astype(BF16)


def kernel(x, c, ctx, c_ctx, w_mod, b_mod, g_attn, g_ffn, w_in, g_q, w_uq, g_kv, w_ukv, na_rpb, w_fnet, w_out,
           w_up, conv_w, conv_b, w_down, g_final):
    batch, seq, d = x.shape
    ctx_len = ctx.shape[1]
    depth = w_mod.shape[0]
    n_lat, n_ctx = batch * seq, batch * ctx_len
    bm = ROW_BLOCK
    lat_blocks = n_lat // bm
    all_blocks = (n_lat + n_ctx) // bm
    blocks_per_batch = seq // bm
    assert n_ctx % bm == 0 and n_lat // bm // blocks_per_batch == batch and batch < 8

    x_lat, x_ctx, ctx_block0 = x.reshape(n_lat, d), ctx.reshape(n_ctx, d), 0
    cin =jnp.zeros((8, d), F32).at[:batch].set(c).at[batch].set(c_ctx)
    mod = _modulation(cin, w_mod, b_mod)
    cs_tab = _rope_table(seq, bm)
    row = lambda v: v.reshape(1, -1)
    w_out_bf, w_up_bf, w_down_bf = w_out.astype(BF16), w_up.astype(BF16), w_down.astype(BF16)
    conv_b3 = conv_b.reshape(depth, 1, -1)

    for l in range(depth):
        ctx_out = l < depth - 1
        mod_l = mod[l].reshape(8, 1, 6 * d)
        oc, on, of, okr = _inproj(x_lat, x_ctx, ctx_block0, n_lat + n_ctx, lat_blocks, mod_l, row(g_attn[l]),
                                  _layout_w_in(w_in, l), blocks_per_batch)
        qm, km, vm = _mla_up(oc, okr, cs_tab, row(g_q[l]), row(g_kv[l]), _layout_w_uq(w_uq[l]),
                             _layout_w_ukv(w_ukv[l]), lat_blocks, blocks_per_batch)

        o_mla = _attention(qm, km, vm, batch=batch, heads=MLA_HEADS, dq=MLA_QK_PAD, dv=MLA_V, bq=2048, ck=1024,
                           seq=seq, ctx_len=ctx_len, k_col0=0, v_col0=0, name="mla_attention")
        o_na = _neighborhood_attention(on, _na_table(na_rpb[l], seq // GRID_W), batch, seq, ctx_len)
        o_fn = _fourier_latent(of, w_fnet[l].astype(BF16), batch, seq)
        ctx_ins = None
        if ctx_out:
            o_mla_c = _context_attention(
                qm, km, vm, batch=batch, heads=MLA_HEADS, dq=MLA_QK_PAD, dv=MLA_V, row0=n_lat, ctx_len=ctx_len,
                k_col0=0, v_col0=0, name="mla_attention_context")
            o_na_c = _context_attention(
                on, on, on, batch=batch, heads=NA_HEADS, dq=NA_HEAD_DIM, dv=NA_HEAD_DIM, row0=n_lat,
                ctx_len=ctx_len, k_col0=NA_HEADS, v_col0=2 * NA_HEADS, name="na_attention_context")
            o_fn_c = _fourier_context(of, w_fnet[l].astype(BF16), batch, n_lat, ctx_len)
            ctx_ins = (o_mla_c, o_na_c, o_fn_c)

        n_blocks = all_blocks if ctx_out else lat_blocks
        n_rows = n_blocks * bm
        unified = x_ctx is x_lat
        merge = ctx_out and not unified
        xa = _proj_residual(x_lat, (o_mla, o_na, o_fn), ctx_ins, w_out_bf, l, mod_l, 2, seq, n_rows, bm,
                            x_ctx=x_ctx if merge else None)
        a = _ffn_up(xa, mod_l, row(g_ffn[l]), l, w_up_bf, conv_w, conv_b3, n_rows, n_lat, seq, ctx_len)
        xa = _proj_residual(xa, (a,), None, w_down_bf, l, mod_l, 5, seq, n_rows, bm,
                            final_gain=None if ctx_out else row(g_final))
        if merge or unified:
            x_lat, x_ctx, ctx_block0 = xa, xa, lat_blocks
        else:
            x_lat = xa

    return xa.reshape(batch, seq, d)
```

```python
import functools

import numpy as np
import jax
import jax.numpy as jnp
from jax import lax
from jax.experimental import pallas as pl
from jax.experimental.pallas import tpu as pltpu

F32 = jnp.float32
BF16 = jnp.bfloat16

D_MODEL = 2048
GRID_W = 64
EPS = 1e-6

MLA_HEADS = 8
MLA_NOPE = 128
MLA_ROPE = 64
MLA_V = 128
MLA_Q_RANK = 512
MLA_KV_RANK = 512
MLA_QK_PAD = 256
MLA_SCALE = (MLA_NOPE + MLA_ROPE) ** -0.5
ROPE_BASE = 10000.0

NA_HEADS = 4
NA_HEAD_DIM = 128
NA_KH = 8
NA_KW = 16
NA_SCALE = NA_HEAD_DIM ** -0.5
NA_QROWS = 8
NA_KROWS = NA_QROWS + NA_KH - 1

FN_GROUPS = 4
FN_CH = 128
FN_WIDTH = FN_GROUPS * FN_CH
FN_K2 = 16

MLA_WIDTH = MLA_HEADS * MLA_V
NA_WIDTH = NA_HEADS * NA_HEAD_DIM
D_FF = 5632
FFN_TILE = 128
CONV_W = 3

ROW_BLOCK = 512
FFN_ROWS = 1024
FFN_COLS = 1408
HALO = 16
PROJ_COLS = 512
MASK_VALUE = -1e30
LOG2E = 1.4426950408889634
VMEM_LIMIT = 60 * 1024 * 1024


def _params(*sem, flags=None):
    return pltpu.CompilerParams(dimension_semantics=sem, vmem_limit_bytes=VMEM_LIMIT, flags=flags)


def _dot(a, b):
    return jnp.dot(a, b, preferred_element_type=F32)


def _dot_nt(a, b):
    return lax.dot_general(a, b, (((1,), (1,)), ((), ())), preferred_element_type=F32)


def _split(x):
    hi = x.astype(BF16)
    lo = (x - hi.astype(F32)).astype(BF16)
    return hi, lo


def _dot3_left(m_hi, m_lo, x):
    x_hi, x_lo = _split(x)
    return _dot(m_hi, x_hi) + (_dot(m_hi, x_lo) + _dot(m_lo, x_hi))


def _dot3_right(x, m_hi, m_lo):
    x_hi, x_lo = _split(x)
    return _dot(x_hi, m_hi) + (_dot(x_lo, m_hi) + _dot(x_hi, m_lo))


def _sigmoid(x):
    return 1.0 / (1.0 + jnp.exp(-x))


def _rms(x, g):
    y = x * lax.rsqrt(jnp.mean(x * x, axis=-1, keepdims=True) + EPS)
    return y * g


def _normmod(x, g, sc, sh):
    return _rms(x, g) * (1.0 + sc) + sh


def _mod_kernel(c_ref, w_ref, b_ref, o_ref):
    c = c_ref[...]
    s = (c * _sigmoid(c)).astype(BF16)
    o_ref[...] = _dot(s, w_ref[...].astype(BF16)) + b_ref[...]


def _modulation(cin, w_mod, b_mod):
    depth, d, n = w_mod.shape
    bn = 1024
    return pl.pallas_call(
        _mod_kernel,
        grid=(depth, n // bn),
        in_specs=[
            pl.BlockSpec((8, d), lambda l, j: (0, 0)),
            pl.BlockSpec((None, d, bn), lambda l, j: (l, 0, j)),
            pl.BlockSpec((None, 1, bn), lambda l, j: (l, 0, j)),
        ],
        out_specs=pl.BlockSpec((None, 8, bn), lambda l, j: (l, 0, j)),
        out_shape=jax.ShapeDtypeStruct((depth, 8, n), F32),
        compiler_params=_params("arbitrary", "arbitrary"),
        name="modulation",
    )(cin, w_mod, b_mod.reshape(depth, 1, n))


def _inproj_kernel(x_ref, xc_ref, sh_ref, sc_ref, g_ref, w_ref, oc_ref, on_ref, of_ref, okr_ref, *, lat_blocks):
    def run(src_ref):
        h = _normmod(src_ref[...], g_ref[...], sc_ref[...], sh_ref[...]).astype(BF16)
        oc_ref[...] = _dot(h, w_ref[:, 0:1024])
        on_ref[:, 0:NA_WIDTH] = (_dot(h, w_ref[:, 1024:1024 + NA_WIDTH]) * (NA_SCALE * LOG2E)).astype(BF16)
        on_ref[:, NA_WIDTH:] = _dot(h, w_ref[:, 1024 + NA_WIDTH:2560]).astype(BF16)
        of_ref[...] = _dot(h, w_ref[:, 2560:3072])
        okr_ref[...] = _dot(h, w_ref[:, 3072:3200])

    i = pl.program_id(0)
    pl.when(i < lat_blocks)(lambda: run(x_ref))
    pl.when(i >= lat_blocks)(lambda: run(xc_ref))


def _mod_spec(k, blocks_per_batch):
    return pl.BlockSpec((None, 1, D_MODEL), lambda i, *_: (i // blocks_per_batch, 0, k))


def _two_source_specs(bm, width, lat_blocks, ctx_block0):
    return [pl.BlockSpec((bm, width), lambda i, *_: (jnp.minimum(i, lat_blocks - 1), 0)),
            pl.BlockSpec((bm, width), lambda i, *_: (ctx_block0 + jnp.maximum(i - lat_blocks, 0), 0))]


def _inproj(x_lat, x_ctx, ctx_block0, rows, lat_blocks, mod_l, g, w_p, blocks_per_batch):
    bm = ROW_BLOCK
    row = lambda w: pl.BlockSpec((bm, w), lambda i: (i, 0))
    return pl.pallas_call(
        functools.partial(_inproj_kernel, lat_blocks=lat_blocks),
        grid=(rows // bm,),
        in_specs=_two_source_specs(bm, D_MODEL, lat_blocks, ctx_block0) + [
            _mod_spec(0, blocks_per_batch),
            _mod_spec(1, blocks_per_batch),
            pl.BlockSpec((1, D_MODEL), lambda i: (0, 0)),
            pl.BlockSpec(w_p.shape, lambda i: (0, 0), pipeline_mode=pl.Buffered(1)),
        ],
        out_specs=[row(1024), row(1536), row(512), row(128)],
        out_shape=[
            jax.ShapeDtypeStruct((rows, 1024), F32),
            jax.ShapeDtypeStruct((rows, 1536), BF16),
            jax.ShapeDtypeStruct((rows, 512), F32),
            jax.ShapeDtypeStruct((rows, 128), F32),
        ],
        compiler_params=_params("arbitrary"),
        name="in_projection",
    )(x_lat, x_ctx, mod_l, mod_l, g, w_p)


def _rope(t, cs):
    t = t * cs
    return t + pltpu.roll(t, 64, axis=1)


def _mla_up_kernel(c_ref, kr_ref, cs_ref, gq_ref, gkv_ref, wq_ref, wkv_ref, q_ref, k_ref, v_ref):
    cs = cs_ref[...]
    cq = _rms(c_ref[:, 0:MLA_Q_RANK], gq_ref[...]).astype(BF16)
    ckv = _rms(c_ref[:, MLA_Q_RANK:MLA_Q_RANK + MLA_KV_RANK], gkv_ref[...]).astype(BF16)
    q = _dot(cq, wq_ref[...]) * (MLA_SCALE * LOG2E)
    kv = _dot(ckv, wkv_ref[...])
    lane = lax.broadcasted_iota(jnp.int32, cs.shape, 1)
    k_rope = jnp.where(lane < MLA_ROPE, _rope(kr_ref[...], cs), 0.0).astype(BF16)
    for h in range(MLA_HEADS):
        o = h * MLA_QK_PAD
        q_ref[:, o:o + MLA_NOPE] = q[:, o:o + MLA_NOPE].astype(BF16)
        q_ref[:, o + MLA_NOPE:o + MLA_QK_PAD] = _rope(q[:, o + MLA_NOPE:o + MLA_QK_PAD], cs).astype(BF16)
        k_ref[:, o:o + MLA_NOPE] = kv[:, h * MLA_NOPE:(h + 1) * MLA_NOPE].astype(BF16)
        k_ref[:, o + MLA_NOPE:o + MLA_QK_PAD] = k_rope
    v_ref[...] = kv[:, MLA_HEADS * MLA_NOPE:].astype(BF16)


def _mla_up(oc, okr, cs_tab, g_q, g_kv, wq_p, wkv_p, lat_blocks, pos_blocks):
    rows = oc.shape[0]
    bm = ROW_BLOCK
    row = lambda w: pl.BlockSpec((bm, w), lambda i: (i, 0))
    const = lambda a: pl.BlockSpec(a.shape, lambda i: (0, 0))
    cs_spec = pl.BlockSpec((bm, 128), lambda i: (jnp.where(i < lat_blocks, i % pos_blocks, pos_blocks), 0))
    qk_w = MLA_HEADS * MLA_QK_PAD
    return pl.pallas_call(
        _mla_up_kernel,
        grid=(rows // bm,),
        in_specs=[row(1024), row(128), cs_spec, const(g_q), const(g_kv), const(wq_p), const(wkv_p)],
        out_specs=[row(qk_w), row(qk_w), row(MLA_WIDTH)],
        out_shape=[
            jax.ShapeDtypeStruct((rows, qk_w), BF16),
            jax.ShapeDtypeStruct((rows, qk_w), BF16),
            jax.ShapeDtypeStruct((rows, MLA_WIDTH), BF16),
        ],
        compiler_params=_params("arbitrary"),
        name="mla_up_projection",
    )(oc, okr, cs_tab, g_q, g_kv, wq_p, wkv_p)


def _with_ones(v):
    return jnp.concatenate([v, jnp.ones_like(v)], axis=1)


def _softmax_first(s, v, s_max=None):
    m = jnp.max(s, axis=-1, keepdims=True) if s_max is None else s_max
    return m, _dot(jnp.exp2(s - m).astype(BF16), _with_ones(v))


def _softmax_next(s, s_max, v, m, acc):
    m_new = jnp.maximum(m, s_max)
    p = jnp.exp2(s - m_new)
    return m_new, jnp.exp2(m - m_new) * acc + _dot(p.astype(BF16), _with_ones(v))


def _normalised(acc):
    dv = acc.shape[1] // 2
    return acc[:, :dv] / acc[:, dv:]


def _attention_kernel(q_ref, k_ref, v_ref, kc_ref, vc_ref, o_ref, sa_ref, sb_ref, *, ck, n_main):
    q = q_ref[...]
    bufs = (sa_ref, sb_ref)
    chunks = [(k_ref, v_ref, c * ck, ck) for c in range(n_main)] + [(kc_ref, vc_ref, 0, kc_ref.shape[0])]

    def scores(i):
        keys, _, start, size = chunks[i]
        s = _dot_nt(q, keys[start:start + size, :])
        bufs[i % 2][:, :size] = s
        return jnp.max(s, axis=-1, keepdims=True)

    s_max = scores(0)
    carry = None
    for i, (_, values, start, size) in enumerate(chunks):
        nxt_max = scores(i + 1) if i + 1 < len(chunks) else None
        s, v = bufs[i % 2][:, :size], values[start:start + size, :]
        carry = _softmax_first(s, v, s_max) if carry is None else _softmax_next(s, s_max, v, *carry)
        s_max = nxt_max
    o_ref[...] = _normalised(carry[1]).astype(o_ref.dtype)


def _attention(q_arr, k_arr, v_arr, *, batch, heads, dq, dv, bq, seq, ctx_len, k_col0, v_col0, name, ck=512):
    nq = seq // bq
    ctx0 = batch * seq // ctx_len
    return pl.pallas_call(
        functools.partial(_attention_kernel, ck=ck, n_main=seq // ck),
        grid=(batch, heads, nq),
        in_specs=[
            pl.BlockSpec((bq, dq), lambda b, h, i: (b * nq + i, h)),
            pl.BlockSpec((seq, dq), lambda b, h, i: (b, k_col0 + h)),
            pl.BlockSpec((seq, dv), lambda b, h, i: (b, v_col0 + h)),
            pl.BlockSpec((ctx_len, dq), lambda b, h, i: (ctx0 + b, k_col0 + h)),
            pl.BlockSpec((ctx_len, dv), lambda b, h, i: (ctx0 + b, v_col0 + h)),
        ],
        out_specs=pl.BlockSpec((bq, dv), lambda b, h, i: (b * nq + i, h)),
        out_shape=jax.ShapeDtypeStruct((batch * seq, heads * dv), BF16),
        scratch_shapes=[pltpu.VMEM((bq, ck), F32), pltpu.VMEM((bq, ck), F32)],
        compiler_params=_params("arbitrary", "arbitrary", "arbitrary"),
        name=name,
    )(q_arr, k_arr, v_arr, k_arr, v_arr)


def _context_attention_kernel(q_ref, k_ref, v_ref, o_ref):
    _, acc = _softmax_first(_dot_nt(q_ref[...], k_ref[...]), v_ref[...])
    o_ref[...] = _normalised(acc).astype(o_ref.dtype)


def _context_attention(q_arr, k_arr, v_arr, *, batch, heads, dq, dv, row0, ctx_len, k_col0, v_col0, name):
    blk0 = row0 // ctx_len
    return pl.pallas_call(
        _context_attention_kernel,
        grid=(batch, heads),
        in_specs=[
            pl.BlockSpec((ctx_len, dq), lambda b, h: (blk0 + b, h)),
            pl.BlockSpec((ctx_len, dq), lambda b, h: (blk0 + b, k_col0 + h)),
            pl.BlockSpec((ctx_len, dv), lambda b, h: (blk0 + b, v_col0 + h)),
        ],
        out_specs=pl.BlockSpec((ctx_len, dv), lambda b, h: (b, h)),
        out_shape=jax.ShapeDtypeStruct((batch * ctx_len, heads * dv), BF16),
        compiler_params=_params("arbitrary", "arbitrary"),
        name=name,
    )(q_arr, k_arr, v_arr)


def _na_kernel(q_ref, k_ref, v_ref, kc_ref, vc_ref, ta_ref, tb_ref, o_ref, *s_refs, grid_rows):
    bq = NA_QROWS * GRID_W
    nwin = NA_KROWS * GRID_W

    def scores(half, t_ref, sw_ref, sc_ref):
        r = pl.program_id(2) * 2 + half
        ks = jnp.clip(r * NA_QROWS - NA_KH // 2, 0, grid_rows - NA_KROWS)
        keys = pl.ds(pl.multiple_of(ks * GRID_W, GRID_W), nwin)
        q = q_ref[half * bq:(half + 1) * bq, :]
        s_win = _dot_nt(q, k_ref[keys, :]) + t_ref[...]
        s_ctx = _dot_nt(q, kc_ref[...])
        sw_ref[...] = s_win
        sc_ref[...] = s_ctx
        return keys, jnp.maximum(jnp.max(s_win, axis=-1, keepdims=True), jnp.max(s_ctx, axis=-1, keepdims=True))

    def finish(half, keys, m, sw_ref, sc_ref):
        p_win = jnp.exp2(sw_ref[...] - m).astype(BF16)
        p_ctx = jnp.exp2(sc_ref[...] - m).astype(BF16)
        acc = _dot(p_win, _with_ones(v_ref[keys, :])) + _dot(p_ctx, _with_ones(vc_ref[...]))
        o_ref[half * bq:(half + 1) * bq, :] = _normalised(acc).astype(o_ref.dtype)

    first = scores(0, ta_ref, *s_refs[0:2])
    second = scores(1, tb_ref, *s_refs[2:4])
    finish(0, *first, *s_refs[0:2])
    finish(1, *second, *s_refs[2:4])


def _na_table(rpb, grid_rows):
    w = np.arange(GRID_W)[:, None]
    j = np.arange(GRID_W)[None, :]
    cs = np.clip(w - NA_KW // 2, 0, GRID_W - NA_KW)
    sel_c = ((j - w + (NA_KW - 1))[:, :, None] == np.arange(2 * NA_KW - 1)).astype(np.float32)
    by_col = jnp.einsum("hpq,wjq->hpwj", rpb.astype(F32), sel_c, precision=lax.Precision.HIGHEST)
    by_col = jnp.where((j >= cs) & (j < cs + NA_KW), by_col * LOG2E, MASK_VALUE)
    masked = jnp.full((rpb.shape[0], GRID_W, GRID_W), MASK_VALUE, F32)
    tabs = []
    for r0 in (0, NA_QROWS, grid_rows - NA_QROWS):
        ks = min(max(r0 - NA_KH // 2, 0), grid_rows - NA_KROWS)
        rows = []
        for r in range(r0, r0 + NA_QROWS):
            rs = min(max(r - NA_KH // 2, 0), grid_rows - NA_KH)
            tiles = [by_col[:, kr - r + NA_KH - 1] if rs <= kr < rs + NA_KH else masked
                     for kr in range(ks, ks + NA_KROWS)]
            rows.append(jnp.concatenate(tiles, axis=-1))
        tabs.append(jnp.concatenate(rows, axis=1))
    return jnp.stack(tabs)


def _neighborhood_attention(on, table, batch, seq, ctx_len):
    grid_rows = seq // GRID_W
    nstep = grid_rows // (2 * NA_QROWS)
    bq = NA_QROWS * GRID_W
    nwin = NA_KROWS * GRID_W
    d = NA_HEAD_DIM
    ctx0 = batch * seq // ctx_len
    assert nstep >= 2
    return pl.pallas_call(
        functools.partial(_na_kernel, grid_rows=grid_rows),
        grid=(batch, NA_HEADS, nstep),
        in_specs=[
            pl.BlockSpec((2 * bq, d), lambda b, h, r: (b * nstep + r, h)),
            pl.BlockSpec((seq, d), lambda b, h, r: (b, NA_HEADS + h)),
            pl.BlockSpec((seq, d), lambda b, h, r: (b, 2 * NA_HEADS + h)),
            pl.BlockSpec((ctx_len, d), lambda b, h, r: (ctx0 + b, NA_HEADS + h)),
            pl.BlockSpec((ctx_len, d), lambda b, h, r: (ctx0 + b, 2 * NA_HEADS + h)),
            pl.BlockSpec((None, None, bq, nwin), lambda b, h, r: (jnp.where(r == 0, 0, 1), h, 0, 0)),
            pl.BlockSpec((None, None, bq, nwin), lambda b, h, r: (jnp.where(r == nstep - 1, 2, 1), h, 0, 0)),
        ],
        out_specs=pl.BlockSpec((2 * bq, d), lambda b, h, r: (b * nstep + r, h)),
        out_shape=jax.ShapeDtypeStruct((batch * seq, NA_WIDTH), BF16),
        scratch_shapes=[pltpu.VMEM((bq, nwin), F32), pltpu.VMEM((bq, ctx_len), F32)] * 2,
        compiler_params=_params("arbitrary", "arbitrary", "arbitrary"),
        name="neighborhood_attention",
    )(on, on, on, on, on, table, table)


def _dft_consts(n):
    jk = (np.arange(n)[:, None] * np.arange(n)[None, :]) % n
    ang = 2.0 * np.pi * jk / n
    return np.cos(ang), np.sin(ang)


def _hi_lo(m):
    m = jnp.asarray(m, F32)
    hi = m.astype(BF16)
    return hi, (m - hi.astype(F32)).astype(BF16)


def _fn_stage1_kernel(x_ref, fh_ref, fl_ref, tc_ref, ts_ref, o_ref):
    n = GRID_W
    b = _dot3_left(fh_ref[...], fl_ref[...], x_ref[...])
    br, bi = b[:n], b[n:]
    tc, ts = tc_ref[...], ts_ref[...]
    o_ref[0] = br * tc + bi * ts
    o_ref[1] = bi * tc - br * ts


def _fn_stage2_kernel(t_ref, gh_ref, gl_ref, ch_ref, cl_ref, w_ref, p_ref, o_ref):
    n = GRID_W
    outs = []
    for j in range(FN_K2):
        t = jnp.concatenate([t_ref[0, j], t_ref[1, j]], axis=0)
        a = _dot3_left(gh_ref[...], gl_ref[...], t)
        outs.append(a)
    ar = jnp.concatenate([a[:n] for a in outs], axis=0)
    ai = jnp.concatenate([a[n:] for a in outs], axis=0)
    ys = []
    for g in range(FN_GROUPS):
        sl = slice(g * FN_CH, (g + 1) * FN_CH)
        z = jnp.concatenate([ar[:, sl], ai[:, sl]], axis=1)
        spec = _dot3_right(z, ch_ref[...], cl_ref[...])
        ys.append(_dot(spec.astype(BF16), w_ref[g]).astype(BF16))
    y = jnp.concatenate(ys, axis=1)
    y = _dot(p_ref[...], y).astype(BF16)
    o_ref[...] = y.reshape(n, FN_K2, FN_WIDTH)


def _fourier_latent(of, w_fnet, batch, seq):
    n = GRID_W
    assert seq == n * n
    cols = n * FN_WIDTH
    c64, s64 = _dft_consts(n)
    f1h, f1l = _hi_lo(np.concatenate([c64, -s64], axis=0))
    tw = 2.0 * np.pi * (np.arange(n)[:, None] * np.arange(n)[None, :]) / (n * n)
    tc = jnp.asarray(np.repeat(np.cos(tw).reshape(n, n, 1), FN_WIDTH, axis=2).reshape(n, cols), F32)
    ts = jnp.asarray(np.repeat(np.sin(tw).reshape(n, n, 1), FN_WIDTH, axis=2).reshape(n, cols), F32)
    x2 = of.reshape(of.shape[0] // n, cols)
    cb = 4096
    t = pl.pallas_call(
        _fn_stage1_kernel,
        grid=(batch, cols // cb),
        in_specs=[
            pl.BlockSpec((n, cb), lambda b, j: (b, j)),
            pl.BlockSpec(f1h.shape, lambda b, j: (0, 0)),
            pl.BlockSpec(f1l.shape, lambda b, j: (0, 0)),
            pl.BlockSpec((n, cb), lambda b, j: (0, j)),
            pl.BlockSpec((n, cb), lambda b, j: (0, j)),
        ],
        out_specs=pl.BlockSpec((None, 2, n, cb), lambda b, j: (b, 0, 0, j)),
        out_shape=jax.ShapeDtypeStruct((batch, 2, n, cols), F32),
        compiler_params=_params("arbitrary", "arbitrary"),
        name="fourier_rows",
    )(x2, f1h, f1l, tc, ts)
    t = t.reshape(batch, 2, n, n, FN_WIDTH)

    g2h, g2l = _hi_lo(np.block([[c64, s64], [-s64, c64]]))
    cc, sc = _dft_consts(FN_CH)
    norm = 1.0 / np.sqrt(seq * FN_CH)
    c4h, c4l = _hi_lo(np.concatenate([cc, sc], axis=0) * norm)
    rows = FN_K2 * n
    perm = np.zeros((rows, rows), np.float32)
    k1 = np.arange(n)[:, None]
    j = np.arange(FN_K2)[None, :]
    perm[(k1 * FN_K2 + j).ravel(), (j * n + k1).ravel()] = 1.0
    perm = jnp.asarray(perm, BF16)
    const = lambda a: pl.BlockSpec(a.shape, lambda b, i: (0,) * a.ndim)
    y = pl.pallas_call(
        _fn_stage2_kernel,
        grid=(batch, n // FN_K2),
        in_specs=[
            pl.BlockSpec((None, 2, FN_K2, n, FN_WIDTH), lambda b, i: (b, 0, i, 0, 0)),
            const(g2h), const(g2l), const(c4h), const(c4l), const(w_fnet), const(perm),
        ],
        out_specs=pl.BlockSpec((None, n, FN_K2, FN_WIDTH), lambda b, i: (b, 0, i, 0)),
        out_shape=jax.ShapeDtypeStruct((batch, n, n, FN_WIDTH), BF16),
        compiler_params=_params("arbitrary", "arbitrary"),
        name="fourier_cols_channels",
    )(t, g2h, g2l, c4h, c4l, w_fnet, perm)
    return y.reshape(batch * seq, FN_WIDTH)


def _fn_ctx_kernel(x_ref, fh_ref, fl_ref, ch_ref, cl_ref, w_ref, o_ref, *, n):
    a = _dot3_left(fh_ref[...], fl_ref[...], x_ref[...])
    ar, ai = a[:n], a[n:]
    ys = []
    for g in range(FN_GROUPS):
        sl = slice(g * FN_CH, (g + 1) * FN_CH)
        z = jnp.concatenate([ar[:, sl], ai[:, sl]], axis=1)
        spec = _dot3_right(z, ch_ref[...], cl_ref[...])
        ys.append(_dot(spec.astype(BF16), w_ref[g]).astype(BF16))
    o_ref[...] = jnp.concatenate(ys, axis=1)


def _fourier_context(of, w_fnet, batch, row0, n):
    c, s = _dft_consts(n)
    fh, fl = _hi_lo(np.concatenate([c, -s], axis=0))
    cc, sc = _dft_consts(FN_CH)
    c4h, c4l = _hi_lo(np.concatenate([cc, sc], axis=0) / np.sqrt(n * FN_CH))
    const = lambda a: pl.BlockSpec(a.shape, lambda b: (0,) * a.ndim)
    blk0 = row0 // n
    return pl.pallas_call(
        functools.partial(_fn_ctx_kernel, n=n),
        grid=(batch,),
        in_specs=[pl.BlockSpec((n, FN_WIDTH), lambda b: (blk0 + b, 0)),
                  const(fh), const(fl), const(c4h), const(c4l), const(w_fnet)],
        out_specs=pl.BlockSpec((n, FN_WIDTH), lambda b: (b, 0)),
        out_shape=jax.ShapeDtypeStruct((batch * n, FN_WIDTH), BF16),
        compiler_params=_params("arbitrary"),
        name="fourier_context",
    )(of, fh, fl, c4h, c4l, w_fnet)


def _proj_residual_kernel(*refs, n_in, lat_blocks, has_ctx, split_x, final_norm):
    lat = refs[:n_in]
    ctx = refs[n_in:2 * n_in] if has_ctx else ()
    rest = refs[(2 if has_ctx else 1) * n_in:]
    w_refs, rest = rest[:n_in], rest[n_in:]
    x_ref, xc_ref = (rest[0], rest[1]) if split_x else (rest[0], rest[0])
    gt_ref, *gf_ref, o_ref = rest[2 if split_x else 1:]

    def run(a_refs, res_ref):
        for n in range(0, D_MODEL, PROJ_COLS):
            cols = slice(n, n + PROJ_COLS)
            acc = None
            for a_ref, w_ref in zip(a_refs, w_refs):
                t = _dot(a_ref[...], w_ref[:, cols])
                acc = t if acc is None else acc + t
            o_ref[:, cols] = res_ref[:, cols] + gt_ref[:, cols] * acc

    if has_ctx:
        i = pl.program_id(0)
        pl.when(i < lat_blocks)(lambda: run(lat, x_ref))
        pl.when(i >= lat_blocks)(lambda: run(ctx, xc_ref))
    else:
        run(lat, x_ref)
    if final_norm:
        o_ref[...] = _rms(o_ref[...], gf_ref[0][...])


def _proj_residual(xa, lat_ins, ctx_ins, w, l, mod_l, gate_k, rows_per_batch, n_rows, bm, final_gain=None,
                   x_ctx=None):
    n_in = len(lat_ins)
    has_ctx = ctx_ins is not None
    split_x = x_ctx is not None
    lat_blocks = lat_ins[0].shape[0] // bm
    bpb = rows_per_batch // bm
    in_specs = [pl.BlockSpec((bm, a.shape[1]), lambda i: (jnp.minimum(i, lat_blocks - 1), 0)) for a in lat_ins]
    if has_ctx:
        in_specs += [pl.BlockSpec((bm, a.shape[1]), lambda i: (jnp.maximum(i - lat_blocks, 0), 0)) for a in ctx_ins]
    offs = np.cumsum([0] + [a.shape[1] for a in lat_ins])
    assert offs[-1] == w.shape[1] and all(o % a.shape[1] == 0 for o, a in zip(offs, lat_ins))
    in_specs += [pl.BlockSpec((None, a.shape[1], D_MODEL), lambda i, k=int(o) // a.shape[1]: (l, k, 0),
                              pipeline_mode=pl.Buffered(1)) for o, a in zip(offs, lat_ins)]
    x_index = len(in_specs)
    if split_x:
        assert has_ctx and xa.shape[0] == lat_blocks * bm
        in_specs += _two_source_specs(bm, D_MODEL, lat_blocks, 0)
    else:
        in_specs.append(pl.BlockSpec((bm, D_MODEL), lambda i: (i, 0)))
    in_specs.append(pl.BlockSpec((None, 1, D_MODEL), lambda i: (i // bpb, 0, gate_k)))
    args = list(lat_ins) + (list(ctx_ins) if has_ctx else []) + [w] * n_in + [xa] + ([x_ctx] if split_x else [])
    args.append(mod_l)
    final_norm = final_gain is not None
    if final_norm:
        in_specs.append(pl.BlockSpec((1, D_MODEL), lambda i: (0, 0)))
        args.append(final_gain)
    in_place = not (final_norm or split_x)
    return pl.pallas_call(
        functools.partial(_proj_residual_kernel, n_in=n_in, lat_blocks=lat_blocks, has_ctx=has_ctx,
                          split_x=split_x, final_norm=final_norm),
        grid=(n_rows // bm,),
        in_specs=in_specs,
        out_specs=pl.BlockSpec((bm, D_MODEL), lambda i: (i, 0)),
        out_shape=jax.ShapeDtypeStruct(xa.shape if in_place else (n_rows, D_MODEL), F32),
        input_output_aliases={x_index: 0} if in_place else {},
        compiler_params=_params("arbitrary"),
        name="projection_residual",
    )(*args)


def _ffn_up_kernel(xm_ref, xp_ref, xn_ref, sh_ref, sc_ref, g_ref, wg_ref, wv_ref, cwg_ref, cwv_ref, cbg_ref,
                   cbv_ref, o_ref, h_ref, *u_refs, lat_blocks, seq, ctx_len):
    bm = o_ref.shape[0]
    i = pl.program_id(0)

    @pl.when(pl.program_id(1) == 0)
    def _():
        g, sc, sh = g_ref[...], sc_ref[...], sh_ref[...]
        h_ref[0:HALO, :] = _normmod(xp_ref[...], g, sc, sh).astype(BF16)
        h_ref[HALO:HALO + bm, :] = _normmod(xm_ref[...], g, sc, sh).astype(BF16)
        h_ref[HALO + bm:, :] = _normmod(xn_ref[...], g, sc, sh).astype(BF16)

    t = FFN_TILE
    n_tiles = o_ref.shape[1] // t

    def pair(g_ref, v_ref, rows, k):
        cols = slice(t * k, t * (k + 1))
        return jnp.concatenate([g_ref[rows, cols], v_ref[rows, cols]], axis=1)

    def matmul(k):
        u_refs[k % 2][...] = _dot(h_ref[...], pair(wg_ref, wv_ref, slice(None), k))

    def run(interior_boundaries):
        if interior_boundaries:
            rows = lax.broadcasted_iota(jnp.int32, (bm, 1), 0)
            first = (rows & (ctx_len - 1)) == 0
            last = ((rows + 1) & (ctx_len - 1)) == 0
        else:
            starts = (i * bm) % seq == 0
            ends = ((i + 1) * bm) % seq == 0
        matmul(0)
        for k in range(n_tiles):
            if k + 1 < n_tiles:
                matmul(k + 1)
            src = u_refs[k % 2]
            if interior_boundaries:
                prev = jnp.where(first, 0.0, src[HALO - 1:HALO - 1 + bm, :])
                nxt = jnp.where(last, 0.0, src[HALO + 1:HALO + 1 + bm, :])
            else:
                src[HALO - 1:HALO, :] = jnp.where(starts, 0.0, src[HALO - 1:HALO, :])
                src[HALO + bm:HALO + bm + 1, :] = jnp.where(ends, 0.0, src[HALO + bm:HALO + bm + 1, :])
                prev = src[HALO - 1:HALO - 1 + bm, :]
                nxt = src[HALO + 1:HALO + 1 + bm, :]
            c = (prev * pair(cwg_ref, cwv_ref, slice(0, 1), k)
                 + src[HALO:HALO + bm, :] * pair(cwg_ref, cwv_ref, slice(1, 2), k)
                 + nxt * pair(cwg_ref, cwv_ref, slice(2, 3), k) + pair(cbg_ref, cbv_ref, slice(None), k))
            gate, val = c[:, :t], c[:, t:]
            o_ref[:, t * k:t * (k + 1)] = (gate * _sigmoid(gate) * val).astype(BF16)

    pl.when(i < lat_blocks)(lambda: run(False))
    pl.when(i >= lat_blocks)(lambda: run(True))


def _ffn_up(xa, mod_l, g, l, w_up, conv_w, conv_b, n_rows, n_lat, seq, ctx_len):
    rows = xa.shape[0]
    bm, bn = FFN_ROWS, FFN_COLS
    assert seq % bm == 0 and bm % ctx_len == 0 and n_lat % bm == 0 and n_rows % bm == 0
    blocks_per_batch, n_blocks, lat_blocks = seq // bm, n_rows // bm, n_lat // bm
    nj = D_FF // bn
    hb = bm // HALO
    last_halo = rows // HALO - 1
    assert seq & (seq - 1) == 0 and ctx_len & (ctx_len - 1) == 0
    return pl.pallas_call(
        functools.partial(_ffn_up_kernel, lat_blocks=lat_blocks, seq=seq, ctx_len=ctx_len),
        grid=(n_blocks, nj),
        in_specs=[
            pl.BlockSpec((bm, D_MODEL), lambda i, j: (i, 0)),
            pl.BlockSpec((HALO, D_MODEL), lambda i, j: (jnp.maximum(i * hb - 1, 0), 0)),
            pl.BlockSpec((HALO, D_MODEL), lambda i, j: (jnp.minimum((i + 1) * hb, last_halo), 0)),
            _mod_spec(3, blocks_per_batch),
            _mod_spec(4, blocks_per_batch),
            pl.BlockSpec((1, D_MODEL), lambda i, j: (0, 0)),
            pl.BlockSpec((None, D_MODEL, bn), lambda i, j: (l, 0, j)),
            pl.BlockSpec((None, D_MODEL, bn), lambda i, j: (l, 0, nj + j)),
            pl.BlockSpec((None, CONV_W, bn), lambda i, j: (l, 0, j)),
            pl.BlockSpec((None, CONV_W, bn), lambda i, j: (l, 0, nj + j)),
            pl.BlockSpec((None, 1, bn), lambda i, j: (l, 0, j)),
            pl.BlockSpec((None, 1, bn), lambda i, j: (l, 0, nj + j)),
        ],
        out_specs=pl.BlockSpec((bm, bn), lambda i, j: (i, j)),
        out_shape=jax.ShapeDtypeStruct((n_blocks * bm, D_FF), BF16),
        scratch_shapes=[pltpu.VMEM((bm + 2 * HALO, D_MODEL), BF16)]
        + [pltpu.VMEM((bm + 2 * HALO, 2 * FFN_TILE), F32)] * 2,
        compiler_params=_params("arbitrary", "arbitrary"),
        name="ffn_up_conv_gate",
    )(xa, xa, xa, mod_l, mod_l, g, w_up, w_up, conv_w, conv_w, conv_b, conv_b)


def _rope_table(seq, pad_rows):
    n = MLA_ROPE // 4
    freqs = ROPE_BASE ** (-jnp.arange(n, dtype=F32) / n)
    pos = jnp.arange(seq)
    ang_r = (pos // GRID_W).astype(F32)[:, None] * freqs
    ang_c = (pos % GRID_W).astype(F32)[:, None] * freqs
    cos = jnp.concatenate([jnp.cos(ang_r)] * 2 + [jnp.cos(ang_c)] * 2, axis=1)
    sin = jnp.concatenate([-jnp.sin(ang_r), jnp.sin(ang_r), -jnp.sin(ang_c), jnp.sin(ang_c)], axis=1)
    lat = jnp.concatenate([cos, sin], axis=1)
    ident = jnp.concatenate([jnp.ones((pad_rows, MLA_ROPE), F32), jnp.zeros((pad_rows, MLA_ROPE), F32)], axis=1)
    return jnp.concatenate([lat, ident], axis=0)


def _partner_perm():
    q = MLA_ROPE // 4
    return np.concatenate([np.arange(q, 2 * q), np.arange(0, q), np.arange(3 * q, 4 * q), np.arange(2 * q, 3 * q)])


def _layout_w_in(w, l):
    offs = np.cumsum((0, MLA_Q_RANK, MLA_KV_RANK, MLA_ROPE, NA_WIDTH, NA_WIDTH, NA_WIDTH, FN_WIDTH))
    cq, ckv, kr, qn, kn, vn, f = (w[l, :, a:b].astype(BF16) for a, b in zip(offs[:-1], offs[1:]))
    return jnp.concatenate([cq, ckv, qn, kn, vn, f, kr, kr[:, _partner_perm()]], axis=1)


def _layout_w_uq(w):
    w = w.reshape(MLA_Q_RANK, MLA_HEADS, MLA_NOPE + MLA_ROPE)
    rope = w[:, :, MLA_NOPE:]
    return jnp.concatenate([w, rope[:, :, _partner_perm()]], axis=2).reshape(MLA_Q_RANK, -1).astype(BF16)


def _layout_w_ukv(w):
    w = w.reshape(MLA_KV_RANK, MLA_HEADS, MLA_NOPE + MLA_V)
    return jnp.concatenate([w[:, :, :MLA_NOPE].reshape(MLA_KV_RANK, -1),
                            w[:, :, MLA_NOPE:].reshape(MLA_KV_RANK, -1)], axis=1).astype(BF16)


def kernel(x, c, ctx, c_ctx, w_mod, b_mod, g_attn, g_ffn, w_in, g_q, w_uq, g_kv, w_ukv, na_rpb, w_fnet, w_out,
           w_up, conv_w, conv_b, w_down, g_final):
    batch, seq, d = x.shape
    ctx_len = ctx.shape[1]
    depth = w_mod.shape[0]
    n_lat, n_ctx = batch * seq, batch * ctx_len
    bm = ROW_BLOCK
    lat_blocks = n_lat // bm
    all_blocks = (n_lat + n_ctx) // bm
    blocks_per_batch = seq // bm
    assert n_ctx % bm == 0 and n_lat // bm // blocks_per_batch == batch and batch < 8

    x_lat, x_ctx, ctx_block0 = x.reshape(n_lat, d), ctx.reshape(n_ctx, d), 0
    cin =jnp.zeros((8, d), F32).at[:batch].set(c).at[batch].set(c_ctx)
    mod = _modulation(cin, w_mod, b_mod)
    cs_tab = _rope_table(seq, bm)
    row = lambda v: v.reshape(1, -1)
    w_out_bf, w_up_bf, w_down_bf = w_out.astype(BF16), w_up.astype(BF16), w_down.astype(BF16)
    conv_b3 = conv_b.reshape(depth, 1, -1)

    for l in range(depth):
        ctx_out = l < depth - 1
        mod_l = mod[l].reshape(8, 1, 6 * d)
        oc, on, of, okr = _inproj(x_lat, x_ctx, ctx_block0, n_lat + n_ctx, lat_blocks, mod_l, row(g_attn[l]),
                                  _layout_w_in(w_in, l), blocks_per_batch)
        qm, km, vm = _mla_up(oc, okr, cs_tab, row(g_q[l]), row(g_kv[l]), _layout_w_uq(w_uq[l]),
                             _layout_w_ukv(w_ukv[l]), lat_blocks, blocks_per_batch)

        o_mla = _attention(qm, km, vm, batch=batch, heads=MLA_HEADS, dq=MLA_QK_PAD, dv=MLA_V, bq=2048, ck=2048,
                           seq=seq, ctx_len=ctx_len, k_col0=0, v_col0=0, name="mla_attention")
        o_na = _neighborhood_attention(on, _na_table(na_rpb[l], seq // GRID_W), batch, seq, ctx_len)
        o_fn = _fourier_latent(of, w_fnet[l].astype(BF16), batch, seq)
        ctx_ins = None
        if ctx_out:
            o_mla_c = _context_attention(
                qm, km, vm, batch=batch, heads=MLA_HEADS, dq=MLA_QK_PAD, dv=MLA_V, row0=n_lat, ctx_len=ctx_len,
                k_col0=0, v_col0=0, name="mla_attention_context")
            o_na_c = _context_attention(
                on, on, on, batch=batch, heads=NA_HEADS, dq=NA_HEAD_DIM, dv=NA_HEAD_DIM, row0=n_lat,
                ctx_len=ctx_len, k_col0=NA_HEADS, v_col0=2 * NA_HEADS, name="na_attention_context")
            o_fn_c = _fourier_context(of, w_fnet[l].astype(BF16), batch, n_lat, ctx_len)
            ctx_ins = (o_mla_c, o_na_c, o_fn_c)

        n_blocks = all_blocks if ctx_out else lat_blocks
        n_rows = n_blocks * bm
        unified = x_ctx is x_lat
        merge = ctx_out and not unified
        xa = _proj_residual(x_lat, (o_mla, o_na, o_fn), ctx_ins, w_out_bf, l, mod_l, 2, seq, n_rows, bm,
                            x_ctx=x_ctx if merge else None)
        a = _ffn_up(xa, mod_l, row(g_ffn[l]), l, w_up_bf, conv_w, conv_b3, n_rows, n_lat, seq, ctx_len)
        xa = _proj_residual(xa, (a,), None, w_down_bf, l, mod_l, 5, seq, n_rows, bm,
                            final_gain=None if ctx_out else row(g_final))
        if merge or unified:
            x_lat, x_ctx, ctx_block0 = xa, xa, lat_blocks
        else:
            x_lat = xa

    return xa.reshape(batch, seq, d)
```
